```python
import jax
import jax.numpy as jnp
from jax import lax
import numpy as np

D_MODEL = 2048
BATCH = 8
SEQ = 4096
DEPTH = 4

GRID_W = 64
CTX_LEN = 256
NORM_EPS = 1e-6
NEG_INF = -1e30
N_MOD = 6

NA_HEADS = 8
NA_HEAD_DIM = 128
NA_WIDTH = NA_HEADS * NA_HEAD_DIM
NA_WIN_H = 8
NA_WIN_W = 16

LRU_WIDTH = 1024
LRU_BLOCKS = 8
LRU_BLOCK_DIM = LRU_WIDTH // LRU_BLOCKS
LRU_CONV_W = 4
LRU_C = 8.0

MLA_HEADS = 8
MLA_Q_RANK = 512
MLA_KV_RANK = 256
MLA_NOPE_DIM = 128
MLA_ROPE_DIM = 64
MLA_V_DIM = 128
MLA_QK_DIM = MLA_NOPE_DIM + MLA_ROPE_DIM
MLA_WIDTH = MLA_HEADS * MLA_V_DIM
ROPE_THETA = 10000.0
Q_BLOCK = 128

N_BRANCH = 3
BRANCH_WIDTH = 1024
D_IN = 3 * NA_WIDTH + 2 * LRU_WIDTH + MLA_Q_RANK + MLA_KV_RANK + MLA_ROPE_DIM + N_BRANCH * D_MODEL

D_FF = 5632
FFN_CONV_W = 3

kernel_name = 'hybrid_natten_rglru_mla_prefix_dit_block'


def rmsnorm(x, g):
    xf = x.astype(jnp.float32)
    y = xf * lax.rsqrt(jnp.mean(xf * xf, axis=-1, keepdims=True) + NORM_EPS)
    return (y * g.astype(jnp.float32)).astype(x.dtype)


def modulate(h, shift, scale):
    return h * (1 + scale) + shift


def depthwise_conv(x, w, b):
    width = w.shape[0]
    n = x.shape[1]
    pad_left = width // 2
    xp = jnp.pad(x, ((0, 0), (pad_left, width - 1 - pad_left), (0, 0)))
    return sum(xp[:, i:i + n] * w[i] for i in range(width)) + b


def axial_rope_angles(n_tok, dim):
    t = jnp.arange(n_tok, dtype=jnp.int32)
    row = (t // GRID_W).astype(jnp.float32)
    col = (t % GRID_W).astype(jnp.float32)
    n_freq = dim // 4
    inv_freq = ROPE_THETA ** (-jnp.arange(n_freq, dtype=jnp.float32) / n_freq)
    ang = jnp.concatenate([row[:, None] * inv_freq, col[:, None] * inv_freq], axis=-1)
    return jnp.cos(ang), jnp.sin(ang)


def apply_rope(x, cos, sin):
    half = x.shape[-1] // 2
    xf = x.astype(jnp.float32)
    x1, x2 = xf[..., :half], xf[..., half:]
    return jnp.concatenate([x1 * cos - x2 * sin, x1 * sin + x2 * cos], axis=-1).astype(x.dtype)


def split_in_proj(z):
    sizes = (3 * NA_WIDTH, LRU_WIDTH, LRU_WIDTH, MLA_Q_RANK, MLA_KV_RANK, MLA_ROPE_DIM)
    offsets = [int(o) for o in np.cumsum(sizes)]
    return jnp.split(z, offsets, axis=-1)


def softmax_attention(q, k, v):
    scale = q.shape[-1] ** -0.5
    s = jnp.einsum('bqhd,bkhd->bhqk', q, k).astype(jnp.float32) * scale
    p = jax.nn.softmax(s, axis=-1).astype(v.dtype)
    o = jnp.einsum('bhqk,bkhd->bqhd', p, v)
    return o.reshape(o.shape[0], o.shape[1], -1)


def neighbourhood_attention(q, k, v, k_ctx, v_ctx, rpb):
    bsz, n, heads, hd = q.shape
    rows = n // GRID_W
    kh = min(NA_WIN_H, rows)
    r = np.arange(rows)
    key_rows = np.clip(r - kh // 2, 0, rows - kh)[:, None] + np.arange(kh)[None, :]
    row_idx = key_rows - r[:, None] + (NA_WIN_H - 1)
    cidx = np.arange(GRID_W)
    c_start = np.clip(cidx - NA_WIN_W // 2, 0, GRID_W - NA_WIN_W)
    in_win = (cidx[None, :] >= c_start[:, None]) & (cidx[None, :] < c_start[:, None] + NA_WIN_W)
    col_idx = np.clip(cidx[None, :] - cidx[:, None], -(NA_WIN_W - 1), NA_WIN_W - 1) + (NA_WIN_W - 1)
    bias = rpb.astype(jnp.float32)[:, row_idx][..., col_idx]
    bias = jnp.where(in_win[None, None, :, None, :], bias.transpose(0, 1, 3, 2, 4), NEG_INF)
    scale = hd ** -0.5
    qg = q.reshape(bsz, rows, GRID_W, heads, hd)
    kg = jnp.take(k.reshape(bsz, rows, GRID_W, heads, hd), key_rows, axis=1)
    vg = jnp.take(v.reshape(bsz, rows, GRID_W, heads, hd), key_rows, axis=1)
    s_win = jnp.einsum('brqhd,brjkhd->bhrqjk', qg, kg).astype(jnp.float32) * scale + bias
    s_ctx = jnp.einsum('brqhd,blhd->bhrql', qg, k_ctx).astype(jnp.float32) * scale
    n_win = kh * GRID_W
    s = jnp.concatenate([s_win.reshape(bsz, heads, rows, GRID_W, n_win), s_ctx], axis=-1)
    p = jax.nn.softmax(s, axis=-1).astype(v.dtype)
    p_win = p[..., :n_win].reshape(bsz, heads, rows, GRID_W, kh, GRID_W)
    o = jnp.einsum('bhrqjk,brjkhd->brqhd', p_win, vg) + jnp.einsum('bhrql,blhd->brqhd', p[..., n_win:], v_ctx)
    return o.reshape(bsz, n, heads * hd)


def rglru_coeffs(x, w_a, b_a, w_x, b_x, lam):
    bsz, n, _ = x.shape
    xb = x.reshape(bsz, n, LRU_BLOCKS, LRU_BLOCK_DIM)
    gate_a = jnp.einsum('bnkc,kcd->bnkd', xb, w_a).reshape(bsz, n, LRU_WIDTH) + b_a
    gate_x = jnp.einsum('bnkc,kcd->bnkd', xb, w_x).reshape(bsz, n, LRU_WIDTH) + b_x
    r = jax.nn.sigmoid(gate_a.astype(jnp.float32))
    i = jax.nn.sigmoid(gate_x.astype(jnp.float32))
    log_a = -LRU_C * r * jax.nn.softplus(-lam.astype(jnp.float32))
    a = jnp.exp(log_a)
    b = jnp.sqrt(-jnp.expm1(2.0 * log_a)) * (i * x.astype(jnp.float32))
    return a, b


def _combine(e1, e2):
    a1, b1 = e1
    a2, b2 = e2
    return a1 * a2, a2 * b1 + b2


def linear_recurrence(a, b, h0):
    b = b.at[:, 0].add(a[:, 0] * h0)
    _, h = lax.associative_scan(_combine, (a, b), axis=1)
    return h


def bidirectional_rglru(u, u_c, w_a, b_a, w_x, b_x, lam):
    h0 = jnp.zeros((u_c.shape[0], LRU_WIDTH), jnp.float32)
    a_c, b_c = rglru_coeffs(u_c, w_a[0], b_a[0], w_x[0], b_x[0], lam[0])
    a_l, b_l = rglru_coeffs(u, w_a[0], b_a[0], w_x[0], b_x[0], lam[0])
    h_cf = linear_recurrence(a_c, b_c, h0)
    h_lf = linear_recurrence(a_l, b_l, h_cf[:, -1])
    a_c, b_c = rglru_coeffs(jnp.flip(u_c, 1), w_a[1], b_a[1], w_x[1], b_x[1], lam[1])
    a_l, b_l = rglru_coeffs(jnp.flip(u, 1), w_a[1], b_a[1], w_x[1], b_x[1], lam[1])
    h_cb = linear_recurrence(a_c, b_c, h0)
    h_lb = linear_recurrence(a_l, b_l, h_cb[:, -1])
    y = (h_lf + jnp.flip(h_lb, 1)).astype(u.dtype)
    return y, h_cf, jnp.flip(h_cb, 1)


def mla_queries(cq, q_norm, w_q_up, cos, sin):
    bsz, n, _ = cq.shape
    q = (rmsnorm(cq, q_norm) @ w_q_up).reshape(bsz, n, MLA_HEADS, MLA_QK_DIM)
    q_nope, q_rope = q[..., :MLA_NOPE_DIM], q[..., MLA_NOPE_DIM:]
    if cos is not None:
        q_rope = apply_rope(q_rope, cos[:, None, :], sin[:, None, :])
    return jnp.concatenate([q_nope, q_rope], axis=-1)


def mla_keys_values(ckv, k_rope, kv_norm, w_kv_up, cos, sin):
    bsz, n, _ = ckv.shape
    kv = (rmsnorm(ckv, kv_norm) @ w_kv_up).reshape(bsz, n, MLA_HEADS, MLA_NOPE_DIM + MLA_V_DIM)
    k_nope, v = kv[..., :MLA_NOPE_DIM], kv[..., MLA_NOPE_DIM:]
    if cos is not None:
        k_rope = apply_rope(k_rope, cos, sin)
    k_rope = jnp.broadcast_to(k_rope[:, :, None, :], (bsz, n, MLA_HEADS, MLA_ROPE_DIM))
    return jnp.concatenate([k_nope, k_rope], axis=-1), v


def blockwise_attention(q, k, v, k_ctx, v_ctx):
    bsz, n, heads, dq = q.shape
    scale = dq ** -0.5
    kk = jnp.concatenate([k, k_ctx], axis=1)
    vv = jnp.concatenate([v, v_ctx], axis=1)
    qb = q.reshape(bsz, n // Q_BLOCK, Q_BLOCK, heads, dq).transpose(1, 0, 2, 3, 4)

    def attend(qi):
        s = jnp.einsum('bqhd,bkhd->bhqk', qi, kk).astype(jnp.float32) * scale
        p = jax.nn.softmax(s, axis=-1).astype(vv.dtype)
        return jnp.einsum('bhqk,bkhd->bqhd', p, vv)

    o = lax.map(attend, qb)
    return o.transpose(1, 0, 2, 3, 4).reshape(bsz, n, heads * vv.shape[-1])


def merge_branches(branches, gate_logits, w_branch, w_out):
    bsz, n, _ = gate_logits.shape
    gates = jax.nn.sigmoid(gate_logits).reshape(bsz, n, N_BRANCH, D_MODEL)
    y = sum(gates[:, :, i] * (branches[i] @ w_branch[i]) for i in range(N_BRANCH))
    return y @ w_out


def mixing_sublayer(h, hc, cos, sin, w_in, na_rpb, lru_conv_w, lru_conv_b, lru_w_a, lru_b_a, lru_w_x, lru_b_x,
                    lru_lam, mla_q_norm, mla_kv_norm, mla_w_q_up, mla_w_kv_up, w_branch, w_out, with_ctx_out):
    bsz, n, _ = h.shape
    n_ctx = hc.shape[1]
    na_qkv, lru_x, lru_g, cq, ckv, kr, gate_logits = split_in_proj(h @ w_in)
    na_qkv_c, lru_x_c, lru_g_c, cq_c, ckv_c, kr_c, gate_logits_c = split_in_proj(hc @ w_in)
    qkv = na_qkv.reshape(bsz, n, 3, NA_HEADS, NA_HEAD_DIM)
    qkv_c = na_qkv_c.reshape(bsz, n_ctx, 3, NA_HEADS, NA_HEAD_DIM)
    out_a = neighbourhood_attention(qkv[:, :, 0], qkv[:, :, 1], qkv[:, :, 2], qkv_c[:, :, 1], qkv_c[:, :, 2], na_rpb)
    u = depthwise_conv(lru_x, lru_conv_w, lru_conv_b)
    u_c = depthwise_conv(lru_x_c, lru_conv_w, lru_conv_b)
    y_b, h_cf, h_cb = bidirectional_rglru(u, u_c, lru_w_a, lru_b_a, lru_w_x, lru_b_x, lru_lam)
    out_b = jax.nn.gelu(lru_g) * y_b
    q_m = mla_queries(cq, mla_q_norm, mla_w_q_up, cos, sin)
    k_m, v_m = mla_keys_values(ckv, kr, mla_kv_norm, mla_w_kv_up, cos, sin)
    k_mc, v_mc = mla_keys_values(ckv_c, kr_c, mla_kv_norm, mla_w_kv_up, None, None)
    out_c = blockwise_attention(q_m, k_m, v_m, k_mc, v_mc)
    y = merge_branches((out_a, out_b, out_c), gate_logits, w_branch, w_out)
    if not with_ctx_out:
        return y, None
    out_ac = softmax_attention(qkv_c[:, :, 0], qkv_c[:, :, 1], qkv_c[:, :, 2])
    out_bc = jax.nn.gelu(lru_g_c) * (h_cf + h_cb).astype(lru_g_c.dtype)
    q_mc = mla_queries(cq_c, mla_q_norm, mla_w_q_up, None, None)
    out_cc = softmax_attention(q_mc, k_mc, v_mc)
    yc = merge_branches((out_ac, out_bc, out_cc), gate_logits_c, w_branch, w_out)
    return y, yc


def conv_ffn(h, w_up, conv_w, conv_b, w_down):
    u = depthwise_conv(h @ w_up, conv_w, conv_b)
    val, gate = jnp.split(u, 2, axis=-1)
    return (jax.nn.silu(gate) * val) @ w_down


def _fwd_setup_inputs(seed: int = 0) -> dict:
    key = jax.random.key(seed)
    ks = jax.random.split(key, 32)
    f32 = jnp.float32
    D = D_MODEL

    def nrm(k, shape, scale):
        return jax.random.normal(k, shape, f32) * scale

    lam_u = jax.random.uniform(ks[16], (DEPTH, 2, LRU_WIDTH), f32, 0.9, 0.999)
    lam_s = lam_u ** (1.0 / LRU_C)
    return {
        'x': nrm(ks[0], (BATCH, SEQ, D), 1.0),
        'c': nrm(ks[1], (BATCH, D), 1.0),
        'ctx': nrm(ks[2], (BATCH, CTX_LEN, D), 1.0),
        'c_ctx': nrm(ks[3], (D,), 1.0),
        'w_mod': nrm(ks[4], (DEPTH, D, N_MOD * D), 0.5 * D ** -0.5),
        'b_mod': nrm(ks[5], (DEPTH, N_MOD * D), 0.02),
        'norm_mix': 1.0 + nrm(ks[6], (DEPTH, D), 0.05),
        'norm_ffn': 1.0 + nrm(ks[7], (DEPTH, D), 0.05),
        'w_in': nrm(ks[8], (DEPTH, D, D_IN), D ** -0.5),
        'na_rpb': nrm(ks[9], (DEPTH, NA_HEADS, 2 * NA_WIN_H - 1, 2 * NA_WIN_W - 1), 0.1),
        'lru_conv_w': nrm(ks[10], (DEPTH, LRU_CONV_W, LRU_WIDTH), LRU_CONV_W ** -0.5),
        'lru_conv_b': nrm(ks[11], (DEPTH, LRU_WIDTH), 0.02),
        'lru_w_a': nrm(ks[12], (DEPTH, 2, LRU_BLOCKS, LRU_BLOCK_DIM, LRU_BLOCK_DIM), LRU_BLOCK_DIM ** -0.5),
        'lru_b_a': nrm(ks[13], (DEPTH, 2, LRU_WIDTH), 0.02),
        'lru_w_x': nrm(ks[14], (DEPTH, 2, LRU_BLOCKS, LRU_BLOCK_DIM, LRU_BLOCK_DIM), LRU_BLOCK_DIM ** -0.5),
        'lru_b_x': nrm(ks[15], (DEPTH, 2, LRU_WIDTH), 0.02),
        'lru_lam': jnp.log(lam_s) - jnp.log1p(-lam_s),
        'mla_q_norm': 1.0 + nrm(ks[17], (DEPTH, MLA_Q_RANK), 0.05),
        'mla_kv_norm': 1.0 + nrm(ks[18], (DEPTH, MLA_KV_RANK), 0.05),
        'mla_w_q_up': nrm(ks[19], (DEPTH, MLA_Q_RANK, MLA_HEADS * MLA_QK_DIM), MLA_Q_RANK ** -0.5),
        'mla_w_kv_up': nrm(ks[20], (DEPTH, MLA_KV_RANK, MLA_HEADS * (MLA_NOPE_DIM + MLA_V_DIM)), MLA_KV_RANK ** -0.5),
        'w_branch': nrm(ks[21], (DEPTH, N_BRANCH, BRANCH_WIDTH, D), BRANCH_WIDTH ** -0.5),
        'w_out': nrm(ks[22], (DEPTH, D, D), D ** -0.5),
        'ffn_w_up': nrm(ks[23], (DEPTH, D, 2 * D_FF), D ** -0.5),
        'ffn_conv_w': nrm(ks[24], (DEPTH, FFN_CONV_W, 2 * D_FF), FFN_CONV_W ** -0.5),
        'ffn_conv_b': nrm(ks[25], (DEPTH, 2 * D_FF), 0.02),
        'ffn_w_down': nrm(ks[26], (DEPTH, D_FF, D), D_FF ** -0.5),
        'norm_final': 1.0 + nrm(ks[27], (D,), 0.05),
    }


def _fwd_reference(x, c, ctx, c_ctx, w_mod, b_mod, norm_mix, norm_ffn, w_in, na_rpb, lru_conv_w, lru_conv_b,
              lru_w_a, lru_b_a, lru_w_x, lru_b_x, lru_lam, mla_q_norm, mla_kv_norm, mla_w_q_up, mla_w_kv_up,
              w_branch, w_out, ffn_w_up, ffn_conv_w, ffn_conv_b, ffn_w_down, norm_final):
    n = x.shape[1]
    cos, sin = axial_rope_angles(n, MLA_ROPE_DIM)
    silu_c = jax.nn.silu(c)
    silu_cc = jax.nn.silu(c_ctx)
    xc = ctx
    for l in range(DEPTH):
        last = l == DEPTH - 1
        mod = (silu_c @ w_mod[l] + b_mod[l])[:, None, :]
        mod_c = silu_cc @ w_mod[l] + b_mod[l]
        sh1, sc1, g1, sh2, sc2, g2 = jnp.split(mod, N_MOD, axis=-1)
        sh1c, sc1c, g1c, sh2c, sc2c, g2c = jnp.split(mod_c, N_MOD, axis=-1)
        h = modulate(rmsnorm(x, norm_mix[l]), sh1, sc1)
        hc = modulate(rmsnorm(xc, norm_mix[l]), sh1c, sc1c)
        y, yc = mixing_sublayer(h, hc, cos, sin, w_in[l], na_rpb[l], lru_conv_w[l], lru_conv_b[l], lru_w_a[l],
                                lru_b_a[l], lru_w_x[l], lru_b_x[l], lru_lam[l], mla_q_norm[l], mla_kv_norm[l],
                                mla_w_q_up[l], mla_w_kv_up[l], w_branch[l], w_out[l], not last)
        x = x + g1 * y
        h2 = modulate(rmsnorm(x, norm_ffn[l]), sh2, sc2)
        x = x + g2 * conv_ffn(h2, ffn_w_up[l], ffn_conv_w[l], ffn_conv_b[l], ffn_w_down[l])
        if not last:
            xc = xc + g1c * yc
            h2c = modulate(rmsnorm(xc, norm_ffn[l]), sh2c, sc2c)
            xc = xc + g2c * conv_ffn(h2c, ffn_w_up[l], ffn_conv_w[l], ffn_conv_b[l], ffn_w_down[l])
    return rmsnorm(x, norm_final)


import jax as _jax
import jax.numpy as _jnp

TWIN_FORMAT = 'train_step'
FWD_PARAMS = ['x', 'c', 'ctx', 'c_ctx', 'w_mod', 'b_mod', 'norm_mix', 'norm_ffn', 'w_in', 'na_rpb', 'lru_conv_w', 'lru_conv_b', 'lru_w_a', 'lru_b_a', 'lru_w_x', 'lru_b_x', 'lru_lam', 'mla_q_norm', 'mla_kv_norm', 'mla_w_q_up', 'mla_w_kv_up', 'w_branch', 'w_out', 'ffn_w_up', 'ffn_conv_w', 'ffn_conv_b', 'ffn_w_down', 'norm_final']
TWIN_WEIGHTS = ['c_ctx', 'w_mod', 'b_mod', 'norm_mix', 'norm_ffn', 'w_in', 'na_rpb', 'lru_conv_w', 'lru_conv_b', 'lru_w_a', 'lru_b_a', 'lru_w_x', 'lru_b_x', 'lru_lam', 'mla_q_norm', 'mla_kv_norm', 'mla_w_q_up', 'mla_w_kv_up', 'w_branch', 'w_out', 'ffn_w_up', 'ffn_conv_w', 'ffn_conv_b', 'ffn_w_down', 'norm_final']
TWIN_DIFF_INPUT = 'x'
TWIN_INPUTS = ['x', 'c', 'ctx', 'c_ctx', 'w_mod', 'b_mod', 'norm_mix', 'norm_ffn', 'w_in', 'na_rpb', 'lru_conv_w', 'lru_conv_b', 'lru_w_a', 'lru_b_a', 'lru_w_x', 'lru_b_x', 'lru_lam', 'mla_q_norm', 'mla_kv_norm', 'mla_w_q_up', 'mla_w_kv_up', 'w_branch', 'w_out', 'ffn_w_up', 'ffn_conv_w', 'ffn_conv_b', 'ffn_w_down', 'norm_final', 'loss_target', 'm_c_ctx', 'm_w_mod', 'm_b_mod', 'm_norm_mix', 'm_norm_ffn', 'm_w_in', 'm_na_rpb', 'm_lru_conv_w', 'm_lru_conv_b', 'm_lru_w_a', 'm_lru_b_a', 'm_lru_w_x', 'm_lru_b_x', 'm_lru_lam', 'm_mla_q_norm', 'm_mla_kv_norm', 'm_mla_w_q_up', 'm_mla_w_kv_up', 'm_w_branch', 'm_w_out', 'm_ffn_w_up', 'm_ffn_conv_w', 'm_ffn_conv_b', 'm_ffn_w_down', 'm_norm_final', 'v_c_ctx', 'v_w_mod', 'v_b_mod', 'v_norm_mix', 'v_norm_ffn', 'v_w_in', 'v_na_rpb', 'v_lru_conv_w', 'v_lru_conv_b', 'v_lru_w_a', 'v_lru_b_a', 'v_lru_w_x', 'v_lru_b_x', 'v_lru_lam', 'v_mla_q_norm', 'v_mla_kv_norm', 'v_mla_w_q_up', 'v_mla_w_kv_up', 'v_w_branch', 'v_w_out', 'v_ffn_w_up', 'v_ffn_conv_w', 'v_ffn_conv_b', 'v_ffn_w_down', 'v_norm_final']
TWIN_OUTPUTS = ['loss', 'grad_x', 'grad_c_ctx', 'grad_w_mod', 'grad_b_mod', 'grad_norm_mix', 'grad_norm_ffn', 'grad_w_in', 'grad_na_rpb', 'grad_lru_conv_w', 'grad_lru_conv_b', 'grad_lru_w_a', 'grad_lru_b_a', 'grad_lru_w_x', 'grad_lru_b_x', 'grad_lru_lam', 'grad_mla_q_norm', 'grad_mla_kv_norm', 'grad_mla_w_q_up', 'grad_mla_w_kv_up', 'grad_w_branch', 'grad_w_out', 'grad_ffn_w_up', 'grad_ffn_conv_w', 'grad_ffn_conv_b', 'grad_ffn_w_down', 'grad_norm_final', 'delta_c_ctx', 'delta_w_mod', 'delta_b_mod', 'delta_norm_mix', 'delta_norm_ffn', 'delta_w_in', 'delta_na_rpb', 'delta_lru_conv_w', 'delta_lru_conv_b', 'delta_lru_w_a', 'delta_lru_b_a', 'delta_lru_w_x', 'delta_lru_b_x', 'delta_lru_lam', 'delta_mla_q_norm', 'delta_mla_kv_norm', 'delta_mla_w_q_up', 'delta_mla_w_kv_up', 'delta_w_branch', 'delta_w_out', 'delta_ffn_w_up', 'delta_ffn_conv_w', 'delta_ffn_conv_b', 'delta_ffn_w_down', 'delta_norm_final', 'new_m_c_ctx', 'new_m_w_mod', 'new_m_b_mod', 'new_m_norm_mix', 'new_m_norm_ffn', 'new_m_w_in', 'new_m_na_rpb', 'new_m_lru_conv_w', 'new_m_lru_conv_b', 'new_m_lru_w_a', 'new_m_lru_b_a', 'new_m_lru_w_x', 'new_m_lru_b_x', 'new_m_lru_lam', 'new_m_mla_q_norm', 'new_m_mla_kv_norm', 'new_m_mla_w_q_up', 'new_m_mla_w_kv_up', 'new_m_w_branch', 'new_m_w_out', 'new_m_ffn_w_up', 'new_m_ffn_conv_w', 'new_m_ffn_conv_b', 'new_m_ffn_w_down', 'new_m_norm_final', 'new_v_c_ctx', 'new_v_w_mod', 'new_v_b_mod', 'new_v_norm_mix', 'new_v_norm_ffn', 'new_v_w_in', 'new_v_na_rpb', 'new_v_lru_conv_w', 'new_v_lru_conv_b', 'new_v_lru_w_a', 'new_v_lru_b_a', 'new_v_lru_w_x', 'new_v_lru_b_x', 'new_v_lru_lam', 'new_v_mla_q_norm', 'new_v_mla_kv_norm', 'new_v_mla_w_q_up', 'new_v_mla_w_kv_up', 'new_v_w_branch', 'new_v_w_out', 'new_v_ffn_w_up', 'new_v_ffn_conv_w', 'new_v_ffn_conv_b', 'new_v_ffn_w_down', 'new_v_norm_final']
TWIN_LEAF_KINDS = {'loss': 'loss', 'grad_x': 'grad_x', 'grad_c_ctx': 'grad_w', 'grad_w_mod': 'grad_w', 'grad_b_mod': 'grad_w', 'grad_norm_mix': 'grad_w', 'grad_norm_ffn': 'grad_w', 'grad_w_in': 'grad_w', 'grad_na_rpb': 'grad_w', 'grad_lru_conv_w': 'grad_w', 'grad_lru_conv_b': 'grad_w', 'grad_lru_w_a': 'grad_w', 'grad_lru_b_a': 'grad_w', 'grad_lru_w_x': 'grad_w', 'grad_lru_b_x': 'grad_w', 'grad_lru_lam': 'grad_w', 'grad_mla_q_norm': 'grad_w', 'grad_mla_kv_norm': 'grad_w', 'grad_mla_w_q_up': 'grad_w', 'grad_mla_w_kv_up': 'grad_w', 'grad_w_branch': 'grad_w', 'grad_w_out': 'grad_w', 'grad_ffn_w_up': 'grad_w', 'grad_ffn_conv_w': 'grad_w', 'grad_ffn_conv_b': 'grad_w', 'grad_ffn_w_down': 'grad_w', 'grad_norm_final': 'grad_w', 'delta_c_ctx': 'delta_w', 'delta_w_mod': 'delta_w', 'delta_b_mod': 'delta_w', 'delta_norm_mix': 'delta_w', 'delta_norm_ffn': 'delta_w', 'delta_w_in': 'delta_w', 'delta_na_rpb': 'delta_w', 'delta_lru_conv_w': 'delta_w', 'delta_lru_conv_b': 'delta_w', 'delta_lru_w_a': 'delta_w', 'delta_lru_b_a': 'delta_w', 'delta_lru_w_x': 'delta_w', 'delta_lru_b_x': 'delta_w', 'delta_lru_lam': 'delta_w', 'delta_mla_q_norm': 'delta_w', 'delta_mla_kv_norm': 'delta_w', 'delta_mla_w_q_up': 'delta_w', 'delta_mla_w_kv_up': 'delta_w', 'delta_w_branch': 'delta_w', 'delta_w_out': 'delta_w', 'delta_ffn_w_up': 'delta_w', 'delta_ffn_conv_w': 'delta_w', 'delta_ffn_conv_b': 'delta_w', 'delta_ffn_w_down': 'delta_w', 'delta_norm_final': 'delta_w', 'new_m_c_ctx': 'new_m', 'new_m_w_mod': 'new_m', 'new_m_b_mod': 'new_m', 'new_m_norm_mix': 'new_m', 'new_m_norm_ffn': 'new_m', 'new_m_w_in': 'new_m', 'new_m_na_rpb': 'new_m', 'new_m_lru_conv_w': 'new_m', 'new_m_lru_conv_b': 'new_m', 'new_m_lru_w_a': 'new_m', 'new_m_lru_b_a': 'new_m', 'new_m_lru_w_x': 'new_m', 'new_m_lru_b_x': 'new_m', 'new_m_lru_lam': 'new_m', 'new_m_mla_q_norm': 'new_m', 'new_m_mla_kv_norm': 'new_m', 'new_m_mla_w_q_up': 'new_m', 'new_m_mla_w_kv_up': 'new_m', 'new_m_w_branch': 'new_m', 'new_m_w_out': 'new_m', 'new_m_ffn_w_up': 'new_m', 'new_m_ffn_conv_w': 'new_m', 'new_m_ffn_conv_b': 'new_m', 'new_m_ffn_w_down': 'new_m', 'new_m_norm_final': 'new_m', 'new_v_c_ctx': 'new_v', 'new_v_w_mod': 'new_v', 'new_v_b_mod': 'new_v', 'new_v_norm_mix': 'new_v', 'new_v_norm_ffn': 'new_v', 'new_v_w_in': 'new_v', 'new_v_na_rpb': 'new_v', 'new_v_lru_conv_w': 'new_v', 'new_v_lru_conv_b': 'new_v', 'new_v_lru_w_a': 'new_v', 'new_v_lru_b_a': 'new_v', 'new_v_lru_w_x': 'new_v', 'new_v_lru_b_x': 'new_v', 'new_v_lru_lam': 'new_v', 'new_v_mla_q_norm': 'new_v', 'new_v_mla_kv_norm': 'new_v', 'new_v_mla_w_q_up': 'new_v', 'new_v_mla_w_kv_up': 'new_v', 'new_v_w_branch': 'new_v', 'new_v_w_out': 'new_v', 'new_v_ffn_w_up': 'new_v', 'new_v_ffn_conv_w': 'new_v', 'new_v_ffn_conv_b': 'new_v', 'new_v_ffn_w_down': 'new_v', 'new_v_norm_final': 'new_v'}


def _forward(args):
    return _fwd_reference(*[args[k] for k in FWD_PARAMS])


def _output_shape():
    def fwd():
        inp = _fwd_setup_inputs(0)
        return _fwd_reference(*[inp[k] for k in FWD_PARAMS])
    out = _jax.eval_shape(fwd)
    return out.shape, out.dtype

N_MICROBATCH = 1
ADAM_LR = 0.001
ADAM_B1 = 0.9
ADAM_B2 = 0.999
ADAM_EPS = 1e-08
ADAM_WD = 0.01
ADAM_STEP = 10
PER_EXAMPLE_BATCH_AXIS = {'x': 0, 'c': 0, 'ctx': 0, 'loss_target': 0}
SHARED_INPUTS = []
_WEIGHT_DTYPES = {'c_ctx': _jnp.float32, 'w_mod': _jnp.float32, 'b_mod': _jnp.float32, 'norm_mix': _jnp.float32, 'norm_ffn': _jnp.float32, 'w_in': _jnp.float32, 'na_rpb': _jnp.float32, 'lru_conv_w': _jnp.float32, 'lru_conv_b': _jnp.float32, 'lru_w_a': _jnp.float32, 'lru_b_a': _jnp.float32, 'lru_w_x': _jnp.float32, 'lru_b_x': _jnp.float32, 'lru_lam': _jnp.float32, 'mla_q_norm': _jnp.float32, 'mla_kv_norm': _jnp.float32, 'mla_w_q_up': _jnp.float32, 'mla_w_kv_up': _jnp.float32, 'w_branch': _jnp.float32, 'w_out': _jnp.float32, 'ffn_w_up': _jnp.float32, 'ffn_conv_w': _jnp.float32, 'ffn_conv_b': _jnp.float32, 'ffn_w_down': _jnp.float32, 'norm_final': _jnp.float32}
MOMENT_SCALE = {'c_ctx': 3.157067e-02, 'w_mod': 7.756797e-02, 'b_mod': 1.364778e-01, 'norm_mix': 5.177911e-02, 'norm_ffn': 2.425270e-02, 'w_in': 3.027096e-02, 'na_rpb': 6.123349e-04, 'lru_conv_w': 8.121749e-02, 'lru_conv_b': 2.011988e-01, 'lru_w_a': 3.846424e-03, 'lru_b_a': 6.342318e-03, 'lru_w_x': 8.252052e-03, 'lru_b_x': 1.672438e-02, 'lru_lam': 1.647552e-02, 'mla_q_norm': 2.692741e-03, 'mla_kv_norm': 2.337911e-02, 'mla_w_q_up': 1.557974e-03, 'mla_w_kv_up': 8.095013e-03, 'w_branch': 3.295088e-02, 'w_out': 5.721598e-02, 'ffn_w_up': 1.080192e-02, 'ffn_conv_w': 1.073070e-02, 'ffn_conv_b': 1.158879e-02, 'ffn_w_down': 1.774347e-02, 'norm_final': 1.609165e+01}


def _to_microbatches(a, axis):
    t = _jnp.moveaxis(a, axis, 0)
    t = t.reshape((N_MICROBATCH, t.shape[0] // N_MICROBATCH) + t.shape[1:])
    return _jnp.moveaxis(t, 1, axis + 1)


def setup_inputs(seed: int = 0) -> dict:
    inp = _fwd_setup_inputs(seed)
    key = _jax.random.fold_in(_jax.random.key(seed), 7919)
    shape, _ = _output_shape()
    out = dict(inp)
    out["loss_target"] = _jax.random.normal(_jax.random.fold_in(key, 0), shape, _jnp.float32)
    for i, name in enumerate(TWIN_WEIGHTS):
        w = inp[name].astype(_jnp.float32)
        if MOMENT_SCALE is None:
            s = _jnp.sqrt(_jnp.mean(_jnp.square(w)) + 1e-30)
        else:
            s = MOMENT_SCALE[name]
        km, kv = _jax.random.split(_jax.random.fold_in(key, i + 1))
        out[name] = w
        out["m_" + name] = s * _jax.random.normal(km, w.shape, _jnp.float32)
        out["v_" + name] = (s * s) * _jax.random.uniform(kv, w.shape, _jnp.float32, 0.5, 1.5)
    if N_MICROBATCH > 1:
        for name, axis in PER_EXAMPLE_BATCH_AXIS.items():
            out[name] = _to_microbatches(out[name], axis)
    return {'x': out['x'], 'c': out['c'], 'ctx': out['ctx'], 'c_ctx': out['c_ctx'], 'w_mod': out['w_mod'], 'b_mod': out['b_mod'], 'norm_mix': out['norm_mix'], 'norm_ffn': out['norm_ffn'], 'w_in': out['w_in'], 'na_rpb': out['na_rpb'], 'lru_conv_w': out['lru_conv_w'], 'lru_conv_b': out['lru_conv_b'], 'lru_w_a': out['lru_w_a'], 'lru_b_a': out['lru_b_a'], 'lru_w_x': out['lru_w_x'], 'lru_b_x': out['lru_b_x'], 'lru_lam': out['lru_lam'], 'mla_q_norm': out['mla_q_norm'], 'mla_kv_norm': out['mla_kv_norm'], 'mla_w_q_up': out['mla_w_q_up'], 'mla_w_kv_up': out['mla_w_kv_up'], 'w_branch': out['w_branch'], 'w_out': out['w_out'], 'ffn_w_up': out['ffn_w_up'], 'ffn_conv_w': out['ffn_conv_w'], 'ffn_conv_b': out['ffn_conv_b'], 'ffn_w_down': out['ffn_w_down'], 'norm_final': out['norm_final'], 'loss_target': out['loss_target'], 'm_c_ctx': out['m_c_ctx'], 'm_w_mod': out['m_w_mod'], 'm_b_mod': out['m_b_mod'], 'm_norm_mix': out['m_norm_mix'], 'm_norm_ffn': out['m_norm_ffn'], 'm_w_in': out['m_w_in'], 'm_na_rpb': out['m_na_rpb'], 'm_lru_conv_w': out['m_lru_conv_w'], 'm_lru_conv_b': out['m_lru_conv_b'], 'm_lru_w_a': out['m_lru_w_a'], 'm_lru_b_a': out['m_lru_b_a'], 'm_lru_w_x': out['m_lru_w_x'], 'm_lru_b_x': out['m_lru_b_x'], 'm_lru_lam': out['m_lru_lam'], 'm_mla_q_norm': out['m_mla_q_norm'], 'm_mla_kv_norm': out['m_mla_kv_norm'], 'm_mla_w_q_up': out['m_mla_w_q_up'], 'm_mla_w_kv_up': out['m_mla_w_kv_up'], 'm_w_branch': out['m_w_branch'], 'm_w_out': out['m_w_out'], 'm_ffn_w_up': out['m_ffn_w_up'], 'm_ffn_conv_w': out['m_ffn_conv_w'], 'm_ffn_conv_b': out['m_ffn_conv_b'], 'm_ffn_w_down': out['m_ffn_w_down'], 'm_norm_final': out['m_norm_final'], 'v_c_ctx': out['v_c_ctx'], 'v_w_mod': out['v_w_mod'], 'v_b_mod': out['v_b_mod'], 'v_norm_mix': out['v_norm_mix'], 'v_norm_ffn': out['v_norm_ffn'], 'v_w_in': out['v_w_in'], 'v_na_rpb': out['v_na_rpb'], 'v_lru_conv_w': out['v_lru_conv_w'], 'v_lru_conv_b': out['v_lru_conv_b'], 'v_lru_w_a': out['v_lru_w_a'], 'v_lru_b_a': out['v_lru_b_a'], 'v_lru_w_x': out['v_lru_w_x'], 'v_lru_b_x': out['v_lru_b_x'], 'v_lru_lam': out['v_lru_lam'], 'v_mla_q_norm': out['v_mla_q_norm'], 'v_mla_kv_norm': out['v_mla_kv_norm'], 'v_mla_w_q_up': out['v_mla_w_q_up'], 'v_mla_w_kv_up': out['v_mla_w_kv_up'], 'v_w_branch': out['v_w_branch'], 'v_w_out': out['v_w_out'], 'v_ffn_w_up': out['v_ffn_w_up'], 'v_ffn_conv_w': out['v_ffn_conv_w'], 'v_ffn_conv_b': out['v_ffn_conv_b'], 'v_ffn_w_down': out['v_ffn_w_down'], 'v_norm_final': out['v_norm_final']}


def _loss(weights, diff, rest, loss_target):
    with _jax.named_scope("forward"):
        args = {**rest, TWIN_DIFF_INPUT: diff, **{k: w.astype(_WEIGHT_DTYPES[k]) for k, w in weights.items()}}
        y = _forward(args)
    with _jax.named_scope("loss_head"):
        err = _jnp.square(y.astype(_jnp.float32) - loss_target)
        return 0.5 * _jnp.sum(_jnp.mean(err, axis=-1)) if err.ndim else 0.5 * err


def _adamw(w, g, m, v):
    m = ADAM_B1 * m + (1.0 - ADAM_B1) * g
    v = ADAM_B2 * v + (1.0 - ADAM_B2) * _jnp.square(g)
    m_hat = m / (1.0 - ADAM_B1 ** ADAM_STEP)
    v_hat = v / (1.0 - ADAM_B2 ** ADAM_STEP)
    delta = -ADAM_LR * (m_hat / (_jnp.sqrt(v_hat) + ADAM_EPS) + ADAM_WD * w)
    return delta, m, v


def reference(x, c, ctx, c_ctx, w_mod, b_mod, norm_mix, norm_ffn, w_in, na_rpb, lru_conv_w, lru_conv_b, lru_w_a, lru_b_a, lru_w_x, lru_b_x, lru_lam, mla_q_norm, mla_kv_norm, mla_w_q_up, mla_w_kv_up, w_branch, w_out, ffn_w_up, ffn_conv_w, ffn_conv_b, ffn_w_down, norm_final, loss_target, m_c_ctx, m_w_mod, m_b_mod, m_norm_mix, m_norm_ffn, m_w_in, m_na_rpb, m_lru_conv_w, m_lru_conv_b, m_lru_w_a, m_lru_b_a, m_lru_w_x, m_lru_b_x, m_lru_lam, m_mla_q_norm, m_mla_kv_norm, m_mla_w_q_up, m_mla_w_kv_up, m_w_branch, m_w_out, m_ffn_w_up, m_ffn_conv_w, m_ffn_conv_b, m_ffn_w_down, m_norm_final, v_c_ctx, v_w_mod, v_b_mod, v_norm_mix, v_norm_ffn, v_w_in, v_na_rpb, v_lru_conv_w, v_lru_conv_b, v_lru_w_a, v_lru_b_a, v_lru_w_x, v_lru_b_x, v_lru_lam, v_mla_q_norm, v_mla_kv_norm, v_mla_w_q_up, v_mla_w_kv_up, v_w_branch, v_w_out, v_ffn_w_up, v_ffn_conv_w, v_ffn_conv_b, v_ffn_w_down, v_norm_final):
    given = dict(x=x, c=c, ctx=ctx, c_ctx=c_ctx, w_mod=w_mod, b_mod=b_mod, norm_mix=norm_mix, norm_ffn=norm_ffn, w_in=w_in, na_rpb=na_rpb, lru_conv_w=lru_conv_w, lru_conv_b=lru_conv_b, lru_w_a=lru_w_a, lru_b_a=lru_b_a, lru_w_x=lru_w_x, lru_b_x=lru_b_x, lru_lam=lru_lam, mla_q_norm=mla_q_norm, mla_kv_norm=mla_kv_norm, mla_w_q_up=mla_w_q_up, mla_w_kv_up=mla_w_kv_up, w_branch=w_branch, w_out=w_out, ffn_w_up=ffn_w_up, ffn_conv_w=ffn_conv_w, ffn_conv_b=ffn_conv_b, ffn_w_down=ffn_w_down, norm_final=norm_final, loss_target=loss_target, m_c_ctx=m_c_ctx, m_w_mod=m_w_mod, m_b_mod=m_b_mod, m_norm_mix=m_norm_mix, m_norm_ffn=m_norm_ffn, m_w_in=m_w_in, m_na_rpb=m_na_rpb, m_lru_conv_w=m_lru_conv_w, m_lru_conv_b=m_lru_conv_b, m_lru_w_a=m_lru_w_a, m_lru_b_a=m_lru_b_a, m_lru_w_x=m_lru_w_x, m_lru_b_x=m_lru_b_x, m_lru_lam=m_lru_lam, m_mla_q_norm=m_mla_q_norm, m_mla_kv_norm=m_mla_kv_norm, m_mla_w_q_up=m_mla_w_q_up, m_mla_w_kv_up=m_mla_w_kv_up, m_w_branch=m_w_branch, m_w_out=m_w_out, m_ffn_w_up=m_ffn_w_up, m_ffn_conv_w=m_ffn_conv_w, m_ffn_conv_b=m_ffn_conv_b, m_ffn_w_down=m_ffn_w_down, m_norm_final=m_norm_final, v_c_ctx=v_c_ctx, v_w_mod=v_w_mod, v_b_mod=v_b_mod, v_norm_mix=v_norm_mix, v_norm_ffn=v_norm_ffn, v_w_in=v_w_in, v_na_rpb=v_na_rpb, v_lru_conv_w=v_lru_conv_w, v_lru_conv_b=v_lru_conv_b, v_lru_w_a=v_lru_w_a, v_lru_b_a=v_lru_b_a, v_lru_w_x=v_lru_w_x, v_lru_b_x=v_lru_b_x, v_lru_lam=v_lru_lam, v_mla_q_norm=v_mla_q_norm, v_mla_kv_norm=v_mla_kv_norm, v_mla_w_q_up=v_mla_w_q_up, v_mla_w_kv_up=v_mla_w_kv_up, v_w_branch=v_w_branch, v_w_out=v_w_out, v_ffn_w_up=v_ffn_w_up, v_ffn_conv_w=v_ffn_conv_w, v_ffn_conv_b=v_ffn_conv_b, v_ffn_w_down=v_ffn_w_down, v_norm_final=v_norm_final)
    weights = {n: given[n] for n in TWIN_WEIGHTS}
    shared = {n: given[n] for n in SHARED_INPUTS}
    per_example = {n: given[n] for n in ['x', 'c', 'ctx']}
    grad_fn = _jax.value_and_grad(_loss, argnums=(0, 1))

    def one_microbatch(ex, loss_target):
        ex = dict(ex)
        diff = ex.pop(TWIN_DIFF_INPUT)
        return grad_fn(weights, diff, {**shared, **ex}, loss_target)

    if N_MICROBATCH == 1:
        loss, (grad_w, grad_x) = one_microbatch(per_example, given["loss_target"])
    else:
        def body(carry, xs):
            loss_sum, grad_sum = carry
            l_k, (gw_k, gx_k) = one_microbatch(xs[0], xs[1])
            with _jax.named_scope("update"):
                return (loss_sum + l_k, _jax.tree.map(_jnp.add, grad_sum, gw_k)), gx_k

        init = (_jnp.zeros((), _jnp.float32), _jax.tree.map(_jnp.zeros_like, weights))
        (loss, grad_w), grad_x = _jax.lax.scan(body, init, (per_example, given["loss_target"]))
    with _jax.named_scope("update"):
        delta_w, new_m, new_v = {}, {}, {}
        for n in TWIN_WEIGHTS:
            delta_w[n], new_m[n], new_v[n] = _adamw(weights[n], grad_w[n], given["m_" + n], given["v_" + n])
    return (loss, grad_x, *[grad_w[n] for n in TWIN_WEIGHTS], *[delta_w[n] for n in TWIN_WEIGHTS],
            *[new_m[n] for n in TWIN_WEIGHTS], *[new_v[n] for n in TWIN_WEIGHTS])
```

```python
import functools
import math

import numpy as np
import jax
import jax.numpy as jnp
from jax import lax
from jax.experimental import pallas as pl
from jax.experimental.pallas import tpu as pltpu

F32 = jnp.float32
BF16 = jnp.bfloat16
MESH = pl.DeviceIdType.MESH

DEPTH = 4
GRID_W = 64
NORM_EPS = 1e-6
NEG_INF = -1e30
N_MOD = 6
NA_HEADS = 8
NA_HEAD_DIM = 128
NA_WIN_H = 8
NA_WIN_W = 16
LRU_BLOCKS = 8
LRU_CONV_W = 4
LRU_C = 8.0
MLA_HEADS = 8
MLA_NOPE_DIM = 128
MLA_ROPE_DIM = 64
MLA_V_DIM = 128
ROPE_THETA = 10000.0
N_BRANCH = 3
FFN_CONV_W = 3
ADAM_LR = 0.001
ADAM_B1 = 0.9
ADAM_B2 = 0.999
ADAM_EPS = 1e-08
ADAM_WD = 0.01
ADAM_STEP = 10

N_CHIPS = 4
LANE = 128
SUBLANE = 8
VMEM_LIMIT = 48 * 1024 * 1024
MOD_ROWS = 128


def _cparams(sem=None, **kw):
    if sem is not None:
        kw["dimension_semantics"] = sem
    return pltpu.CompilerParams(vmem_limit_bytes=VMEM_LIMIT, **kw)


def _tile(d, cap, align):
    best = None
    t = align
    while t <= min(d, cap):
        if d % t == 0:
            best = t
        t += align
    return d if best is None else best


def _mm(a, b, *, ta=False, tb=False, name):
    m, k = (a.shape[1], a.shape[0]) if ta else a.shape
    n = b.shape[0] if tb else b.shape[1]
    tm = _tile(m, 1088, 16)
    tn = _tile(n, 640, LANE)
    tk = _tile(k, 640, LANE)
    if tk < 512 and _tile(k, 2304, LANE) > tk:
        tk = _tile(k, 2304, LANE)
        tm = _tile(m, 512, 16)
    nk = k // tk

    def body(a_ref, b_ref, o_ref, acc_ref):
        kk = pl.program_id(2)

        @pl.when(kk == 0)
        def _():
            acc_ref[...] = jnp.zeros_like(acc_ref)

        av = a_ref[...]
        bv = b_ref[...]
        if ta:
            av = av.astype(F32).T
        av = av.astype(BF16)
        bv = bv.astype(BF16)
        dims = (((1,), (1 if tb else 0,)), ((), ()))
        acc_ref[...] += lax.dot_general(av, bv, dims, preferred_element_type=F32)

        @pl.when(kk == nk - 1)
        def _():
            o_ref[...] = acc_ref[...]

    a_spec = pl.BlockSpec((tk, tm), lambda i, j, kk: (kk, i)) if ta else pl.BlockSpec((tm, tk), lambda i, j, kk: (i, kk))
    b_spec = pl.BlockSpec((tn, tk), lambda i, j, kk: (j, kk)) if tb else pl.BlockSpec((tk, tn), lambda i, j, kk: (kk, j))
    return pl.pallas_call(
        body, name=name, grid=(m // tm, n // tn, nk),
        in_specs=[a_spec, b_spec], out_specs=pl.BlockSpec((tm, tn), lambda i, j, kk: (i, j)),
        out_shape=jax.ShapeDtypeStruct((m, n), F32), scratch_shapes=[pltpu.VMEM((tm, tn), F32)],
        compiler_params=_cparams(("parallel", "parallel", "arbitrary")),
    )(a, b)


@jax.custom_vjp
def pdot(a, w, wp):
    return _mm(a, w, name="mm_fwd")


def _pdot_fwd(a, w, wp):
    return _mm(a, w, name="mm_fwd"), (a, w)


def _pdot_bwd(res, g):
    a, w = res
    da = _mm(g, w, tb=True, name="mm_dgrad")
    dw = _mm(a, g, ta=True, name="mm_wgrad")
    return da, jnp.zeros_like(w), dw


pdot.defvjp(_pdot_fwd, _pdot_bwd)


def _bd_mm(x, w, name):
    t, c = x.shape
    nb, bd, _ = w.shape
    tm = _tile(t, 1088, 16)

    def body(x_ref, w_ref, o_ref):
        o_ref[...] = jnp.dot(x_ref[...].astype(BF16), w_ref[0].astype(BF16), preferred_element_type=F32)

    return pl.pallas_call(
        body, name=name, grid=(t // tm, nb),
        in_specs=[pl.BlockSpec((tm, bd), lambda i, k: (i, k)), pl.BlockSpec((1, bd, bd), lambda i, k: (k, 0, 0))],
        out_specs=pl.BlockSpec((tm, bd), lambda i, k: (i, k)),
        out_shape=jax.ShapeDtypeStruct((t, c), F32),
        compiler_params=_cparams(("parallel", "parallel")),
    )(x, w)


def _bd_wgrad(x, g, nb, name):
    t, c = x.shape
    bd = c // nb
    tk = _tile(t, 1088, 16)
    nt = t // tk

    def body(x_ref, g_ref, o_ref):
        @pl.when(pl.program_id(1) == 0)
        def _():
            o_ref[...] = jnp.zeros_like(o_ref)

        xt = x_ref[...].T.astype(BF16)
        o_ref[0] += jnp.dot(xt, g_ref[...].astype(BF16), preferred_element_type=F32)

    return pl.pallas_call(
        body, name=name, grid=(nb, nt),
        in_specs=[pl.BlockSpec((tk, bd), lambda k, i: (i, k)), pl.BlockSpec((tk, bd), lambda k, i: (i, k))],
        out_specs=pl.BlockSpec((1, bd, bd), lambda k, i: (k, 0, 0)),
        out_shape=jax.ShapeDtypeStruct((nb, bd, bd), F32),
        compiler_params=_cparams(("parallel", "arbitrary")),
    )(x, g)


@jax.custom_vjp
def bd_dot(x, w):
    return _bd_mm(x, w, "bd_fwd")


def _bd_fwd(x, w):
    return _bd_mm(x, w, "bd_fwd"), (x, w)


def _bd_bwd(res, g):
    x, w = res
    return _bd_mm(g, jnp.swapaxes(w, 1, 2), "bd_dgrad"), _bd_wgrad(x, g, w.shape[0], "bd_wgrad")


bd_dot.defvjp(_bd_fwd, _bd_bwd)


def _nm_tiles(t, seg_rows):
    tm = _tile(math.gcd(t, seg_rows), 256, SUBLANE)
    return tm, seg_rows // tm


def _nm_fwd_call(x, g, shift, scale, seg_rows):
    t, d = x.shape
    tm, seg_blocks = _nm_tiles(t, seg_rows)
    nseg = shift.shape[0]

    def seg(i):
        return jnp.minimum(i // seg_blocks, nseg - 1)

    def body(x_ref, g_ref, sh_ref, sc_ref, o_ref):
        xv = x_ref[...]
        y = xv * lax.rsqrt(jnp.mean(xv * xv, axis=-1, keepdims=True) + NORM_EPS)
        o_ref[...] = (y * g_ref[...]) * (1.0 + sc_ref[0]) + sh_ref[0]

    return pl.pallas_call(
        body, name="norm_mod_fwd", grid=(t // tm,),
        in_specs=[pl.BlockSpec((tm, d), lambda i: (i, 0)), pl.BlockSpec((1, d), lambda i: (0, 0)),
                  pl.BlockSpec((1, 1, d), lambda i: (seg(i), 0, 0)), pl.BlockSpec((1, 1, d), lambda i: (seg(i), 0, 0))],
        out_specs=pl.BlockSpec((tm, d), lambda i: (i, 0)),
        out_shape=jax.ShapeDtypeStruct((t, d), F32),
        compiler_params=_cparams(("parallel",)),
    )(x, g.reshape(1, d), shift.reshape(nseg, 1, d), scale.reshape(nseg, 1, d))


def _nm_bwd_call(x, g, scale, dy, seg_rows):
    t, d = x.shape
    tm, seg_blocks = _nm_tiles(t, seg_rows)
    nseg = scale.shape[0]

    def seg(i):
        return jnp.minimum(i // seg_blocks, nseg - 1)

    def body(x_ref, g_ref, sc_ref, dy_ref, dx_ref, dg_ref, dsh_ref, dsc_ref):
        i = pl.program_id(0)

        @pl.when(i == 0)
        def _():
            dg_ref[...] = jnp.zeros_like(dg_ref)

        @pl.when((i == 0) | (i == seg_blocks))
        def _():
            dsh_ref[...] = jnp.zeros_like(dsh_ref)
            dsc_ref[...] = jnp.zeros_like(dsc_ref)

        xv = x_ref[...]
        dyv = dy_ref[...]
        gv = g_ref[...]
        rstd = lax.rsqrt(jnp.mean(xv * xv, axis=-1, keepdims=True) + NORM_EPS)
        xhat = xv * rstd
        dsh_ref[0] += jnp.sum(dyv, axis=0, keepdims=True)
        dsc_ref[0] += jnp.sum(dyv * (xhat * gv), axis=0, keepdims=True)
        dn = dyv * (1.0 + sc_ref[0])
        dg_ref[...] += jnp.sum(dn * xhat, axis=0, keepdims=True)
        dxh = dn * gv
        dx_ref[...] = rstd * (dxh - xhat * jnp.mean(dxh * xhat, axis=-1, keepdims=True))

    return pl.pallas_call(
        body, name="norm_mod_bwd", grid=(t // tm,),
        in_specs=[pl.BlockSpec((tm, d), lambda i: (i, 0)), pl.BlockSpec((1, d), lambda i: (0, 0)),
                  pl.BlockSpec((1, 1, d), lambda i: (seg(i), 0, 0)), pl.BlockSpec((tm, d), lambda i: (i, 0))],
        out_specs=[pl.BlockSpec((tm, d), lambda i: (i, 0)), pl.BlockSpec((1, d), lambda i: (0, 0)),
                   pl.BlockSpec((1, 1, d), lambda i: (seg(i), 0, 0)), pl.BlockSpec((1, 1, d), lambda i: (seg(i), 0, 0))],
        out_shape=[jax.ShapeDtypeStruct((t, d), F32), jax.ShapeDtypeStruct((1, d), F32),
                   jax.ShapeDtypeStruct((nseg, 1, d), F32), jax.ShapeDtypeStruct((nseg, 1, d), F32)],
        compiler_params=_cparams(("arbitrary",)),
    )(x, g.reshape(1, d), scale.reshape(nseg, 1, d), dy)


@functools.partial(jax.custom_vjp, nondiff_argnums=(4,))
def norm_mod(x, g, shift, scale, seg_rows):
    return _nm_fwd_call(x, g, shift, scale, seg_rows)


def _norm_mod_fwd(x, g, shift, scale, seg_rows):
    return _nm_fwd_call(x, g, shift, scale, seg_rows), (x, g, scale)


def _norm_mod_bwd(seg_rows, res, dy):
    x, g, scale = res
    dx, dg, dsh, dsc = _nm_bwd_call(x, g, scale, dy, seg_rows)
    return dx, dg.reshape(g.shape), dsh.reshape(scale.shape), dsc.reshape(scale.shape)


norm_mod.defvjp(_norm_mod_fwd, _norm_mod_bwd)


def rmsnorm_p(x, g):
    z = jnp.zeros((1, x.shape[1]), F32)
    return norm_mod(x, g, z, z, x.shape[0])


def _linrec_call(a, b, name):
    t, c = a.shape
    tb = _tile(t, 512, SUBLANE)

    def body(a_ref, b_ref, h_ref, carry_ref):
        @pl.when(pl.program_id(0) == 0)
        def _():
            carry_ref[...] = jnp.zeros_like(carry_ref)

        def tile(i, carry):
            r = pl.multiple_of(i * SUBLANE, SUBLANE)
            av = a_ref[pl.ds(r, SUBLANE), :]
            bv = b_ref[pl.ds(r, SUBLANE), :]
            row = lax.broadcasted_iota(jnp.int32, av.shape, 0)
            d = 1
            while d < SUBLANE:
                a_sh = pltpu.roll(av, d, 0)
                b_sh = pltpu.roll(bv, d, 0)
                m = row >= d
                bv = jnp.where(m, av * b_sh + bv, bv)
                av = jnp.where(m, av * a_sh, av)
                d *= 2
            hv = av * carry + bv
            h_ref[pl.ds(r, SUBLANE), :] = hv
            return jnp.sum(jnp.where(row == SUBLANE - 1, hv, 0.0), axis=0, keepdims=True)

        carry_ref[...] = lax.fori_loop(0, tb // SUBLANE, tile, carry_ref[...])

    return pl.pallas_call(
        body, name=name, grid=(t // tb,),
        in_specs=[pl.BlockSpec((tb, c), lambda i: (i, 0)), pl.BlockSpec((tb, c), lambda i: (i, 0))],
        out_specs=pl.BlockSpec((tb, c), lambda i: (i, 0)),
        out_shape=jax.ShapeDtypeStruct((t, c), F32), scratch_shapes=[pltpu.VMEM((1, c), F32)],
        compiler_params=_cparams(("arbitrary",)),
    )(a, b)


@jax.custom_vjp
def linrec(a, b):
    return _linrec_call(a, b, "linrec_fwd")


def _linrec_fwd(a, b):
    h = _linrec_call(a, b, "linrec_fwd")
    return h, (a, h)


def _linrec_bwd(res, dh):
    a, h = res
    a_next = jnp.concatenate([a[1:], jnp.zeros_like(a[:1])], axis=0)
    g = jnp.flip(_linrec_call(jnp.flip(a_next, 0), jnp.flip(dh, 0), "linrec_bwd"), 0)
    h_prev = jnp.concatenate([jnp.zeros_like(h[:1]), h[:-1]], axis=0)
    return g * h_prev, g


linrec.defvjp(_linrec_fwd, _linrec_bwd)


def _attn_fwd_call(q, k, v, scale):
    h, lq, dq = q.shape
    lk, dv = v.shape[1], v.shape[2]
    tq = _tile(lq, 256, SUBLANE)

    def body(q_ref, k_ref, v_ref, o_ref, lse_ref):
        s = lax.dot_general(q_ref[0].astype(BF16), k_ref[0].astype(BF16), (((1,), (1,)), ((), ())),
                            preferred_element_type=F32) * scale
        m = jnp.max(s, axis=-1, keepdims=True)
        p = jnp.exp(s - m)
        l = jnp.sum(p, axis=-1, keepdims=True)
        p = p / l
        o_ref[0] = jnp.dot(p.astype(BF16), v_ref[0].astype(BF16), preferred_element_type=F32)
        lse_ref[0] = m + jnp.log(l)

    return pl.pallas_call(
        body, name="attn_fwd", grid=(h, lq // tq),
        in_specs=[pl.BlockSpec((1, tq, dq), lambda a, i: (a, i, 0)), pl.BlockSpec((1, lk, dq), lambda a, i: (a, 0, 0)),
                  pl.BlockSpec((1, lk, dv), lambda a, i: (a, 0, 0))],
        out_specs=[pl.BlockSpec((1, tq, dv), lambda a, i: (a, i, 0)), pl.BlockSpec((1, tq, 1), lambda a, i: (a, i, 0))],
        out_shape=[jax.ShapeDtypeStruct((h, lq, dv), F32), jax.ShapeDtypeStruct((h, lq, 1), F32)],
        compiler_params=_cparams(("parallel", "parallel")),
    )(q, k, v)


def _attn_dq_call(q, k, v, do, lse, delta, scale):
    h, lq, dq = q.shape
    lk, dv = v.shape[1], v.shape[2]
    tq = _tile(lq, 256, SUBLANE)

    def body(q_ref, k_ref, v_ref, do_ref, lse_ref, dl_ref, dq_ref):
        kb = k_ref[0].astype(BF16)
        s = lax.dot_general(q_ref[0].astype(BF16), kb, (((1,), (1,)), ((), ())), preferred_element_type=F32) * scale
        p = jnp.exp(s - lse_ref[0])
        dp = lax.dot_general(do_ref[0].astype(BF16), v_ref[0].astype(BF16), (((1,), (1,)), ((), ())),
                             preferred_element_type=F32)
        ds = p * (dp - dl_ref[0])
        dq_ref[0] = jnp.dot(ds.astype(BF16), kb, preferred_element_type=F32) * scale

    return pl.pallas_call(
        body, name="attn_dq", grid=(h, lq // tq),
        in_specs=[pl.BlockSpec((1, tq, dq), lambda a, i: (a, i, 0)), pl.BlockSpec((1, lk, dq), lambda a, i: (a, 0, 0)),
                  pl.BlockSpec((1, lk, dv), lambda a, i: (a, 0, 0)), pl.BlockSpec((1, tq, dv), lambda a, i: (a, i, 0)),
                  pl.BlockSpec((1, tq, 1), lambda a, i: (a, i, 0)), pl.BlockSpec((1, tq, 1), lambda a, i: (a, i, 0))],
        out_specs=pl.BlockSpec((1, tq, dq), lambda a, i: (a, i, 0)),
        out_shape=jax.ShapeDtypeStruct((h, lq, dq), F32),
        compiler_params=_cparams(("parallel", "parallel")),
    )(q, k, v, do, lse, delta)


def _attn_dkv_call(q, k, v, do, lse_row, delta_row, scale):
    h, lq, dq = q.shape
    lk, dv = v.shape[1], v.shape[2]
    tk = _tile(lk, 256, SUBLANE)

    def body(q_ref, k_ref, v_ref, do_ref, lse_ref, dl_ref, dk_ref, dv_ref):
        qb = q_ref[0].astype(BF16)
        dob = do_ref[0].astype(BF16)
        st = lax.dot_general(k_ref[0].astype(BF16), qb, (((1,), (1,)), ((), ())), preferred_element_type=F32) * scale
        pt = jnp.exp(st - lse_ref[0])
        dpt = lax.dot_general(v_ref[0].astype(BF16), dob, (((1,), (1,)), ((), ())), preferred_element_type=F32)
        dst = pt * (dpt - dl_ref[0])
        dk_ref[0] = jnp.dot(dst.astype(BF16), qb, preferred_element_type=F32) * scale
        dv_ref[0] = jnp.dot(pt.astype(BF16), dob, preferred_element_type=F32)

    return pl.pallas_call(
        body, name="attn_dkv", grid=(h, lk // tk),
        in_specs=[pl.BlockSpec((1, lq, dq), lambda a, i: (a, 0, 0)), pl.BlockSpec((1, tk, dq), lambda a, i: (a, i, 0)),
                  pl.BlockSpec((1, tk, dv), lambda a, i: (a, i, 0)), pl.BlockSpec((1, lq, dv), lambda a, i: (a, 0, 0)),
                  pl.BlockSpec((1, 1, lq), lambda a, i: (a, 0, 0)), pl.BlockSpec((1, 1, lq), lambda a, i: (a, 0, 0))],
        out_specs=[pl.BlockSpec((1, tk, dq), lambda a, i: (a, i, 0)), pl.BlockSpec((1, tk, dv), lambda a, i: (a, i, 0))],
        out_shape=[jax.ShapeDtypeStruct((h, lk, dq), F32), jax.ShapeDtypeStruct((h, lk, dv), F32)],
        compiler_params=_cparams(("parallel", "parallel")),
    )(q, k, v, do, lse_row, delta_row)


@jax.custom_vjp
def attention(q, k, v):
    return _attn_fwd_call(q, k, v, q.shape[-1] ** -0.5)[0]


def _attention_fwd(q, k, v):
    o, lse = _attn_fwd_call(q, k, v, q.shape[-1] ** -0.5)
    return o, (q, k, v, o, lse)


def _attention_bwd(res, do):
    q, k, v, o, lse = res
    h, lq, _ = q.shape
    scale = q.shape[-1] ** -0.5
    delta = jnp.sum(do * o, axis=-1, keepdims=True)
    dq = _attn_dq_call(q, k, v, do, lse, delta, scale)
    dk, dv = _attn_dkv_call(q, k, v, do, lse.reshape(h, 1, lq), delta.reshape(h, 1, lq), scale)
    return dq, dk, dv


attention.defvjp(_attention_fwd, _attention_bwd)


def _na_start(r, rows):
    return jnp.clip(r - NA_WIN_H // 2, 0, rows - NA_WIN_H)


def _na_fwd_call(q, k, v, kc, vc, bias):
    n, width = q.shape
    heads, hd = width // NA_HEAD_DIM, NA_HEAD_DIM
    rows, nwin, nctx = n // GRID_W, NA_WIN_H * GRID_W, kc.shape[0]
    scale = hd ** -0.5
    nt = (((1,), (1,)), ((), ()))

    def body(q_ref, k_ref, v_ref, kc_ref, vc_ref, b_ref, o_ref, lse_ref):
        r = pl.program_id(1)
        start = pl.multiple_of(_na_start(r, rows) * GRID_W, GRID_W)
        qb = q_ref[...].astype(BF16)
        kw = k_ref[pl.ds(start, nwin), :].astype(BF16)
        vw = v_ref[pl.ds(start, nwin), :].astype(BF16)
        sw = lax.dot_general(qb, kw, nt, preferred_element_type=F32) * scale + b_ref[0, 0]
        sc = lax.dot_general(qb, kc_ref[...].astype(BF16), nt, preferred_element_type=F32) * scale
        m = jnp.maximum(jnp.max(sw, axis=-1, keepdims=True), jnp.max(sc, axis=-1, keepdims=True))
        pw = jnp.exp(sw - m)
        pc = jnp.exp(sc - m)
        l = jnp.sum(pw, axis=-1, keepdims=True) + jnp.sum(pc, axis=-1, keepdims=True)
        pw = pw / l
        pc = pc / l
        o_ref[...] = (jnp.dot(pw.astype(BF16), vw, preferred_element_type=F32)
                      + jnp.dot(pc.astype(BF16), vc_ref[...].astype(BF16), preferred_element_type=F32))
        lse_ref[0] = m + jnp.log(l)

    full = lambda rws: pl.BlockSpec((rws, hd), lambda a, r: (0, a))
    return pl.pallas_call(
        body, name="natten_fwd", grid=(heads, rows),
        in_specs=[pl.BlockSpec((GRID_W, hd), lambda a, r: (r, a)), full(n), full(n), full(nctx), full(nctx),
                  pl.BlockSpec((1, 1, GRID_W, nwin), lambda a, r: (a, r - _na_start(r, rows), 0, 0))],
        out_specs=[pl.BlockSpec((GRID_W, hd), lambda a, r: (r, a)), pl.BlockSpec((1, GRID_W, 1), lambda a, r: (a, r, 0))],
        out_shape=[jax.ShapeDtypeStruct((n, width), F32), jax.ShapeDtypeStruct((heads, n, 1), F32)],
        compiler_params=_cparams(("parallel", "arbitrary")),
    )(q, k, v, kc, vc, bias)


def _na_bwd_call(q, k, v, kc, vc, bias, bias_t, do, lse, delta, lse_row, delta_row):
    n, width = q.shape
    heads, hd = width // NA_HEAD_DIM, NA_HEAD_DIM
    rows, nwin, nctx = n // GRID_W, NA_WIN_H * GRID_W, kc.shape[0]
    scale = hd ** -0.5
    nt = (((1,), (1,)), ((), ()))

    def body(q_ref, k_ref, v_ref, kc_ref, vc_ref, b_ref, bt_ref, do_ref, lse_ref, dl_ref, lser_ref, dlr_ref,
             dq_ref, dk_ref, dv_ref, dkc_ref, dvc_ref, db_ref):
        r = pl.program_id(1)
        st = _na_start(r, rows)
        start = pl.multiple_of(st * GRID_W, GRID_W)

        @pl.when(r == 0)
        def _():
            dk_ref[...] = jnp.zeros_like(dk_ref)
            dv_ref[...] = jnp.zeros_like(dv_ref)
            dkc_ref[...] = jnp.zeros_like(dkc_ref)
            dvc_ref[...] = jnp.zeros_like(dvc_ref)

        @pl.when((r <= NA_WIN_H // 2) | (r > rows - NA_WIN_H // 2))
        def _():
            db_ref[...] = jnp.zeros_like(db_ref)

        qb = q_ref[...].astype(BF16)
        dob = do_ref[...].astype(BF16)
        kw = k_ref[pl.ds(start, nwin), :].astype(BF16)
        vw = v_ref[pl.ds(start, nwin), :].astype(BF16)
        kcb = kc_ref[...].astype(BF16)
        vcb = vc_ref[...].astype(BF16)
        lse_c, dl_c = lse_ref[0], dl_ref[0]
        pw = jnp.exp(lax.dot_general(qb, kw, nt, preferred_element_type=F32) * scale + b_ref[0, 0] - lse_c)
        pc = jnp.exp(lax.dot_general(qb, kcb, nt, preferred_element_type=F32) * scale - lse_c)
        dsw = pw * (lax.dot_general(dob, vw, nt, preferred_element_type=F32) - dl_c)
        dsc = pc * (lax.dot_general(dob, vcb, nt, preferred_element_type=F32) - dl_c)
        db_ref[0, 0] += dsw
        dq_ref[...] = (jnp.dot(dsw.astype(BF16), kw, preferred_element_type=F32)
                       + jnp.dot(dsc.astype(BF16), kcb, preferred_element_type=F32)) * scale
        lse_r, dl_r = lser_ref[0, 0], dlr_ref[0, 0]
        pwt = jnp.exp(lax.dot_general(kw, qb, nt, preferred_element_type=F32) * scale + bt_ref[0, 0] - lse_r)
        pct = jnp.exp(lax.dot_general(kcb, qb, nt, preferred_element_type=F32) * scale - lse_r)
        dswt = pwt * (lax.dot_general(vw, dob, nt, preferred_element_type=F32) - dl_r)
        dsct = pct * (lax.dot_general(vcb, dob, nt, preferred_element_type=F32) - dl_r)
        dk_ref[pl.ds(start, nwin), :] += jnp.dot(dswt.astype(BF16), qb, preferred_element_type=F32) * scale
        dv_ref[pl.ds(start, nwin), :] += jnp.dot(pwt.astype(BF16), dob, preferred_element_type=F32)
        dkc_ref[...] += jnp.dot(dsct.astype(BF16), qb, preferred_element_type=F32) * scale
        dvc_ref[...] += jnp.dot(pct.astype(BF16), dob, preferred_element_type=F32)

    full = lambda rws: pl.BlockSpec((rws, hd), lambda a, r: (0, a))
    tile = pl.BlockSpec((GRID_W, hd), lambda a, r: (r, a))
    pat = lambda r: r - _na_start(r, rows)
    col = pl.BlockSpec((1, GRID_W, 1), lambda a, r: (a, r, 0))
    rowv = pl.BlockSpec((1, 1, 1, GRID_W), lambda a, r: (a, r, 0, 0))
    return pl.pallas_call(
        body, name="natten_bwd", grid=(heads, rows),
        in_specs=[tile, full(n), full(n), full(nctx), full(nctx),
                  pl.BlockSpec((1, 1, GRID_W, nwin), lambda a, r: (a, pat(r), 0, 0)),
                  pl.BlockSpec((1, 1, nwin, GRID_W), lambda a, r: (a, pat(r), 0, 0)),
                  tile, col, col, rowv, rowv],
        out_specs=[tile, full(n), full(n), full(nctx), full(nctx),
                   pl.BlockSpec((1, 1, GRID_W, nwin), lambda a, r: (a, pat(r), 0, 0))],
        out_shape=[jax.ShapeDtypeStruct((n, width), F32), jax.ShapeDtypeStruct((n, width), F32),
                   jax.ShapeDtypeStruct((n, width), F32), jax.ShapeDtypeStruct((nctx, width), F32),
                   jax.ShapeDtypeStruct((nctx, width), F32), jax.ShapeDtypeStruct(bias.shape, F32)],
        compiler_params=_cparams(("parallel", "arbitrary")),
    )(q, k, v, kc, vc, bias, bias_t, do, lse, delta, lse_row, delta_row)


@jax.custom_vjp
def natten(q, k, v, kc, vc, bias):
    return _na_fwd_call(q, k, v, kc, vc, bias)[0]


def _natten_fwd(q, k, v, kc, vc, bias):
    o, lse = _na_fwd_call(q, k, v, kc, vc, bias)
    return o, (q, k, v, kc, vc, bias, o, lse)


def _natten_bwd(res, do):
    q, k, v, kc, vc, bias, o, lse = res
    n, width = q.shape
    heads, rows = width // NA_HEAD_DIM, n // GRID_W
    delta = jnp.sum((do * o).reshape(n, heads, NA_HEAD_DIM), axis=-1).T.reshape(heads, n, 1)
    return tuple(_na_bwd_call(q, k, v, kc, vc, bias, jnp.swapaxes(bias, 2, 3), do, lse, delta,
                              lse.reshape(heads, rows, 1, GRID_W), delta.reshape(heads, rows, 1, GRID_W)))


natten.defvjp(_natten_fwd, _natten_bwd)


def na_bias_patterns(rpb):
    heads = rpb.shape[0]
    pid = np.arange(NA_WIN_H)[:, None]
    j = np.arange(NA_WIN_H)[None, :]
    row_idx = j - pid + (NA_WIN_H - 1)
    row_hot = (row_idx[..., None] == np.arange(2 * NA_WIN_H - 1)).astype(np.float32)
    cidx = np.arange(GRID_W)
    c_start = np.clip(cidx - NA_WIN_W // 2, 0, GRID_W - NA_WIN_W)
    in_win = (cidx[None, :] >= c_start[:, None]) & (cidx[None, :] < c_start[:, None] + NA_WIN_W)
    col_idx = np.clip(cidx[None, :] - cidx[:, None], -(NA_WIN_W - 1), NA_WIN_W - 1) + (NA_WIN_W - 1)
    col_hot = (col_idx[..., None] == np.arange(2 * NA_WIN_W - 1)).astype(np.float32)
    hi = lax.Precision.HIGHEST
    tmp = jnp.einsum("hab,pja->hpjb", rpb, jnp.asarray(row_hot), precision=hi)
    bias = jnp.einsum("hpjb,qkb->hpqjk", tmp, jnp.asarray(col_hot), precision=hi)
    bias = jnp.where(jnp.asarray(in_win)[None, None, :, None, :], bias, NEG_INF)
    return bias.reshape(heads, NA_WIN_H, GRID_W, NA_WIN_H * GRID_W)


def adamw(w, g, m, v):
    shape = w.shape
    last = shape[-1]
    r = int(np.prod(shape[:-1]))
    w2, g2, m2, v2 = (t.reshape(1, r, last) for t in (w, g, m, v))
    tr = _tile(r, max(SUBLANE, (1 << 18) // last), SUBLANE)

    def body(w_ref, g_ref, m_ref, v_ref, d_ref, mo_ref, vo_ref):
        gv = g_ref[...]
        mn = ADAM_B1 * m_ref[...] + (1.0 - ADAM_B1) * gv
        vn = ADAM_B2 * v_ref[...] + (1.0 - ADAM_B2) * (gv * gv)
        m_hat = mn / (1.0 - ADAM_B1 ** ADAM_STEP)
        v_hat = vn / (1.0 - ADAM_B2 ** ADAM_STEP)
        d_ref[...] = -ADAM_LR * (m_hat / (jnp.sqrt(v_hat) + ADAM_EPS) + ADAM_WD * w_ref[...])
        mo_ref[...] = mn
        vo_ref[...] = vn

    spec = pl.BlockSpec((1, tr, last), lambda i: (0, i, 0))
    out = pl.pallas_call(
        body, name="adamw", grid=(r // tr,), in_specs=[spec] * 4, out_specs=[spec] * 3,
        out_shape=[jax.ShapeDtypeStruct((1, r, last), F32)] * 3, compiler_params=_cparams(("parallel",)),
    )(w2, g2, m2, v2)
    return tuple(t.reshape(shape) for t in out)


def _place():
    x, y, c = lax.axis_index("x"), lax.axis_index("y"), lax.axis_index("c")
    chips = [(1 - x, y), (x, 1 - y), (1 - x, 1 - y)]
    return x, y, c, chips


HBM_SPEC = pl.BlockSpec(memory_space=pltpu.HBM)


def all_gather_chips(xs, name):
    def body(x_ref, out_ref, send_sems, recv_sems, local_sem):
        x, y, c, chips = _place()
        me = 2 * x + y
        sibling = (x, y, 1 - c)

        def piece(chip, half):
            return out_ref.at[2 * chip[0] + chip[1], half]

        def copy(k, chip, half, to, src=None):
            dst = piece(chip, half)
            return pltpu.make_async_remote_copy(src_ref=dst if src is None else src, dst_ref=dst, send_sem=send_sems.at[k],
                                                recv_sem=recv_sems.at[k], device_id=to, device_id_type=MESH)

        mine = pltpu.make_async_copy(x_ref, out_ref.at[me], local_sem)
        mine.start()
        first = [copy(k, (x, y), c, (*chip, c), src=x_ref.at[c]) for k, chip in enumerate(chips)]
        for cp in first:
            cp.start()
        passed = [copy(3 + k, chip, c, sibling) for k, chip in enumerate(chips)]
        for k, chip in enumerate(chips):
            copy(k, chip, c, sibling).wait_recv()
            passed[k].start()
        for k, chip in enumerate(chips):
            copy(3 + k, chip, 1 - c, sibling).wait_recv()
        for cp in first + passed:
            cp.wait_send()
        mine.wait()

    return pl.pallas_call(
        body, name=name, in_specs=[HBM_SPEC], out_specs=HBM_SPEC,
        out_shape=jax.ShapeDtypeStruct((N_CHIPS,) + xs.shape, xs.dtype),
        scratch_shapes=[pltpu.SemaphoreType.DMA((6,)), pltpu.SemaphoreType.DMA((6,)), pltpu.SemaphoreType.DMA],
        compiler_params=pltpu.CompilerParams(has_side_effects=True),
    )(xs)


def _rs_to_sibling(g, name):
    _, _, r, b = g.shape

    def body(g_ref, out_ref, send_sems, recv_sems):
        x, y, c, _ = _place()
        cps = [pltpu.make_async_remote_copy(src_ref=g_ref.at[s, 1 - c], dst_ref=out_ref.at[s], send_sem=send_sems.at[s],
                                            recv_sem=recv_sems.at[s], device_id=(x, y, 1 - c), device_id_type=MESH)
               for s in range(N_CHIPS)]
        for cp in cps:
            cp.start()
        for cp in cps:
            cp.wait()

    return pl.pallas_call(
        body, name=name, in_specs=[HBM_SPEC], out_specs=HBM_SPEC, out_shape=jax.ShapeDtypeStruct((N_CHIPS, r, b), g.dtype),
        scratch_shapes=[pltpu.SemaphoreType.DMA((N_CHIPS,)), pltpu.SemaphoreType.DMA((N_CHIPS,))],
        compiler_params=pltpu.CompilerParams(has_side_effects=True),
    )(g)


def _rs_chip_sum(g, la, c_idx, name):
    _, _, r, b = g.shape
    tr = _tile(r, max(16, (1 << 18) // b), 16)

    def body(c_ref, g_ref, la_ref, o_ref):
        o_ref[...] = (g_ref[:, 0] + la_ref[...]).astype(BF16)

    return pl.pallas_call(
        body, name=name,
        grid_spec=pltpu.PrefetchScalarGridSpec(
            num_scalar_prefetch=1, grid=(N_CHIPS, r // tr),
            in_specs=[pl.BlockSpec((1, 1, tr, b), lambda s, i, c_ref: (s, c_ref[0], i, 0)),
                      pl.BlockSpec((1, tr, b), lambda s, i, c_ref: (s, i, 0))],
            out_specs=pl.BlockSpec((1, tr, b), lambda s, i, c_ref: (s, i, 0))),
        out_shape=jax.ShapeDtypeStruct((N_CHIPS, r, b), BF16), compiler_params=_cparams(("parallel", "parallel")),
    )(c_idx, g, la)


def _rs_to_owners(hb, name):
    _, r, b = hb.shape

    def body(h_ref, out_ref, send_sems, recv_sems):
        x, y, c, chips = _place()
        cps = [pltpu.make_async_remote_copy(src_ref=h_ref.at[2 * chip[0] + chip[1]], dst_ref=out_ref.at[k],
                                            send_sem=send_sems.at[k], recv_sem=recv_sems.at[k], device_id=(*chip, c),
                                            device_id_type=MESH) for k, chip in enumerate(chips)]
        for cp in cps:
            cp.start()
        for cp in cps:
            cp.wait()

    return pl.pallas_call(
        body, name=name, in_specs=[HBM_SPEC], out_specs=HBM_SPEC, out_shape=jax.ShapeDtypeStruct((3, r, b), hb.dtype),
        scratch_shapes=[pltpu.SemaphoreType.DMA((3,)), pltpu.SemaphoreType.DMA((3,))],
        compiler_params=pltpu.CompilerParams(has_side_effects=True),
    )(hb)


def _rs_final_sum(g, la, lb, sc_idx, name):
    _, _, r, b = g.shape
    tr = _tile(r, max(16, (1 << 18) // b), 16)

    def body(i_ref, g_ref, la_ref, lb_ref, o_ref):
        o_ref[...] = (g_ref[0, 0] + la_ref[0]) + lb_ref[0].astype(F32) + lb_ref[1].astype(F32) + lb_ref[2].astype(F32)

    return pl.pallas_call(
        body, name=name,
        grid_spec=pltpu.PrefetchScalarGridSpec(
            num_scalar_prefetch=1, grid=(r // tr,),
            in_specs=[pl.BlockSpec((1, 1, tr, b), lambda i, i_ref: (i_ref[0], i_ref[1], i, 0)),
                      pl.BlockSpec((1, tr, b), lambda i, i_ref: (i_ref[0], i, 0)),
                      pl.BlockSpec((3, tr, b), lambda i, i_ref: (0, i, 0))],
            out_specs=pl.BlockSpec((tr, b), lambda i, i_ref: (i, 0))),
        out_shape=jax.ShapeDtypeStruct((r, b), F32), compiler_params=_cparams(("parallel",)),
    )(sc_idx, g, la, lb)


def _rs_join_halves(f, name):
    r, b = f.shape

    def body(f_ref, out_ref, send_sem, recv_sem, local_sem):
        x, y, c, _ = _place()
        mine = pltpu.make_async_copy(f_ref, out_ref.at[c], local_sem)
        mine.start()
        cp = pltpu.make_async_remote_copy(src_ref=f_ref, dst_ref=out_ref.at[c], send_sem=send_sem, recv_sem=recv_sem,
                                          device_id=(x, y, 1 - c), device_id_type=MESH)
        cp.start()
        pltpu.make_async_remote_copy(src_ref=f_ref, dst_ref=out_ref.at[1 - c], send_sem=send_sem, recv_sem=recv_sem,
                                     device_id=(x, y, 1 - c), device_id_type=MESH).wait_recv()
        cp.wait_send()
        mine.wait()

    return pl.pallas_call(
        body, name=name, in_specs=[HBM_SPEC], out_specs=HBM_SPEC, out_shape=jax.ShapeDtypeStruct((2, r, b), f.dtype),
        scratch_shapes=[pltpu.SemaphoreType.DMA, pltpu.SemaphoreType.DMA, pltpu.SemaphoreType.DMA],
        compiler_params=pltpu.CompilerParams(has_side_effects=True),
    )(f)


def reduce_scatter_devices(g, tag):
    c = lax.axis_index("c")
    me = 2 * lax.axis_index("x") + lax.axis_index("y")
    la = _rs_to_sibling(g, "rs_sibling_" + tag)
    hb = _rs_chip_sum(g, la, jnp.stack([c]).astype(jnp.int32), "rs_chipsum_" + tag)
    lb = _rs_to_owners(hb, "rs_owners_" + tag)
    f = _rs_final_sum(g, la, lb, jnp.stack([me, c]).astype(jnp.int32), "rs_final_" + tag)
    return _rs_join_halves(f, "rs_join_" + tag)


W_IN_SIZES = ("na", "lru_x", "lru_g", "cq", "ckv", "kr", "gates")


def _w_in_layout(d_model, na_width, lru_width, q_rank, kv_rank):
    sizes = dict(na=3 * na_width, lru_x=lru_width, lru_g=lru_width, cq=q_rank, ckv=kv_rank, kr=MLA_ROPE_DIM,
                 gates=N_BRANCH * d_model)
    head = sizes["na"] + 2 * lru_width + q_rank + kv_rank
    total = head + MLA_ROPE_DIM + sizes["gates"]
    padded = -(-total // LANE) * LANE
    return sizes, head, total, padded


def _permute_w_in(w, head, total, padded):
    parts = [w[:, :head], w[:, head + MLA_ROPE_DIM:total], w[:, head:head + MLA_ROPE_DIM]]
    if padded > total:
        parts.append(jnp.zeros((w.shape[0], padded - total), w.dtype))
    return jnp.concatenate(parts, axis=1)


def _unpermute_w_in(g, head, total):
    n_gates = total - head - MLA_ROPE_DIM
    return jnp.concatenate([g[:, :head], g[:, head + n_gates:head + n_gates + MLA_ROPE_DIM], g[:, head:head + n_gates]], axis=1)


def _dwconv(x, w, b):
    width = w.shape[0]
    n = x.shape[0]
    left = width // 2
    xp = jnp.pad(x, ((left, width - 1 - left), (0, 0)))
    return sum(xp[i:i + n] * w[i] for i in range(width)) + b


def _dwconv_seg(x, w, b, n_lat):
    return jnp.concatenate([_dwconv(x[:n_lat], w, b), _dwconv(x[n_lat:], w, b)], axis=0)


def _rope(x, cos, sin):
    half = x.shape[-1] // 2
    x1, x2 = x[..., :half], x[..., half:]
    return jnp.concatenate([x1 * cos - x2 * sin, x1 * sin + x2 * cos], axis=-1)


def _rope_tables(n_lat, n_ctx):
    t = jnp.arange(n_lat, dtype=jnp.int32)
    row = (t // GRID_W).astype(F32)
    col = (t % GRID_W).astype(F32)
    n_freq = MLA_ROPE_DIM // 4
    inv_freq = ROPE_THETA ** (-jnp.arange(n_freq, dtype=F32) / n_freq)
    ang = jnp.concatenate([row[:, None] * inv_freq, col[:, None] * inv_freq], axis=-1)
    ones = jnp.ones((n_ctx, MLA_ROPE_DIM // 2), F32)
    return jnp.concatenate([jnp.cos(ang), ones], axis=0), jnp.concatenate([jnp.sin(ang), 0.0 * ones], axis=0)


def _heads_first(t, heads):
    n = t.shape[0]
    return t.reshape(n, heads, -1).transpose(1, 0, 2)


def _heads_last(t):
    return t.transpose(1, 0, 2).reshape(t.shape[1], -1)


def _lru_coeffs(u, w_a, b_a, w_x, b_x, lam):
    r = jax.nn.sigmoid(bd_dot(u, w_a) + b_a)
    i = jax.nn.sigmoid(bd_dot(u, w_x) + b_x)
    log_a = -LRU_C * r * jax.nn.softplus(-lam)
    return jnp.exp(log_a), jnp.sqrt(-jnp.expm1(2.0 * log_a)) * (i * u)


def _forward_loss(big, prox, small, x, ctx, silu_c, target):
    n_lat, d_model = x.shape
    n_ctx = ctx.shape[0]
    na_width = NA_HEADS * NA_HEAD_DIM
    lru_width = small["lru_conv_b"].shape[1]
    q_rank, kv_rank = small["mla_q_norm"].shape[1], small["mla_kv_norm"].shape[1]
    sizes, head, total, padded = _w_in_layout(d_model, na_width, lru_width, q_rank, kv_rank)
    cos, sin = _rope_tables(n_lat, n_ctx)
    is_ctx = (jnp.arange(n_lat + n_ctx) >= n_lat)[:, None]

    def by_row(v2):
        return jnp.where(is_ctx, v2[1][None, :], v2[0][None, :])

    s_rows = jnp.concatenate([silu_c, jax.nn.silu(small["c_ctx"])[None, :],
                              jnp.zeros((MOD_ROWS - 2, d_model), F32)], axis=0)
    xs = jnp.concatenate([x, ctx], axis=0)
    for l in range(DEPTH):
        mod = (pdot(s_rows, big["w_mod"][l], prox["w_mod"][l]) + small["b_mod"][l])[:2]
        sh1, sc1, g1, sh2, sc2, g2 = jnp.split(mod, N_MOD, axis=-1)
        h = norm_mod(xs, small["norm_mix"][l], sh1, sc1, n_lat)
        z = pdot(h, big["w_in"][l], prox["w_in"][l])
        o = 0
        cols = {}
        for name in ("na", "lru_x", "lru_g", "cq", "ckv", "gates", "kr"):
            cols[name] = z[:, o:o + sizes[name]]
            o += sizes[name]
        q_a, k_a, v_a = (cols["na"][:, i * na_width:(i + 1) * na_width] for i in range(3))
        out_a_lat = natten(q_a[:n_lat], k_a[:n_lat], v_a[:n_lat], k_a[n_lat:], v_a[n_lat:], na_bias_patterns(small["na_rpb"][l]))
        out_a_ctx = _heads_last(attention(*(_heads_first(t[n_lat:], NA_HEADS) for t in (q_a, k_a, v_a))))
        out_a = jnp.concatenate([out_a_lat, out_a_ctx], axis=0)
        u = _dwconv_seg(cols["lru_x"], small["lru_conv_w"][l], small["lru_conv_b"][l], n_lat)
        seqs = (jnp.concatenate([u[n_lat:], u[:n_lat]], axis=0),
                jnp.concatenate([jnp.flip(u[n_lat:], 0), jnp.flip(u[:n_lat], 0)], axis=0))
        hs = [linrec(*_lru_coeffs(seqs[d], small["lru_w_a"][l, d], small["lru_b_a"][l, d], small["lru_w_x"][l, d],
                                  small["lru_b_x"][l, d], small["lru_lam"][l, d])) for d in range(2)]
        y_b = jnp.concatenate([hs[0][n_ctx:] + jnp.flip(hs[1][n_ctx:], 0), hs[0][:n_ctx] + jnp.flip(hs[1][:n_ctx], 0)], axis=0)
        out_b = jax.nn.gelu(cols["lru_g"]) * y_b
        q_m = pdot(rmsnorm_p(cols["cq"], small["mla_q_norm"][l]), big["mla_w_q_up"][l], prox["mla_w_q_up"][l])
        q_m = q_m.reshape(-1, MLA_HEADS, MLA_NOPE_DIM + MLA_ROPE_DIM)
        q_m = jnp.concatenate([q_m[..., :MLA_NOPE_DIM], _rope(q_m[..., MLA_NOPE_DIM:], cos[:, None, :], sin[:, None, :])], axis=-1)
        kv = pdot(rmsnorm_p(cols["ckv"], small["mla_kv_norm"][l]), big["mla_w_kv_up"][l], prox["mla_w_kv_up"][l])
        kv = kv.reshape(-1, MLA_HEADS, MLA_NOPE_DIM + MLA_V_DIM)
        k_rope = jnp.broadcast_to(_rope(cols["kr"], cos, sin)[:, None, :], (n_lat + n_ctx, MLA_HEADS, MLA_ROPE_DIM))
        k_m = jnp.concatenate([kv[..., :MLA_NOPE_DIM], k_rope], axis=-1).transpose(1, 0, 2)
        v_m = kv[..., MLA_NOPE_DIM:].transpose(1, 0, 2)
        q_m = q_m.transpose(1, 0, 2)
        out_c = jnp.concatenate([_heads_last(attention(q_m[:, :n_lat], k_m, v_m)),
                                 _heads_last(attention(q_m[:, n_lat:], k_m[:, n_lat:], v_m[:, n_lat:]))], axis=0)
        gates = jax.nn.sigmoid(cols["gates"])
        y = sum(gates[:, i * d_model:(i + 1) * d_model] * pdot(br, big["w_branch"][l][i], prox["w_branch"][l][i])
                for i, br in enumerate((out_a, out_b, out_c)))
        xs = xs + by_row(g1) * pdot(y, big["w_out"][l], prox["w_out"][l])
        h2 = norm_mod(xs, small["norm_ffn"][l], sh2, sc2, n_lat)
        up = _dwconv_seg(pdot(h2, big["ffn_w_up"][l], prox["ffn_w_up"][l]), small["ffn_conv_w"][l], small["ffn_conv_b"][l], n_lat)
        val, gate = jnp.split(up, 2, axis=-1)
        xs = xs + by_row(g2) * pdot(jax.nn.silu(gate) * val, big["ffn_w_down"][l], prox["ffn_w_down"][l])
    y_out = rmsnorm_p(xs[:n_lat], small["norm_final"])
    return 0.5 * jnp.sum(jnp.mean(jnp.square(y_out - target), axis=-1))


WEIGHTS = ['c_ctx', 'w_mod', 'b_mod', 'norm_mix', 'norm_ffn', 'w_in', 'na_rpb', 'lru_conv_w', 'lru_conv_b', 'lru_w_a',
           'lru_b_a', 'lru_w_x', 'lru_b_x', 'lru_lam', 'mla_q_norm', 'mla_kv_norm', 'mla_w_q_up', 'mla_w_kv_up', 'w_branch',
           'w_out', 'ffn_w_up', 'ffn_conv_w', 'ffn_conv_b', 'ffn_w_down', 'norm_final']
BIG = {'w_mod': 2, 'w_in': 2, 'mla_w_q_up': 2, 'mla_w_kv_up': 2, 'w_branch': 3, 'w_out': 1, 'ffn_w_up': 2, 'ffn_w_down': 1}
SMALL_SHARDED = {'lru_conv_w': 2, 'lru_b_a': 2, 'lru_b_x': 2, 'lru_lam': 2, 'ffn_conv_w': 2}
PACK_LANES = 128
PACK_ROWS = 16


def _gathered_to_full(g, ax):
    t = jnp.moveaxis(g, 0, ax)
    return t.reshape(t.shape[:ax] + (t.shape[ax] * t.shape[ax + 1],) + t.shape[ax + 2:])


def _full_to_stacked(g, ax):
    n = g.shape[ax]
    t = g.reshape(g.shape[:ax] + (N_CHIPS, n // N_CHIPS) + g.shape[ax + 1:])
    return jnp.moveaxis(t, ax, 0)


def _canon(shape):
    return (2, int(np.prod(shape[:-1])) // 2, shape[-1])


def _pack(parts, multiple):
    flat = jnp.concatenate([p.reshape(-1) for p in parts])
    n = flat.shape[0]
    padded = -(-n // multiple) * multiple
    return jnp.pad(flat, (0, padded - n))


def _unpack(flat, shapes):
    out, o = [], 0
    for s in shapes:
        n = int(np.prod(s))
        out.append(flat[o:o + n].reshape(s))
        o += n
    return out


def kernel(x, c, ctx, c_ctx, w_mod, b_mod, norm_mix, norm_ffn, w_in, na_rpb, lru_conv_w, lru_conv_b, lru_w_a, lru_b_a, lru_w_x, lru_b_x, lru_lam, mla_q_norm, mla_kv_norm, mla_w_q_up, mla_w_kv_up, w_branch, w_out, ffn_w_up, ffn_conv_w, ffn_conv_b, ffn_w_down, norm_final, loss_target, m_c_ctx, m_w_mod, m_b_mod, m_norm_mix, m_norm_ffn, m_w_in, m_na_rpb, m_lru_conv_w, m_lru_conv_b, m_lru_w_a, m_lru_b_a, m_lru_w_x, m_lru_b_x, m_lru_lam, m_mla_q_norm, m_mla_kv_norm, m_mla_w_q_up, m_mla_w_kv_up, m_w_branch, m_w_out, m_ffn_w_up, m_ffn_conv_w, m_ffn_conv_b, m_ffn_w_down, m_norm_final, v_c_ctx, v_w_mod, v_b_mod, v_norm_mix, v_norm_ffn, v_w_in, v_na_rpb, v_lru_conv_w, v_lru_conv_b, v_lru_w_a, v_lru_b_a, v_lru_w_x, v_lru_b_x, v_lru_lam, v_mla_q_norm, v_mla_kv_norm, v_mla_w_q_up, v_mla_w_kv_up, v_w_branch, v_w_out, v_ffn_w_up, v_ffn_conv_w, v_ffn_conv_b, v_ffn_w_down, v_norm_final):
    args = dict(locals())
    w = {n: args[n] for n in WEIGHTS}
    m = {n: args["m_" + n] for n in WEIGHTS}
    v = {n: args["v_" + n] for n in WEIGHTS}
    me = 2 * lax.axis_index("x") + lax.axis_index("y")

    full = {}
    for name, ax in BIG.items():
        shard = w[name]
        g = all_gather_chips(shard.astype(BF16).reshape(_canon(shard.shape)), "ag_" + name)
        full[name] = _gathered_to_full(g.reshape((N_CHIPS,) + shard.shape), ax)
    packed = _pack([w[n] for n in SMALL_SHARDED], 2 * PACK_ROWS * PACK_LANES).reshape(2, -1, PACK_LANES)
    g = all_gather_chips(packed, "ag_small").reshape(N_CHIPS, -1)
    for name, t in zip(SMALL_SHARDED, zip(*[_unpack(g[s], [w[n].shape for n in SMALL_SHARDED]) for s in range(N_CHIPS)])):
        full[name] = _gathered_to_full(jnp.stack(t), SMALL_SHARDED[name])
    small = {n: full.get(n, w[n]) for n in WEIGHTS if n not in BIG}

    d_model = x.shape[-1]
    _, head, total, padded = _w_in_layout(d_model, NA_HEADS * NA_HEAD_DIM, lru_conv_b.shape[1], mla_q_norm.shape[1], mla_kv_norm.shape[1])
    big = {n: [full[n][l] for l in range(DEPTH)] for n in BIG}
    big["w_in"] = [_permute_w_in(t, head, total, padded) for t in big["w_in"]]
    big["w_branch"] = [[t[i] for i in range(N_BRANCH)] for t in big["w_branch"]]
    prox = jax.tree.map(lambda t: jnp.zeros(t.shape, F32), big)

    loss, (g_prox, g_small, g_x) = jax.value_and_grad(_forward_loss, argnums=(1, 2, 3))(
        big, prox, small, x[0], ctx[0], jax.nn.silu(c), loss_target[0])
    loss = lax.psum(loss, ("x", "y", "c"))

    grads = {}
    g_prox["w_in"] = [_unpermute_w_in(t, head, total) for t in g_prox["w_in"]]
    g_prox["w_branch"] = [jnp.stack(t) for t in g_prox["w_branch"]]
    for name, ax in BIG.items():
        stacked = _full_to_stacked(jnp.stack(g_prox[name]), ax)
        red = reduce_scatter_devices(stacked.reshape((N_CHIPS,) + _canon(w[name].shape)), name)
        grads[name] = red.reshape(w[name].shape)
    small_names = [n for n in WEIGHTS if n not in BIG]
    packed = _pack([g_small[n] for n in small_names], N_CHIPS * 2 * PACK_ROWS * PACK_LANES).reshape(N_CHIPS, 2, -1, PACK_LANES)
    red = all_gather_chips(reduce_scatter_devices(packed, "small"), "ag_small_grads").reshape(-1)
    for name, t in zip(small_names, _unpack(red, [g_small[n].shape for n in small_names])):
        if name in SMALL_SHARDED:
            ax = SMALL_SHARDED[name]
            t = lax.dynamic_slice_in_dim(t, me * w[name].shape[ax], w[name].shape[ax], axis=ax)
        grads[name] = t

    upd = {n: adamw(w[n], grads[n], m[n], v[n]) for n in WEIGHTS}
    return (loss, g_x[None], *[grads[n] for n in WEIGHTS], *[upd[n][0] for n in WEIGHTS],
            *[upd[n][1] for n in WEIGHTS], *[upd[n][2] for n in WEIGHTS])
```

```python
import functools
import math

import numpy as np
import jax
import jax.numpy as jnp
from jax import lax
from jax.experimental import pallas as pl
from jax.experimental.pallas import tpu as pltpu

F32 = jnp.float32
BF16 = jnp.bfloat16
MESH = pl.DeviceIdType.MESH

DEPTH = 4
GRID_W = 64
NORM_EPS = 1e-6
NEG_INF = -1e30
N_MOD = 6
NA_HEADS = 8
NA_HEAD_DIM = 128
NA_WIN_H = 8
NA_WIN_W = 16
LRU_BLOCKS = 8
LRU_CONV_W = 4
LRU_C = 8.0
MLA_HEADS = 8
MLA_NOPE_DIM = 128
MLA_ROPE_DIM = 64
MLA_V_DIM = 128
ROPE_THETA = 10000.0
N_BRANCH = 3
FFN_CONV_W = 3
ADAM_LR = 0.001
ADAM_B1 = 0.9
ADAM_B2 = 0.999
ADAM_EPS = 1e-08
ADAM_WD = 0.01
ADAM_STEP = 10

N_CHIPS = 4
LANE = 128
SUBLANE = 8
VMEM_LIMIT = 48 * 1024 * 1024
MOD_ROWS = 128


def _cparams(sem=None, **kw):
    if sem is not None:
        kw["dimension_semantics"] = sem
    return pltpu.CompilerParams(vmem_limit_bytes=VMEM_LIMIT, **kw)


def _tile(d, cap, align):
    best = None
    t = align
    while t <= min(d, cap):
        if d % t == 0:
            best = t
        t += align
    return d if best is None else best


FULL_K_MAX = 2304


def _mm_rows(a, b, *, tb, emit, name):
    m, k = a.shape
    n = b.shape[0] if tb else b.shape[1]
    tm = _tile(m, 544, 16)
    tn = _tile(n, 640, LANE)
    dims = (((1,), (1 if tb else 0,)), ((), ()))

    def body(a_ref, b_ref, o_ref, *rest):
        a_sc = rest[-1]

        @pl.when(pl.program_id(1) == 0)
        def _():
            a_sc[...] = a_ref[...].astype(BF16)
            if emit:
                rest[0][...] = a_sc[...]

        o_ref[...] = lax.dot_general(a_sc[...], b_ref[...].astype(BF16), dims, preferred_element_type=F32)

    b_spec = pl.BlockSpec((tn, k), lambda i, j: (j, 0)) if tb else pl.BlockSpec((k, tn), lambda i, j: (0, j))
    out_specs = [pl.BlockSpec((tm, tn), lambda i, j: (i, j))]
    out_shape = [jax.ShapeDtypeStruct((m, n), F32)]
    if emit:
        out_specs.append(pl.BlockSpec((tm, k), lambda i, j: (i, 0)))
        out_shape.append(jax.ShapeDtypeStruct((m, k), BF16))
    out = pl.pallas_call(
        body, name=name, grid=(m // tm, n // tn),
        in_specs=[pl.BlockSpec((tm, k), lambda i, j: (i, 0)), b_spec], out_specs=out_specs, out_shape=out_shape,
        scratch_shapes=[pltpu.VMEM((tm, k), BF16)], compiler_params=_cparams(("parallel", "arbitrary")),
    )(a, b)
    return out if emit else out[0]


def _mm_acc(a, b, *, ta=False, tb=False, name):
    m, k = (a.shape[1], a.shape[0]) if ta else a.shape
    n = b.shape[0] if tb else b.shape[1]
    tm = _tile(m, 2048 if ta else 1088, 16)
    tn = _tile(n, 640 if ta else 2048, LANE)
    tk = _tile(k, 256 if ta else 640, LANE)
    dims = (((0 if ta else 1,), (1 if tb else 0,)), ((), ()))

    def body(a_ref, b_ref, o_ref):
        @pl.when(pl.program_id(2) == 0)
        def _():
            o_ref[...] = jnp.zeros_like(o_ref)

        o_ref[...] += lax.dot_general(a_ref[...].astype(BF16), b_ref[...].astype(BF16), dims, preferred_element_type=F32)

    a_spec = pl.BlockSpec((tk, tm), lambda i, j, kk: (kk, i)) if ta else pl.BlockSpec((tm, tk), lambda i, j, kk: (i, kk))
    b_spec = pl.BlockSpec((tn, tk), lambda i, j, kk: (j, kk)) if tb else pl.BlockSpec((tk, tn), lambda i, j, kk: (kk, j))
    return pl.pallas_call(
        body, name=name, grid=(m // tm, n // tn, k // tk),
        in_specs=[a_spec, b_spec], out_specs=pl.BlockSpec((tm, tn), lambda i, j, kk: (i, j)),
        out_shape=jax.ShapeDtypeStruct((m, n), F32),
        compiler_params=_cparams(("parallel", "parallel", "arbitrary")),
    )(a, b)


def _pdot_fwd(a, w, wp):
    if a.shape[1] <= FULL_K_MAX:
        out, a_bf = _mm_rows(a, w, tb=False, emit=True, name="mm_fwd")
        return out, (a_bf, w)
    return _mm_acc(a, w, name="mm_fwd_acc"), (a, w)


@jax.custom_vjp
def pdot(a, w, wp):
    return _pdot_fwd(a, w, wp)[0]


def _pdot_bwd(res, g):
    a, w = res
    if w.shape[1] <= FULL_K_MAX:
        da = _mm_rows(g, w, tb=True, emit=False, name="mm_dgrad")
    else:
        da = _mm_acc(g, w, tb=True, name="mm_dgrad_acc")
    return da, jnp.zeros_like(w), _mm_acc(a, g, ta=True, name="mm_wgrad")


pdot.defvjp(_pdot_fwd, _pdot_bwd)


def _bd_mm(x, w, name):
    t, c = x.shape
    nb, bd, _ = w.shape
    tm = _tile(t, 1088, 16)

    def body(x_ref, w_ref, o_ref):
        o_ref[...] = jnp.dot(x_ref[...].astype(BF16), w_ref[0].astype(BF16), preferred_element_type=F32)

    return pl.pallas_call(
        body, name=name, grid=(t // tm, nb),
        in_specs=[pl.BlockSpec((tm, bd), lambda i, k: (i, k)), pl.BlockSpec((1, bd, bd), lambda i, k: (k, 0, 0))],
        out_specs=pl.BlockSpec((tm, bd), lambda i, k: (i, k)),
        out_shape=jax.ShapeDtypeStruct((t, c), F32),
        compiler_params=_cparams(("parallel", "parallel")),
    )(x, w)


def _bd_wgrad(x, g, nb, name):
    t, c = x.shape
    bd = c // nb
    tk = _tile(t, 1088, 16)
    nt = t // tk

    def body(x_ref, g_ref, o_ref):
        @pl.when(pl.program_id(1) == 0)
        def _():
            o_ref[...] = jnp.zeros_like(o_ref)

        xt = x_ref[...].T.astype(BF16)
        o_ref[0] += jnp.dot(xt, g_ref[...].astype(BF16), preferred_element_type=F32)

    return pl.pallas_call(
        body, name=name, grid=(nb, nt),
        in_specs=[pl.BlockSpec((tk, bd), lambda k, i: (i, k)), pl.BlockSpec((tk, bd), lambda k, i: (i, k))],
        out_specs=pl.BlockSpec((1, bd, bd), lambda k, i: (k, 0, 0)),
        out_shape=jax.ShapeDtypeStruct((nb, bd, bd), F32),
        compiler_params=_cparams(("parallel", "arbitrary")),
    )(x, g)


@jax.custom_vjp
def bd_dot(x, w):
    return _bd_mm(x, w, "bd_fwd")


def _bd_fwd(x, w):
    return _bd_mm(x, w, "bd_fwd"), (x, w)


def _bd_bwd(res, g):
    x, w = res
    return _bd_mm(g, jnp.swapaxes(w, 1, 2), "bd_dgrad"), _bd_wgrad(x, g, w.shape[0], "bd_wgrad")


bd_dot.defvjp(_bd_fwd, _bd_bwd)


def _nm_tiles(t, seg_rows):
    tm = _tile(math.gcd(t, seg_rows), 256, SUBLANE)
    return tm, seg_rows // tm


def _nm_fwd_call(x, g, shift, scale, seg_rows):
    t, d = x.shape
    tm, seg_blocks = _nm_tiles(t, seg_rows)
    nseg = shift.shape[0]

    def seg(i):
        return jnp.minimum(i // seg_blocks, nseg - 1)

    def body(x_ref, g_ref, sh_ref, sc_ref, o_ref):
        xv = x_ref[...]
        y = xv * lax.rsqrt(jnp.mean(xv * xv, axis=-1, keepdims=True) + NORM_EPS)
        o_ref[...] = (y * g_ref[...]) * (1.0 + sc_ref[0]) + sh_ref[0]

    return pl.pallas_call(
        body, name="norm_mod_fwd", grid=(t // tm,),
        in_specs=[pl.BlockSpec((tm, d), lambda i: (i, 0)), pl.BlockSpec((1, d), lambda i: (0, 0)),
                  pl.BlockSpec((1, 1, d), lambda i: (seg(i), 0, 0)), pl.BlockSpec((1, 1, d), lambda i: (seg(i), 0, 0))],
        out_specs=pl.BlockSpec((tm, d), lambda i: (i, 0)),
        out_shape=jax.ShapeDtypeStruct((t, d), F32),
        compiler_params=_cparams(("parallel",)),
    )(x, g.reshape(1, d), shift.reshape(nseg, 1, d), scale.reshape(nseg, 1, d))


def _nm_bwd_call(x, g, scale, dy, seg_rows):
    t, d = x.shape
    tm, seg_blocks = _nm_tiles(t, seg_rows)
    nseg = scale.shape[0]

    def seg(i):
        return jnp.minimum(i // seg_blocks, nseg - 1)

    def body(x_ref, g_ref, sc_ref, dy_ref, dx_ref, dg_ref, dsh_ref, dsc_ref):
        i = pl.program_id(0)

        @pl.when(i == 0)
        def _():
            dg_ref[...] = jnp.zeros_like(dg_ref)

        @pl.when((i == 0) | (i == seg_blocks))
        def _():
            dsh_ref[...] = jnp.zeros_like(dsh_ref)
            dsc_ref[...] = jnp.zeros_like(dsc_ref)

        xv = x_ref[...]
        dyv = dy_ref[...]
        gv = g_ref[...]
        rstd = lax.rsqrt(jnp.mean(xv * xv, axis=-1, keepdims=True) + NORM_EPS)
        xhat = xv * rstd
        dsh_ref[0] += jnp.sum(dyv, axis=0, keepdims=True)
        dsc_ref[0] += jnp.sum(dyv * (xhat * gv), axis=0, keepdims=True)
        dn = dyv * (1.0 + sc_ref[0])
        dg_ref[...] += jnp.sum(dn * xhat, axis=0, keepdims=True)
        dxh = dn * gv
        dx_ref[...] = rstd * (dxh - xhat * jnp.mean(dxh * xhat, axis=-1, keepdims=True))

    return pl.pallas_call(
        body, name="norm_mod_bwd", grid=(t // tm,),
        in_specs=[pl.BlockSpec((tm, d), lambda i: (i, 0)), pl.BlockSpec((1, d), lambda i: (0, 0)),
                  pl.BlockSpec((1, 1, d), lambda i: (seg(i), 0, 0)), pl.BlockSpec((tm, d), lambda i: (i, 0))],
        out_specs=[pl.BlockSpec((tm, d), lambda i: (i, 0)), pl.BlockSpec((1, d), lambda i: (0, 0)),
                   pl.BlockSpec((1, 1, d), lambda i: (seg(i), 0, 0)), pl.BlockSpec((1, 1, d), lambda i: (seg(i), 0, 0))],
        out_shape=[jax.ShapeDtypeStruct((t, d), F32), jax.ShapeDtypeStruct((1, d), F32),
                   jax.ShapeDtypeStruct((nseg, 1, d), F32), jax.ShapeDtypeStruct((nseg, 1, d), F32)],
        compiler_params=_cparams(("arbitrary",)),
    )(x, g.reshape(1, d), scale.reshape(nseg, 1, d), dy)


@functools.partial(jax.custom_vjp, nondiff_argnums=(4,))
def norm_mod(x, g, shift, scale, seg_rows):
    return _nm_fwd_call(x, g, shift, scale, seg_rows)


def _norm_mod_fwd(x, g, shift, scale, seg_rows):
    return _nm_fwd_call(x, g, shift, scale, seg_rows), (x, g, scale)


def _norm_mod_bwd(seg_rows, res, dy):
    x, g, scale = res
    dx, dg, dsh, dsc = _nm_bwd_call(x, g, scale, dy, seg_rows)
    return dx, dg.reshape(g.shape), dsh.reshape(scale.shape), dsc.reshape(scale.shape)


norm_mod.defvjp(_norm_mod_fwd, _norm_mod_bwd)


def rmsnorm_p(x, g):
    z = jnp.zeros((1, x.shape[1]), F32)
    return norm_mod(x, g, z, z, x.shape[0])


def _linrec_call(a, b, n_lat, mode, name):
    t, c = a.shape
    tb = _tile(math.gcd(t, n_lat), 512, SUBLANE)
    nb, nl = t // tb, n_lat // tb
    reverse = mode in ("B", "C")
    tiles = tb // SUBLANE

    def block(i):
        if mode == "A":
            return (i + nl) % nb
        if mode == "B":
            return nb - 1 - i
        if mode == "C":
            return jnp.where(i < nl, nl - 1 - i, nb - 1 - (i - nl))
        return i

    def body(a_ref, b_ref, h_ref, carry_ref):
        @pl.when(pl.program_id(0) == 0)
        def _():
            carry_ref[...] = jnp.zeros_like(carry_ref)

        def tile(i, carry):
            r = pl.multiple_of((tiles - 1 - i if reverse else i) * SUBLANE, SUBLANE)
            av = a_ref[pl.ds(r, SUBLANE), :]
            bv = b_ref[pl.ds(r, SUBLANE), :]
            row = lax.broadcasted_iota(jnp.int32, av.shape, 0)
            d = 1
            while d < SUBLANE:
                shift = SUBLANE - d if reverse else d
                a_sh = pltpu.roll(av, shift, 0)
                b_sh = pltpu.roll(bv, shift, 0)
                m = (row < SUBLANE - d) if reverse else (row >= d)
                bv = jnp.where(m, av * b_sh + bv, bv)
                av = jnp.where(m, av * a_sh, av)
                d *= 2
            hv = av * carry + bv
            h_ref[pl.ds(r, SUBLANE), :] = hv
            return jnp.sum(jnp.where(row == (0 if reverse else SUBLANE - 1), hv, 0.0), axis=0, keepdims=True)

        carry_ref[...] = lax.fori_loop(0, tiles, tile, carry_ref[...])

    spec = pl.BlockSpec((tb, c), lambda i: (block(i), 0))
    return pl.pallas_call(
        body, name=name, grid=(nb,), in_specs=[spec, spec], out_specs=spec,
        out_shape=jax.ShapeDtypeStruct((t, c), F32), scratch_shapes=[pltpu.VMEM((1, c), F32)],
        compiler_params=_cparams(("arbitrary",)),
    )(a, b)


@functools.partial(jax.custom_vjp, nondiff_argnums=(2, 3))
def linrec(a, b, n_lat, direction):
    return _linrec_call(a, b, n_lat, "AB"[direction], "linrec_fwd")


def _linrec_fwd(a, b, n_lat, direction):
    h = _linrec_call(a, b, n_lat, "AB"[direction], "linrec_fwd")
    return h, (a, h)


def _linrec_bwd(n_lat, direction, res, dh):
    a, h = res
    zero = jnp.zeros_like(a[:1])
    if direction == 0:
        a_next = jnp.concatenate([a[1:n_lat], zero, a[n_lat + 1:], a[:1]], axis=0)
        h_prev = jnp.concatenate([h[-1:], h[:n_lat - 1], zero, h[n_lat:-1]], axis=0)
    else:
        a_next = jnp.concatenate([zero, a[:-1]], axis=0)
        h_prev = jnp.concatenate([h[1:], zero], axis=0)
    g = _linrec_call(a_next, dh, n_lat, "CD"[direction], "linrec_bwd")
    return g * h_prev, g


linrec.defvjp(_linrec_fwd, _linrec_bwd)


def _attn_fwd_call(q, k, v, scale):
    h, lq, dq = q.shape
    lk, dv = v.shape[1], v.shape[2]
    tq = _tile(lq, 256, SUBLANE)

    def body(q_ref, k_ref, v_ref, o_ref, lse_ref):
        s = lax.dot_general(q_ref[0].astype(BF16), k_ref[0].astype(BF16), (((1,), (1,)), ((), ())),
                            preferred_element_type=F32) * scale
        m = jnp.max(s, axis=-1, keepdims=True)
        p = jnp.exp(s - m)
        l = jnp.sum(p, axis=-1, keepdims=True)
        p = p / l
        o_ref[0] = jnp.dot(p.astype(BF16), v_ref[0].astype(BF16), preferred_element_type=F32)
        lse_ref[0] = m + jnp.log(l)

    return pl.pallas_call(
        body, name="attn_fwd", grid=(h, lq // tq),
        in_specs=[pl.BlockSpec((1, tq, dq), lambda a, i: (a, i, 0)), pl.BlockSpec((1, lk, dq), lambda a, i: (a, 0, 0)),
                  pl.BlockSpec((1, lk, dv), lambda a, i: (a, 0, 0))],
        out_specs=[pl.BlockSpec((1, tq, dv), lambda a, i: (a, i, 0)), pl.BlockSpec((1, tq, 1), lambda a, i: (a, i, 0))],
        out_shape=[jax.ShapeDtypeStruct((h, lq, dv), F32), jax.ShapeDtypeStruct((h, lq, 1), F32)],
        compiler_params=_cparams(("parallel", "parallel")),
    )(q, k, v)


def _attn_dq_call(q, k, v, do, lse, delta, scale):
    h, lq, dq = q.shape
    lk, dv = v.shape[1], v.shape[2]
    tq = _tile(lq, 256, SUBLANE)

    def body(q_ref, k_ref, v_ref, do_ref, lse_ref, dl_ref, dq_ref):
        kb = k_ref[0].astype(BF16)
        s = lax.dot_general(q_ref[0].astype(BF16), kb, (((1,), (1,)), ((), ())), preferred_element_type=F32) * scale
        p = jnp.exp(s - lse_ref[0])
        dp = lax.dot_general(do_ref[0].astype(BF16), v_ref[0].astype(BF16), (((1,), (1,)), ((), ())),
                             preferred_element_type=F32)
        ds = p * (dp - dl_ref[0])
        dq_ref[0] = jnp.dot(ds.astype(BF16), kb, preferred_element_type=F32) * scale

    return pl.pallas_call(
        body, name="attn_dq", grid=(h, lq // tq),
        in_specs=[pl.BlockSpec((1, tq, dq), lambda a, i: (a, i, 0)), pl.BlockSpec((1, lk, dq), lambda a, i: (a, 0, 0)),
                  pl.BlockSpec((1, lk, dv), lambda a, i: (a, 0, 0)), pl.BlockSpec((1, tq, dv), lambda a, i: (a, i, 0)),
                  pl.BlockSpec((1, tq, 1), lambda a, i: (a, i, 0)), pl.BlockSpec((1, tq, 1), lambda a, i: (a, i, 0))],
        out_specs=pl.BlockSpec((1, tq, dq), lambda a, i: (a, i, 0)),
        out_shape=jax.ShapeDtypeStruct((h, lq, dq), F32),
        compiler_params=_cparams(("parallel", "parallel")),
    )(q, k, v, do, lse, delta)


def _attn_dkv_call(q, k, v, do, lse_row, delta_row, scale):
    h, lq, dq = q.shape
    lk, dv = v.shape[1], v.shape[2]
    tk = _tile(lk, 256, SUBLANE)

    def body(q_ref, k_ref, v_ref, do_ref, lse_ref, dl_ref, dk_ref, dv_ref):
        qb = q_ref[0].astype(BF16)
        dob = do_ref[0].astype(BF16)
        st = lax.dot_general(k_ref[0].astype(BF16), qb, (((1,), (1,)), ((), ())), preferred_element_type=F32) * scale
        pt = jnp.exp(st - lse_ref[0])
        dpt = lax.dot_general(v_ref[0].astype(BF16), dob, (((1,), (1,)), ((), ())), preferred_element_type=F32)
        dst = pt * (dpt - dl_ref[0])
        dk_ref[0] = jnp.dot(dst.astype(BF16), qb, preferred_element_type=F32) * scale
        dv_ref[0] = jnp.dot(pt.astype(BF16), dob, preferred_element_type=F32)

    return pl.pallas_call(
        body, name="attn_dkv", grid=(h, lk // tk),
        in_specs=[pl.BlockSpec((1, lq, dq), lambda a, i: (a, 0, 0)), pl.BlockSpec((1, tk, dq), lambda a, i: (a, i, 0)),
                  pl.BlockSpec((1, tk, dv), lambda a, i: (a, i, 0)), pl.BlockSpec((1, lq, dv), lambda a, i: (a, 0, 0)),
                  pl.BlockSpec((1, 1, lq), lambda a, i: (a, 0, 0)), pl.BlockSpec((1, 1, lq), lambda a, i: (a, 0, 0))],
        out_specs=[pl.BlockSpec((1, tk, dq), lambda a, i: (a, i, 0)), pl.BlockSpec((1, tk, dv), lambda a, i: (a, i, 0))],
        out_shape=[jax.ShapeDtypeStruct((h, lk, dq), F32), jax.ShapeDtypeStruct((h, lk, dv), F32)],
        compiler_params=_cparams(("parallel", "parallel")),
    )(q, k, v, do, lse_row, delta_row)


@jax.custom_vjp
def attention(q, k, v):
    return _attn_fwd_call(q, k, v, q.shape[-1] ** -0.5)[0]


def _attention_fwd(q, k, v):
    o, lse = _attn_fwd_call(q, k, v, q.shape[-1] ** -0.5)
    return o, (q, k, v, o, lse)


def _attention_bwd(res, do):
    q, k, v, o, lse = res
    h, lq, _ = q.shape
    scale = q.shape[-1] ** -0.5
    delta = jnp.sum(do * o, axis=-1, keepdims=True)
    dq = _attn_dq_call(q, k, v, do, lse, delta, scale)
    dk, dv = _attn_dkv_call(q, k, v, do, lse.reshape(h, 1, lq), delta.reshape(h, 1, lq), scale)
    return dq, dk, dv


attention.defvjp(_attention_fwd, _attention_bwd)


def _na_start(r, rows):
    return jnp.clip(r - NA_WIN_H // 2, 0, rows - NA_WIN_H)


def _na_fwd_call(q, k, v, kc, vc, bias):
    n, width = q.shape
    heads, hd = width // NA_HEAD_DIM, NA_HEAD_DIM
    rows, nwin, nctx = n // GRID_W, NA_WIN_H * GRID_W, kc.shape[0]
    scale = hd ** -0.5
    nt = (((1,), (1,)), ((), ()))

    def body(q_ref, k_ref, v_ref, kc_ref, vc_ref, b_ref, o_ref, lse_ref):
        r = pl.program_id(1)
        start = pl.multiple_of(_na_start(r, rows) * GRID_W, GRID_W)
        qb = q_ref[...].astype(BF16)
        kw = k_ref[pl.ds(start, nwin), :].astype(BF16)
        vw = v_ref[pl.ds(start, nwin), :].astype(BF16)
        sw = lax.dot_general(qb, kw, nt, preferred_element_type=F32) * scale + b_ref[0, 0]
        sc = lax.dot_general(qb, kc_ref[...].astype(BF16), nt, preferred_element_type=F32) * scale
        m = jnp.maximum(jnp.max(sw, axis=-1, keepdims=True), jnp.max(sc, axis=-1, keepdims=True))
        pw = jnp.exp(sw - m)
        pc = jnp.exp(sc - m)
        l = jnp.sum(pw, axis=-1, keepdims=True) + jnp.sum(pc, axis=-1, keepdims=True)
        pw = pw / l
        pc = pc / l
        o_ref[...] = (jnp.dot(pw.astype(BF16), vw, preferred_element_type=F32)
                      + jnp.dot(pc.astype(BF16), vc_ref[...].astype(BF16), preferred_element_type=F32))
        lse_ref[0] = m + jnp.log(l)

    full = lambda rws: pl.BlockSpec((rws, hd), lambda a, r: (0, a))
    return pl.pallas_call(
        body, name="natten_fwd", grid=(heads, rows),
        in_specs=[pl.BlockSpec((GRID_W, hd), lambda a, r: (r, a)), full(n), full(n), full(nctx), full(nctx),
                  pl.BlockSpec((1, 1, GRID_W, nwin), lambda a, r: (a, r - _na_start(r, rows), 0, 0))],
        out_specs=[pl.BlockSpec((GRID_W, hd), lambda a, r: (r, a)), pl.BlockSpec((1, GRID_W, 1), lambda a, r: (a, r, 0))],
        out_shape=[jax.ShapeDtypeStruct((n, width), F32), jax.ShapeDtypeStruct((heads, n, 1), F32)],
        compiler_params=_cparams(("parallel", "arbitrary")),
    )(q, k, v, kc, vc, bias)


def _na_bwd_call(q, k, v, kc, vc, bias, bias_t, do, lse, delta, lse_row, delta_row):
    n, width = q.shape
    heads, hd = width // NA_HEAD_DIM, NA_HEAD_DIM
    rows, nwin, nctx = n // GRID_W, NA_WIN_H * GRID_W, kc.shape[0]
    scale = hd ** -0.5
    nt = (((1,), (1,)), ((), ()))

    def body(q_ref, k_ref, v_ref, kc_ref, vc_ref, b_ref, bt_ref, do_ref, lse_ref, dl_ref, lser_ref, dlr_ref,
             dq_ref, dk_ref, dv_ref, dkc_ref, dvc_ref, db_ref):
        r = pl.program_id(1)
        st = _na_start(r, rows)
        start = pl.multiple_of(st * GRID_W, GRID_W)

        @pl.when(r == 0)
        def _():
            dk_ref[...] = jnp.zeros_like(dk_ref)
            dv_ref[...] = jnp.zeros_like(dv_ref)
            dkc_ref[...] = jnp.zeros_like(dkc_ref)
            dvc_ref[...] = jnp.zeros_like(dvc_ref)

        @pl.when((r <= NA_WIN_H // 2) | (r > rows - NA_WIN_H // 2))
        def _():
            db_ref[...] = jnp.zeros_like(db_ref)

        qb = q_ref[...].astype(BF16)
        dob = do_ref[...].astype(BF16)
        kw = k_ref[pl.ds(start, nwin), :].astype(BF16)
        vw = v_ref[pl.ds(start, nwin), :].astype(BF16)
        kcb = kc_ref[...].astype(BF16)
        vcb = vc_ref[...].astype(BF16)
        lse_c, dl_c = lse_ref[0], dl_ref[0]
        pw = jnp.exp(lax.dot_general(qb, kw, nt, preferred_element_type=F32) * scale + b_ref[0, 0] - lse_c)
        pc = jnp.exp(lax.dot_general(qb, kcb, nt, preferred_element_type=F32) * scale - lse_c)
        dsw = pw * (lax.dot_general(dob, vw, nt, preferred_element_type=F32) - dl_c)
        dsc = pc * (lax.dot_general(dob, vcb, nt, preferred_element_type=F32) - dl_c)
        db_ref[0, 0] += dsw
        dq_ref[...] = (jnp.dot(dsw.astype(BF16), kw, preferred_element_type=F32)
                       + jnp.dot(dsc.astype(BF16), kcb, preferred_element_type=F32)) * scale
        lse_r, dl_r = lser_ref[0, 0], dlr_ref[0, 0]
        pwt = jnp.exp(lax.dot_general(kw, qb, nt, preferred_element_type=F32) * scale + bt_ref[0, 0] - lse_r)
        pct = jnp.exp(lax.dot_general(kcb, qb, nt, preferred_element_type=F32) * scale - lse_r)
        dswt = pwt * (lax.dot_general(vw, dob, nt, preferred_element_type=F32) - dl_r)
        dsct = pct * (lax.dot_general(vcb, dob, nt, preferred_element_type=F32) - dl_r)
        dk_ref[pl.ds(start, nwin), :] += jnp.dot(dswt.astype(BF16), qb, preferred_element_type=F32) * scale
        dv_ref[pl.ds(start, nwin), :] += jnp.dot(pwt.astype(BF16), dob, preferred_element_type=F32)
        dkc_ref[...] += jnp.dot(dsct.astype(BF16), qb, preferred_element_type=F32) * scale
        dvc_ref[...] += jnp.dot(pct.astype(BF16), dob, preferred_element_type=F32)

    full = lambda rws: pl.BlockSpec((rws, hd), lambda a, r: (0, a))
    tile = pl.BlockSpec((GRID_W, hd), lambda a, r: (r, a))
    pat = lambda r: r - _na_start(r, rows)
    col = pl.BlockSpec((1, GRID_W, 1), lambda a, r: (a, r, 0))
    rowv = pl.BlockSpec((1, 1, 1, GRID_W), lambda a, r: (a, r, 0, 0))
    return pl.pallas_call(
        body, name="natten_bwd", grid=(heads, rows),
        in_specs=[tile, full(n), full(n), full(nctx), full(nctx),
                  pl.BlockSpec((1, 1, GRID_W, nwin), lambda a, r: (a, pat(r), 0, 0)),
                  pl.BlockSpec((1, 1, nwin, GRID_W), lambda a, r: (a, pat(r), 0, 0)),
                  tile, col, col, rowv, rowv],
        out_specs=[tile, full(n), full(n), full(nctx), full(nctx),
                   pl.BlockSpec((1, 1, GRID_W, nwin), lambda a, r: (a, pat(r), 0, 0))],
        out_shape=[jax.ShapeDtypeStruct((n, width), F32), jax.ShapeDtypeStruct((n, width), F32),
                   jax.ShapeDtypeStruct((n, width), F32), jax.ShapeDtypeStruct((nctx, width), F32),
                   jax.ShapeDtypeStruct((nctx, width), F32), jax.ShapeDtypeStruct(bias.shape, F32)],
        compiler_params=_cparams(("parallel", "arbitrary")),
    )(q, k, v, kc, vc, bias, bias_t, do, lse, delta, lse_row, delta_row)


@jax.custom_vjp
def natten(q, k, v, kc, vc, bias):
    return _na_fwd_call(q, k, v, kc, vc, bias)[0]


def _natten_fwd(q, k, v, kc, vc, bias):
    o, lse = _na_fwd_call(q, k, v, kc, vc, bias)
    return o, (q, k, v, kc, vc, bias, o, lse)


def _natten_bwd(res, do):
    q, k, v, kc, vc, bias, o, lse = res
    n, width = q.shape
    heads, rows = width // NA_HEAD_DIM, n // GRID_W
    delta = jnp.sum((do * o).reshape(n, heads, NA_HEAD_DIM), axis=-1).T.reshape(heads, n, 1)
    return tuple(_na_bwd_call(q, k, v, kc, vc, bias, jnp.swapaxes(bias, 2, 3), do, lse, delta,
                              lse.reshape(heads, rows, 1, GRID_W), delta.reshape(heads, rows, 1, GRID_W)))


natten.defvjp(_natten_fwd, _natten_bwd)


def na_bias_patterns(rpb):
    heads = rpb.shape[0]
    pid = np.arange(NA_WIN_H)[:, None]
    j = np.arange(NA_WIN_H)[None, :]
    row_idx = j - pid + (NA_WIN_H - 1)
    row_hot = (row_idx[..., None] == np.arange(2 * NA_WIN_H - 1)).astype(np.float32)
    cidx = np.arange(GRID_W)
    c_start = np.clip(cidx - NA_WIN_W // 2, 0, GRID_W - NA_WIN_W)
    in_win = (cidx[None, :] >= c_start[:, None]) & (cidx[None, :] < c_start[:, None] + NA_WIN_W)
    col_idx = np.clip(cidx[None, :] - cidx[:, None], -(NA_WIN_W - 1), NA_WIN_W - 1) + (NA_WIN_W - 1)
    col_hot = (col_idx[..., None] == np.arange(2 * NA_WIN_W - 1)).astype(np.float32)
    hi = lax.Precision.HIGHEST
    tmp = jnp.einsum("hab,pja->hpjb", rpb, jnp.asarray(row_hot), precision=hi)
    bias = jnp.einsum("hpjb,qkb->hpqjk", tmp, jnp.asarray(col_hot), precision=hi)
    bias = jnp.where(jnp.asarray(in_win)[None, None, :, None, :], bias, NEG_INF)
    return bias.reshape(heads, NA_WIN_H, GRID_W, NA_WIN_H * GRID_W)


def adamw(w, g, m, v):
    shape = w.shape
    last = shape[-1]
    r = int(np.prod(shape[:-1]))
    w2, g2, m2, v2 = (t.reshape(1, r, last) for t in (w, g, m, v))
    tr = _tile(r, max(SUBLANE, (1 << 18) // last), SUBLANE)

    def body(w_ref, g_ref, m_ref, v_ref, d_ref, mo_ref, vo_ref):
        gv = g_ref[...]
        mn = ADAM_B1 * m_ref[...] + (1.0 - ADAM_B1) * gv
        vn = ADAM_B2 * v_ref[...] + (1.0 - ADAM_B2) * (gv * gv)
        m_hat = mn / (1.0 - ADAM_B1 ** ADAM_STEP)
        v_hat = vn / (1.0 - ADAM_B2 ** ADAM_STEP)
        d_ref[...] = -ADAM_LR * (m_hat / (jnp.sqrt(v_hat) + ADAM_EPS) + ADAM_WD * w_ref[...])
        mo_ref[...] = mn
        vo_ref[...] = vn

    spec = pl.BlockSpec((1, tr, last), lambda i: (0, i, 0))
    out = pl.pallas_call(
        body, name="adamw", grid=(r // tr,), in_specs=[spec] * 4, out_specs=[spec] * 3,
        out_shape=[jax.ShapeDtypeStruct((1, r, last), F32)] * 3, compiler_params=_cparams(("parallel",)),
    )(w2, g2, m2, v2)
    return tuple(t.reshape(shape) for t in out)


def _place():
    x, y, c = lax.axis_index("x"), lax.axis_index("y"), lax.axis_index("c")
    chips = [(1 - x, y), (x, 1 - y), (1 - x, 1 - y)]
    return x, y, c, chips


HBM_SPEC = pl.BlockSpec(memory_space=pltpu.HBM)


def all_gather_chips(xs, name):
    def body(x_ref, out_ref, send_sems, recv_sems):
        x, y, c, chips = _place()
        sibling = (x, y, 1 - c)

        def piece(chip, half):
            return out_ref.at[2 * chip[0] + chip[1], half]

        def copy(k, chip, half, to, src=None):
            dst = piece(chip, half)
            return pltpu.make_async_remote_copy(src_ref=dst if src is None else src, dst_ref=dst, send_sem=send_sems.at[k],
                                                recv_sem=recv_sems.at[k], device_id=to, device_id_type=MESH)

        first = [copy(k, (x, y), c, (*chip, c), src=x_ref.at[c]) for k, chip in enumerate(chips)]
        for cp in first:
            cp.start()
        passed = [copy(3 + k, chip, c, sibling) for k, chip in enumerate(chips)]
        for k, chip in enumerate(chips):
            copy(k, chip, c, sibling).wait_recv()
            passed[k].start()
        for k, chip in enumerate(chips):
            copy(3 + k, chip, 1 - c, sibling).wait_recv()
        for cp in first + passed:
            cp.wait_send()

    out = pl.pallas_call(
        body, name=name, in_specs=[HBM_SPEC], out_specs=HBM_SPEC,
        out_shape=jax.ShapeDtypeStruct((N_CHIPS,) + xs.shape, xs.dtype),
        scratch_shapes=[pltpu.SemaphoreType.DMA((6,)), pltpu.SemaphoreType.DMA((6,))],
        compiler_params=pltpu.CompilerParams(has_side_effects=True),
    )(xs)
    me = 2 * lax.axis_index("x") + lax.axis_index("y")
    return lax.dynamic_update_slice(out, xs[None], (me,) + (0,) * xs.ndim)


def _rs_to_sibling(g, name):
    _, _, r, b = g.shape

    def body(g_ref, out_ref, send_sems, recv_sems):
        x, y, c, _ = _place()
        cps = [pltpu.make_async_remote_copy(src_ref=g_ref.at[s, 1 - c], dst_ref=out_ref.at[s], send_sem=send_sems.at[s],
                                            recv_sem=recv_sems.at[s], device_id=(x, y, 1 - c), device_id_type=MESH)
               for s in range(N_CHIPS)]
        for cp in cps:
            cp.start()
        for cp in cps:
            cp.wait()

    return pl.pallas_call(
        body, name=name, in_specs=[HBM_SPEC], out_specs=HBM_SPEC, out_shape=jax.ShapeDtypeStruct((N_CHIPS, r, b), g.dtype),
        scratch_shapes=[pltpu.SemaphoreType.DMA((N_CHIPS,)), pltpu.SemaphoreType.DMA((N_CHIPS,))],
        compiler_params=pltpu.CompilerParams(has_side_effects=True),
    )(g)


def _rs_chip_sum(g, la, c_idx, name):
    _, _, r, b = g.shape
    tr = _tile(r, max(16, (1 << 18) // b), 16)

    def body(c_ref, g_ref, la_ref, o_ref):
        o_ref[...] = (g_ref[:, 0] + la_ref[...]).astype(BF16)

    return pl.pallas_call(
        body, name=name,
        grid_spec=pltpu.PrefetchScalarGridSpec(
            num_scalar_prefetch=1, grid=(N_CHIPS, r // tr),
            in_specs=[pl.BlockSpec((1, 1, tr, b), lambda s, i, c_ref: (s, c_ref[0], i, 0)),
                      pl.BlockSpec((1, tr, b), lambda s, i, c_ref: (s, i, 0))],
            out_specs=pl.BlockSpec((1, tr, b), lambda s, i, c_ref: (s, i, 0))),
        out_shape=jax.ShapeDtypeStruct((N_CHIPS, r, b), BF16), compiler_params=_cparams(("parallel", "parallel")),
    )(c_idx, g, la)


def _rs_to_owners(hb, name):
    _, r, b = hb.shape

    def body(h_ref, out_ref, send_sems, recv_sems):
        x, y, c, chips = _place()
        cps = [pltpu.make_async_remote_copy(src_ref=h_ref.at[2 * chip[0] + chip[1]], dst_ref=out_ref.at[k],
                                            send_sem=send_sems.at[k], recv_sem=recv_sems.at[k], device_id=(*chip, c),
                                            device_id_type=MESH) for k, chip in enumerate(chips)]
        for cp in cps:
            cp.start()
        for cp in cps:
            cp.wait()

    return pl.pallas_call(
        body, name=name, in_specs=[HBM_SPEC], out_specs=HBM_SPEC, out_shape=jax.ShapeDtypeStruct((3, r, b), hb.dtype),
        scratch_shapes=[pltpu.SemaphoreType.DMA((3,)), pltpu.SemaphoreType.DMA((3,))],
        compiler_params=pltpu.CompilerParams(has_side_effects=True),
    )(hb)


def _rs_final_sum(g, la, lb, sc_idx, name):
    _, _, r, b = g.shape
    tr = _tile(r, max(16, (1 << 18) // b), 16)

    def body(i_ref, g_ref, la_ref, lb_ref, o_ref):
        o_ref[...] = (g_ref[0, 0] + la_ref[0]) + lb_ref[0].astype(F32) + lb_ref[1].astype(F32) + lb_ref[2].astype(F32)

    return pl.pallas_call(
        body, name=name,
        grid_spec=pltpu.PrefetchScalarGridSpec(
            num_scalar_prefetch=1, grid=(r // tr,),
            in_specs=[pl.BlockSpec((1, 1, tr, b), lambda i, i_ref: (i_ref[0], i_ref[1], i, 0)),
                      pl.BlockSpec((1, tr, b), lambda i, i_ref: (i_ref[0], i, 0)),
                      pl.BlockSpec((3, tr, b), lambda i, i_ref: (0, i, 0))],
            out_specs=pl.BlockSpec((tr, b), lambda i, i_ref: (i, 0))),
        out_shape=jax.ShapeDtypeStruct((r, b), F32), compiler_params=_cparams(("parallel",)),
    )(sc_idx, g, la, lb)


def _rs_join_halves(f, name):
    r, b = f.shape

    def body(f_ref, out_ref, send_sem, recv_sem):
        x, y, c, _ = _place()
        cp = pltpu.make_async_remote_copy(src_ref=f_ref, dst_ref=out_ref.at[c], send_sem=send_sem, recv_sem=recv_sem,
                                          device_id=(x, y, 1 - c), device_id_type=MESH)
        cp.start()
        pltpu.make_async_remote_copy(src_ref=f_ref, dst_ref=out_ref.at[1 - c], send_sem=send_sem, recv_sem=recv_sem,
                                     device_id=(x, y, 1 - c), device_id_type=MESH).wait_recv()
        cp.wait_send()

    out = pl.pallas_call(
        body, name=name, in_specs=[HBM_SPEC], out_specs=HBM_SPEC, out_shape=jax.ShapeDtypeStruct((2, r, b), f.dtype),
        scratch_shapes=[pltpu.SemaphoreType.DMA, pltpu.SemaphoreType.DMA],
        compiler_params=pltpu.CompilerParams(has_side_effects=True),
    )(f)
    return lax.dynamic_update_slice(out, f[None], (lax.axis_index("c"), 0, 0))


def reduce_scatter_devices(g, tag):
    c = lax.axis_index("c")
    me = 2 * lax.axis_index("x") + lax.axis_index("y")
    la = _rs_to_sibling(g, "rs_sibling_" + tag)
    hb = _rs_chip_sum(g, la, jnp.stack([c]).astype(jnp.int32), "rs_chipsum_" + tag)
    lb = _rs_to_owners(hb, "rs_owners_" + tag)
    f = _rs_final_sum(g, la, lb, jnp.stack([me, c]).astype(jnp.int32), "rs_final_" + tag)
    return _rs_join_halves(f, "rs_join_" + tag)


W_IN_SIZES = ("na", "lru_x", "lru_g", "cq", "ckv", "kr", "gates")


def _w_in_layout(d_model, na_width, lru_width, q_rank, kv_rank):
    sizes = dict(na=3 * na_width, lru_x=lru_width, lru_g=lru_width, cq=q_rank, ckv=kv_rank, kr=MLA_ROPE_DIM,
                 gates=N_BRANCH * d_model)
    head = sizes["na"] + 2 * lru_width + q_rank + kv_rank
    total = head + MLA_ROPE_DIM + sizes["gates"]
    padded = -(-total // LANE) * LANE
    return sizes, head, total, padded


def _permute_w_in(w, head, total, padded):
    parts = [w[:, :head], w[:, head + MLA_ROPE_DIM:total], w[:, head:head + MLA_ROPE_DIM]]
    if padded > total:
        parts.append(jnp.zeros((w.shape[0], padded - total), w.dtype))
    return jnp.concatenate(parts, axis=1)


def _unpermute_w_in(g, head, total):
    n_gates = total - head - MLA_ROPE_DIM
    return jnp.concatenate([g[:, :head], g[:, head + n_gates:head + n_gates + MLA_ROPE_DIM], g[:, head:head + n_gates]], axis=1)


@functools.partial(jax.custom_vjp, nondiff_argnums=(1,))
def split_cols(z, sizes):
    out, o = [], 0
    for sz in sizes:
        out.append(z[:, o:o + sz])
        o += sz
    return tuple(out)


def _split_cols_fwd(z, sizes):
    return split_cols(z, sizes), None


def _split_cols_bwd(sizes, res, cts):
    return (jnp.concatenate(cts, axis=1),)


split_cols.defvjp(_split_cols_fwd, _split_cols_bwd)


def _dwconv(x, w, b):
    width = w.shape[0]
    n = x.shape[0]
    left = width // 2
    xp = jnp.pad(x, ((left, width - 1 - left), (0, 0)))
    return sum(xp[i:i + n] * w[i] for i in range(width)) + b


def _dwconv_seg(x, w, b, n_lat):
    return jnp.concatenate([_dwconv(x[:n_lat], w, b), _dwconv(x[n_lat:], w, b)], axis=0)


def _rope(x, cos, sin):
    half = x.shape[-1] // 2
    x1, x2 = x[..., :half], x[..., half:]
    return jnp.concatenate([x1 * cos - x2 * sin, x1 * sin + x2 * cos], axis=-1)


def _rope_tables(n_lat, n_ctx):
    t = jnp.arange(n_lat, dtype=jnp.int32)
    row = (t // GRID_W).astype(F32)
    col = (t % GRID_W).astype(F32)
    n_freq = MLA_ROPE_DIM // 4
    inv_freq = ROPE_THETA ** (-jnp.arange(n_freq, dtype=F32) / n_freq)
    ang = jnp.concatenate([row[:, None] * inv_freq, col[:, None] * inv_freq], axis=-1)
    ones = jnp.ones((n_ctx, MLA_ROPE_DIM // 2), F32)
    return jnp.concatenate([jnp.cos(ang), ones], axis=0), jnp.concatenate([jnp.sin(ang), 0.0 * ones], axis=0)


def _heads_first(t, heads):
    n = t.shape[0]
    return t.reshape(n, heads, -1).transpose(1, 0, 2)


def _heads_last(t):
    return t.transpose(1, 0, 2).reshape(t.shape[1], -1)


def _lru_coeffs(u, w_a, b_a, w_x, b_x, lam):
    r = jax.nn.sigmoid(bd_dot(u, w_a) + b_a)
    i = jax.nn.sigmoid(bd_dot(u, w_x) + b_x)
    log_a = -LRU_C * r * jax.nn.softplus(-lam)
    return jnp.exp(log_a), jnp.sqrt(-jnp.expm1(2.0 * log_a)) * (i * u)


def _forward_loss(big, prox, small, x, ctx, silu_c, target):
    n_lat, d_model = x.shape
    n_ctx = ctx.shape[0]
    na_width = NA_HEADS * NA_HEAD_DIM
    lru_width = small["lru_conv_b"].shape[1]
    q_rank, kv_rank = small["mla_q_norm"].shape[1], small["mla_kv_norm"].shape[1]
    _, head, total, padded = _w_in_layout(d_model, na_width, lru_width, q_rank, kv_rank)
    cos, sin = _rope_tables(n_lat, n_ctx)
    is_ctx = (jnp.arange(n_lat + n_ctx) >= n_lat)[:, None]

    def by_row(v2):
        return jnp.where(is_ctx, v2[1][None, :], v2[0][None, :])

    s_rows = jnp.concatenate([silu_c, jax.nn.silu(small["c_ctx"])[None, :],
                              jnp.zeros((MOD_ROWS - 2, d_model), F32)], axis=0)
    xs = jnp.concatenate([x, ctx], axis=0)
    for l in range(DEPTH):
        mod = (pdot(s_rows, big["w_mod"][l], prox["w_mod"][l]) + small["b_mod"][l])[:2]
        sh1, sc1, g1, sh2, sc2, g2 = split_cols(mod, (d_model,) * N_MOD)
        h = norm_mod(xs, small["norm_mix"][l], sh1, sc1, n_lat)
        z = pdot(h, big["w_in"][l], prox["w_in"][l])
        names = ("q_a", "k_a", "v_a", "lru_x", "lru_g", "cq", "ckv", "gate_a", "gate_b", "gate_c", "kr", "pad")
        widths = (na_width,) * 3 + (lru_width,) * 2 + (q_rank, kv_rank) + (d_model,) * 3 + (MLA_ROPE_DIM, padded - total)
        cols = dict(zip(names, split_cols(z, widths)))
        q_a, k_a, v_a = cols["q_a"], cols["k_a"], cols["v_a"]
        out_a_lat = natten(q_a[:n_lat], k_a[:n_lat], v_a[:n_lat], k_a[n_lat:], v_a[n_lat:], na_bias_patterns(small["na_rpb"][l]))
        out_a_ctx = _heads_last(attention(*(_heads_first(t[n_lat:], NA_HEADS) for t in (q_a, k_a, v_a))))
        out_a = jnp.concatenate([out_a_lat, out_a_ctx], axis=0)
        u = _dwconv_seg(cols["lru_x"], small["lru_conv_w"][l], small["lru_conv_b"][l], n_lat)
        hs = [linrec(*_lru_coeffs(u, small["lru_w_a"][l, d], small["lru_b_a"][l, d], small["lru_w_x"][l, d],
                                  small["lru_b_x"][l, d], small["lru_lam"][l, d]), n_lat, d) for d in range(2)]
        y_b = hs[0] + hs[1]
        out_b = jax.nn.gelu(cols["lru_g"]) * y_b
        q_m = pdot(rmsnorm_p(cols["cq"], small["mla_q_norm"][l]), big["mla_w_q_up"][l], prox["mla_w_q_up"][l])
        q_m = q_m.reshape(-1, MLA_HEADS, MLA_NOPE_DIM + MLA_ROPE_DIM)
        q_m = jnp.concatenate([q_m[..., :MLA_NOPE_DIM], _rope(q_m[..., MLA_NOPE_DIM:], cos[:, None, :], sin[:, None, :])], axis=-1)
        kv = pdot(rmsnorm_p(cols["ckv"], small["mla_kv_norm"][l]), big["mla_w_kv_up"][l], prox["mla_w_kv_up"][l])
        kv = kv.reshape(-1, MLA_HEADS, MLA_NOPE_DIM + MLA_V_DIM)
        k_rope = jnp.broadcast_to(_rope(cols["kr"], cos, sin)[:, None, :], (n_lat + n_ctx, MLA_HEADS, MLA_ROPE_DIM))
        k_m = jnp.concatenate([kv[..., :MLA_NOPE_DIM], k_rope], axis=-1).transpose(1, 0, 2)
        v_m = kv[..., MLA_NOPE_DIM:].transpose(1, 0, 2)
        q_m = q_m.transpose(1, 0, 2)
        out_c = jnp.concatenate([_heads_last(attention(q_m[:, :n_lat], k_m, v_m)),
                                 _heads_last(attention(q_m[:, n_lat:], k_m[:, n_lat:], v_m[:, n_lat:]))], axis=0)
        y = sum(jax.nn.sigmoid(cols[gate]) * pdot(br, big["w_branch"][l][i], prox["w_branch"][l][i])
                for i, (gate, br) in enumerate(zip(("gate_a", "gate_b", "gate_c"), (out_a, out_b, out_c))))
        xs = xs + by_row(g1) * pdot(y, big["w_out"][l], prox["w_out"][l])
        h2 = norm_mod(xs, small["norm_ffn"][l], sh2, sc2, n_lat)
        up = _dwconv_seg(pdot(h2, big["ffn_w_up"][l], prox["ffn_w_up"][l]), small["ffn_conv_w"][l], small["ffn_conv_b"][l], n_lat)
        val, gate = split_cols(up, (up.shape[1] // 2,) * 2)
        xs = xs + by_row(g2) * pdot(jax.nn.silu(gate) * val, big["ffn_w_down"][l], prox["ffn_w_down"][l])
    y_out = rmsnorm_p(xs[:n_lat], small["norm_final"])
    return 0.5 * jnp.sum(jnp.mean(jnp.square(y_out - target), axis=-1))


WEIGHTS = ['c_ctx', 'w_mod', 'b_mod', 'norm_mix', 'norm_ffn', 'w_in', 'na_rpb', 'lru_conv_w', 'lru_conv_b', 'lru_w_a',
           'lru_b_a', 'lru_w_x', 'lru_b_x', 'lru_lam', 'mla_q_norm', 'mla_kv_norm', 'mla_w_q_up', 'mla_w_kv_up', 'w_branch',
           'w_out', 'ffn_w_up', 'ffn_conv_w', 'ffn_conv_b', 'ffn_w_down', 'norm_final']
BIG = {'w_mod': 2, 'w_in': 2, 'mla_w_q_up': 2, 'mla_w_kv_up': 2, 'w_branch': 3, 'w_out': 1, 'ffn_w_up': 2, 'ffn_w_down': 1}
SMALL_SHARDED = {'lru_conv_w': 2, 'lru_b_a': 2, 'lru_b_x': 2, 'lru_lam': 2, 'ffn_conv_w': 2}
PACK_LANES = 128
PACK_ROWS = 16


def _gathered_to_full(g, ax):
    t = jnp.moveaxis(g, 0, ax)
    return t.reshape(t.shape[:ax] + (t.shape[ax] * t.shape[ax + 1],) + t.shape[ax + 2:])


def _full_to_stacked(g, ax):
    n = g.shape[ax]
    t = g.reshape(g.shape[:ax] + (N_CHIPS, n // N_CHIPS) + g.shape[ax + 1:])
    return jnp.moveaxis(t, ax, 0)


def _canon(shape):
    return (2, int(np.prod(shape[:-1])) // 2, shape[-1])


def _pack(parts, multiple):
    flat = jnp.concatenate([p.reshape(-1) for p in parts])
    n = flat.shape[0]
    padded = -(-n // multiple) * multiple
    return jnp.pad(flat, (0, padded - n))


def _unpack(flat, shapes):
    out, o = [], 0
    for s in shapes:
        n = int(np.prod(s))
        out.append(flat[o:o + n].reshape(s))
        o += n
    return out


def kernel(x, c, ctx, c_ctx, w_mod, b_mod, norm_mix, norm_ffn, w_in, na_rpb, lru_conv_w, lru_conv_b, lru_w_a, lru_b_a, lru_w_x, lru_b_x, lru_lam, mla_q_norm, mla_kv_norm, mla_w_q_up, mla_w_kv_up, w_branch, w_out, ffn_w_up, ffn_conv_w, ffn_conv_b, ffn_w_down, norm_final, loss_target, m_c_ctx, m_w_mod, m_b_mod, m_norm_mix, m_norm_ffn, m_w_in, m_na_rpb, m_lru_conv_w, m_lru_conv_b, m_lru_w_a, m_lru_b_a, m_lru_w_x, m_lru_b_x, m_lru_lam, m_mla_q_norm, m_mla_kv_norm, m_mla_w_q_up, m_mla_w_kv_up, m_w_branch, m_w_out, m_ffn_w_up, m_ffn_conv_w, m_ffn_conv_b, m_ffn_w_down, m_norm_final, v_c_ctx, v_w_mod, v_b_mod, v_norm_mix, v_norm_ffn, v_w_in, v_na_rpb, v_lru_conv_w, v_lru_conv_b, v_lru_w_a, v_lru_b_a, v_lru_w_x, v_lru_b_x, v_lru_lam, v_mla_q_norm, v_mla_kv_norm, v_mla_w_q_up, v_mla_w_kv_up, v_w_branch, v_w_out, v_ffn_w_up, v_ffn_conv_w, v_ffn_conv_b, v_ffn_w_down, v_norm_final):
    args = dict(locals())
    w = {n: args[n] for n in WEIGHTS}
    m = {n: args["m_" + n] for n in WEIGHTS}
    v = {n: args["v_" + n] for n in WEIGHTS}
    me = 2 * lax.axis_index("x") + lax.axis_index("y")

    full = {}
    for name, ax in BIG.items():
        shard = w[name]
        g = all_gather_chips(shard.astype(BF16).reshape(_canon(shard.shape)), "ag_" + name)
        full[name] = _gathered_to_full(g.reshape((N_CHIPS,) + shard.shape), ax)
    packed = _pack([w[n] for n in SMALL_SHARDED], 2 * PACK_ROWS * PACK_LANES).reshape(2, -1, PACK_LANES)
    g = all_gather_chips(packed, "ag_small").reshape(N_CHIPS, -1)
    for name, t in zip(SMALL_SHARDED, zip(*[_unpack(g[s], [w[n].shape for n in SMALL_SHARDED]) for s in range(N_CHIPS)])):
        full[name] = _gathered_to_full(jnp.stack(t), SMALL_SHARDED[name])
    small = {n: full.get(n, w[n]) for n in WEIGHTS if n not in BIG}

    d_model = x.shape[-1]
    _, head, total, padded = _w_in_layout(d_model, NA_HEADS * NA_HEAD_DIM, lru_conv_b.shape[1], mla_q_norm.shape[1], mla_kv_norm.shape[1])
    big = {n: [full[n][l] for l in range(DEPTH)] for n in BIG}
    big["w_in"] = [_permute_w_in(t, head, total, padded) for t in big["w_in"]]
    big["w_branch"] = [[t[i] for i in range(N_BRANCH)] for t in big["w_branch"]]
    prox = jax.tree.map(lambda t: jnp.zeros(t.shape, F32), big)

    loss, (g_prox, g_small, g_x) = jax.value_and_grad(_forward_loss, argnums=(1, 2, 3))(
        big, prox, small, x[0], ctx[0], jax.nn.silu(c), loss_target[0])
    loss = lax.psum(loss, ("x", "y", "c"))

    grads = {}
    g_prox["w_in"] = [_unpermute_w_in(t, head, total) for t in g_prox["w_in"]]
    g_prox["w_branch"] = [jnp.stack(t) for t in g_prox["w_branch"]]
    for name, ax in BIG.items():
        stacked = _full_to_stacked(jnp.stack(g_prox[name]), ax)
        red = reduce_scatter_devices(stacked.reshape((N_CHIPS,) + _canon(w[name].shape)), name)
        grads[name] = red.reshape(w[name].shape)
    small_names = [n for n in WEIGHTS if n not in BIG]
    packed = _pack([g_small[n] for n in small_names], N_CHIPS * 2 * PACK_ROWS * PACK_LANES).reshape(N_CHIPS, 2, -1, PACK_LANES)
    red = all_gather_chips(reduce_scatter_devices(packed, "small"), "ag_small_grads").reshape(-1)
    for name, t in zip(small_names, _unpack(red, [g_small[n].shape for n in small_names])):
        if name in SMALL_SHARDED:
            ax = SMALL_SHARDED[name]
            t = lax.dynamic_slice_in_dim(t, me * w[name].shape[ax], w[name].shape[ax], axis=ax)
        grads[name] = t

    upd = {n: adamw(w[n], grads[n], m[n], v[n]) for n in WEIGHTS}
    return (loss, g_x[None], *[grads[n] for n in WEIGHTS], *[upd[n][0] for n in WEIGHTS],
            *[upd[n][1] for n in WEIGHTS], *[upd[n][2] for n in WEIGHTS])
```

```python
import functools
import math

import numpy as np
import jax
import jax.numpy as jnp
from jax import lax
from jax.experimental import pallas as pl
from jax.experimental.pallas import tpu as pltpu

F32 = jnp.float32
BF16 = jnp.bfloat16
MESH = pl.DeviceIdType.MESH

DEPTH = 4
GRID_W = 64
NORM_EPS = 1e-6
NEG_INF = -1e30
N_MOD = 6
NA_HEADS = 8
NA_HEAD_DIM = 128
NA_WIN_H = 8
NA_WIN_W = 16
LRU_BLOCKS = 8
LRU_CONV_W = 4
LRU_C = 8.0
MLA_HEADS = 8
MLA_NOPE_DIM = 128
MLA_ROPE_DIM = 64
MLA_V_DIM = 128
ROPE_THETA = 10000.0
N_BRANCH = 3
FFN_CONV_W = 3
ADAM_LR = 0.001
ADAM_B1 = 0.9
ADAM_B2 = 0.999
ADAM_EPS = 1e-08
ADAM_WD = 0.01
ADAM_STEP = 10

N_CHIPS = 4
LANE = 128
SUBLANE = 8
VMEM_LIMIT = 48 * 1024 * 1024
MOD_ROWS = 128


def _cparams(sem=None, **kw):
    if sem is not None:
        kw["dimension_semantics"] = sem
    return pltpu.CompilerParams(vmem_limit_bytes=VMEM_LIMIT, **kw)


def _tile(d, cap, align):
    best = None
    t = align
    while t <= min(d, cap):
        if d % t == 0:
            best = t
        t += align
    return d if best is None else best


FULL_K_MAX = 2304


def _mm_rows(a, b, *, tb, emit, name):
    m, k = a.shape
    n = b.shape[0] if tb else b.shape[1]
    tm = _tile(m, 544, 16)
    tn = _tile(n, 640, LANE)
    dims = (((1,), (1 if tb else 0,)), ((), ()))

    def body(a_ref, b_ref, o_ref, *rest):
        a_sc = rest[-1]

        @pl.when(pl.program_id(1) == 0)
        def _():
            a_sc[...] = a_ref[...].astype(BF16)
            if emit:
                rest[0][...] = a_sc[...]

        o_ref[...] = lax.dot_general(a_sc[...], b_ref[...].astype(BF16), dims, preferred_element_type=F32)

    b_spec = pl.BlockSpec((tn, k), lambda i, j: (j, 0)) if tb else pl.BlockSpec((k, tn), lambda i, j: (0, j))
    out_specs = [pl.BlockSpec((tm, tn), lambda i, j: (i, j))]
    out_shape = [jax.ShapeDtypeStruct((m, n), F32)]
    if emit:
        out_specs.append(pl.BlockSpec((tm, k), lambda i, j: (i, 0)))
        out_shape.append(jax.ShapeDtypeStruct((m, k), BF16))
    out = pl.pallas_call(
        body, name=name, grid=(m // tm, n // tn),
        in_specs=[pl.BlockSpec((tm, k), lambda i, j: (i, 0)), b_spec], out_specs=out_specs, out_shape=out_shape,
        scratch_shapes=[pltpu.VMEM((tm, k), BF16)], compiler_params=_cparams(("parallel", "arbitrary")),
    )(a, b)
    return out if emit else out[0]


def _mm_acc(a, b, *, ta=False, tb=False, name):
    m, k = (a.shape[1], a.shape[0]) if ta else a.shape
    n = b.shape[0] if tb else b.shape[1]
    tm = _tile(m, 2048 if ta else 1088, 16)
    tn = _tile(n, 640 if ta else 2048, LANE)
    tk = _tile(k, 1088, 16) if ta else _tile(k, 640, LANE)
    dims = (((0 if ta else 1,), (1 if tb else 0,)), ((), ()))

    def body(a_ref, b_ref, o_ref):
        @pl.when(pl.program_id(2) == 0)
        def _():
            o_ref[...] = jnp.zeros_like(o_ref)

        o_ref[...] += lax.dot_general(a_ref[...].astype(BF16), b_ref[...].astype(BF16), dims, preferred_element_type=F32)

    a_spec = pl.BlockSpec((tk, tm), lambda i, j, kk: (kk, i)) if ta else pl.BlockSpec((tm, tk), lambda i, j, kk: (i, kk))
    b_spec = pl.BlockSpec((tn, tk), lambda i, j, kk: (j, kk)) if tb else pl.BlockSpec((tk, tn), lambda i, j, kk: (kk, j))
    return pl.pallas_call(
        body, name=name, grid=(m // tm, n // tn, k // tk),
        in_specs=[a_spec, b_spec], out_specs=pl.BlockSpec((tm, tn), lambda i, j, kk: (i, j)),
        out_shape=jax.ShapeDtypeStruct((m, n), F32),
        compiler_params=_cparams(("parallel", "parallel", "arbitrary")),
    )(a, b)


def _pdot_fwd(a, w, wp):
    if a.shape[1] <= FULL_K_MAX:
        out, a_bf = _mm_rows(a, w, tb=False, emit=True, name="mm_fwd")
        return out, (a_bf, w)
    return _mm_acc(a, w, name="mm_fwd_acc"), (a, w)


@jax.custom_vjp
def pdot(a, w, wp):
    return _pdot_fwd(a, w, wp)[0]


def _pdot_bwd(res, g):
    a, w = res
    if w.shape[1] <= FULL_K_MAX:
        da = _mm_rows(g, w, tb=True, emit=False, name="mm_dgrad")
    else:
        da = _mm_acc(g, w, tb=True, name="mm_dgrad_acc")
    return da, jnp.zeros_like(w), _mm_acc(a, g, ta=True, name="mm_wgrad")


pdot.defvjp(_pdot_fwd, _pdot_bwd)


def _bd_mm(x, w, name):
    t, c = x.shape
    nb, bd, _ = w.shape
    tm = _tile(t, 1088, 16)

    def body(x_ref, w_ref, o_ref):
        o_ref[...] = jnp.dot(x_ref[...].astype(BF16), w_ref[0].astype(BF16), preferred_element_type=F32)

    return pl.pallas_call(
        body, name=name, grid=(t // tm, nb),
        in_specs=[pl.BlockSpec((tm, bd), lambda i, k: (i, k)), pl.BlockSpec((1, bd, bd), lambda i, k: (k, 0, 0))],
        out_specs=pl.BlockSpec((tm, bd), lambda i, k: (i, k)),
        out_shape=jax.ShapeDtypeStruct((t, c), F32),
        compiler_params=_cparams(("parallel", "parallel")),
    )(x, w)


def _bd_wgrad(x, g, nb, name):
    t, c = x.shape
    bd = c // nb
    tk = _tile(t, 1088, 16)
    nt = t // tk

    def body(x_ref, g_ref, o_ref):
        @pl.when(pl.program_id(1) == 0)
        def _():
            o_ref[...] = jnp.zeros_like(o_ref)

        xt = x_ref[...].T.astype(BF16)
        o_ref[0] += jnp.dot(xt, g_ref[...].astype(BF16), preferred_element_type=F32)

    return pl.pallas_call(
        body, name=name, grid=(nb, nt),
        in_specs=[pl.BlockSpec((tk, bd), lambda k, i: (i, k)), pl.BlockSpec((tk, bd), lambda k, i: (i, k))],
        out_specs=pl.BlockSpec((1, bd, bd), lambda k, i: (k, 0, 0)),
        out_shape=jax.ShapeDtypeStruct((nb, bd, bd), F32),
        compiler_params=_cparams(("parallel", "arbitrary")),
    )(x, g)


@jax.custom_vjp
def bd_dot(x, w):
    return _bd_mm(x, w, "bd_fwd")


def _bd_fwd(x, w):
    return _bd_mm(x, w, "bd_fwd"), (x, w)


def _bd_bwd(res, g):
    x, w = res
    return _bd_mm(g, jnp.swapaxes(w, 1, 2), "bd_dgrad"), _bd_wgrad(x, g, w.shape[0], "bd_wgrad")


bd_dot.defvjp(_bd_fwd, _bd_bwd)


def _nm_tiles(t, seg_rows):
    tm = _tile(math.gcd(t, seg_rows), 256, SUBLANE)
    return tm, seg_rows // tm


def _nm_fwd_call(x, g, shift, scale, seg_rows):
    t, d = x.shape
    tm, seg_blocks = _nm_tiles(t, seg_rows)
    nseg = shift.shape[0]

    def seg(i):
        return jnp.minimum(i // seg_blocks, nseg - 1)

    def body(x_ref, g_ref, sh_ref, sc_ref, o_ref):
        xv = x_ref[...]
        y = xv * lax.rsqrt(jnp.mean(xv * xv, axis=-1, keepdims=True) + NORM_EPS)
        o_ref[...] = (y * g_ref[...]) * (1.0 + sc_ref[0]) + sh_ref[0]

    return pl.pallas_call(
        body, name="norm_mod_fwd", grid=(t // tm,),
        in_specs=[pl.BlockSpec((tm, d), lambda i: (i, 0)), pl.BlockSpec((1, d), lambda i: (0, 0)),
                  pl.BlockSpec((1, 1, d), lambda i: (seg(i), 0, 0)), pl.BlockSpec((1, 1, d), lambda i: (seg(i), 0, 0))],
        out_specs=pl.BlockSpec((tm, d), lambda i: (i, 0)),
        out_shape=jax.ShapeDtypeStruct((t, d), F32),
        compiler_params=_cparams(("parallel",)),
    )(x, g.reshape(1, d), shift.reshape(nseg, 1, d), scale.reshape(nseg, 1, d))


def _nm_bwd_call(x, g, scale, dy, seg_rows):
    t, d = x.shape
    tm, seg_blocks = _nm_tiles(t, seg_rows)
    nseg = scale.shape[0]

    def seg(i):
        return jnp.minimum(i // seg_blocks, nseg - 1)

    def body(x_ref, g_ref, sc_ref, dy_ref, dx_ref, dg_ref, dsh_ref, dsc_ref):
        i = pl.program_id(0)

        @pl.when(i == 0)
        def _():
            dg_ref[...] = jnp.zeros_like(dg_ref)

        @pl.when((i == 0) | (i == seg_blocks))
        def _():
            dsh_ref[...] = jnp.zeros_like(dsh_ref)
            dsc_ref[...] = jnp.zeros_like(dsc_ref)

        xv = x_ref[...]
        dyv = dy_ref[...]
        gv = g_ref[...]
        rstd = lax.rsqrt(jnp.mean(xv * xv, axis=-1, keepdims=True) + NORM_EPS)
        xhat = xv * rstd
        dsh_ref[0] += jnp.sum(dyv, axis=0, keepdims=True)
        dsc_ref[0] += jnp.sum(dyv * (xhat * gv), axis=0, keepdims=True)
        dn = dyv * (1.0 + sc_ref[0])
        dg_ref[...] += jnp.sum(dn * xhat, axis=0, keepdims=True)
        dxh = dn * gv
        dx_ref[...] = rstd * (dxh - xhat * jnp.mean(dxh * xhat, axis=-1, keepdims=True))

    return pl.pallas_call(
        body, name="norm_mod_bwd", grid=(t // tm,),
        in_specs=[pl.BlockSpec((tm, d), lambda i: (i, 0)), pl.BlockSpec((1, d), lambda i: (0, 0)),
                  pl.BlockSpec((1, 1, d), lambda i: (seg(i), 0, 0)), pl.BlockSpec((tm, d), lambda i: (i, 0))],
        out_specs=[pl.BlockSpec((tm, d), lambda i: (i, 0)), pl.BlockSpec((1, d), lambda i: (0, 0)),
                   pl.BlockSpec((1, 1, d), lambda i: (seg(i), 0, 0)), pl.BlockSpec((1, 1, d), lambda i: (seg(i), 0, 0))],
        out_shape=[jax.ShapeDtypeStruct((t, d), F32), jax.ShapeDtypeStruct((1, d), F32),
                   jax.ShapeDtypeStruct((nseg, 1, d), F32), jax.ShapeDtypeStruct((nseg, 1, d), F32)],
        compiler_params=_cparams(("arbitrary",)),
    )(x, g.reshape(1, d), scale.reshape(nseg, 1, d), dy)


@functools.partial(jax.custom_vjp, nondiff_argnums=(4,))
def norm_mod(x, g, shift, scale, seg_rows):
    return _nm_fwd_call(x, g, shift, scale, seg_rows)


def _norm_mod_fwd(x, g, shift, scale, seg_rows):
    return _nm_fwd_call(x, g, shift, scale, seg_rows), (x, g, scale)


def _norm_mod_bwd(seg_rows, res, dy):
    x, g, scale = res
    dx, dg, dsh, dsc = _nm_bwd_call(x, g, scale, dy, seg_rows)
    return dx, dg.reshape(g.shape), dsh.reshape(scale.shape), dsc.reshape(scale.shape)


norm_mod.defvjp(_norm_mod_fwd, _norm_mod_bwd)


def rmsnorm_p(x, g):
    z = jnp.zeros((1, x.shape[1]), F32)
    return norm_mod(x, g, z, z, x.shape[0])


def _linrec_call(a, b, n_lat, mode, name):
    t, c = a.shape
    tb = _tile(math.gcd(t, n_lat), 512, SUBLANE)
    nb, nl = t // tb, n_lat // tb
    reverse = mode in ("B", "C")
    tiles = tb // SUBLANE

    def block(i):
        if mode == "A":
            return (i + nl) % nb
        if mode == "B":
            return nb - 1 - i
        if mode == "C":
            return jnp.where(i < nl, nl - 1 - i, nb - 1 - (i - nl))
        return i

    def body(a_ref, b_ref, h_ref, carry_ref):
        @pl.when(pl.program_id(0) == 0)
        def _():
            carry_ref[...] = jnp.zeros_like(carry_ref)

        def tile(i, carry):
            r = pl.multiple_of((tiles - 1 - i if reverse else i) * SUBLANE, SUBLANE)
            av = a_ref[pl.ds(r, SUBLANE), :]
            bv = b_ref[pl.ds(r, SUBLANE), :]
            row = lax.broadcasted_iota(jnp.int32, av.shape, 0)
            d = 1
            while d < SUBLANE:
                shift = SUBLANE - d if reverse else d
                a_sh = pltpu.roll(av, shift, 0)
                b_sh = pltpu.roll(bv, shift, 0)
                m = (row < SUBLANE - d) if reverse else (row >= d)
                bv = jnp.where(m, av * b_sh + bv, bv)
                av = jnp.where(m, av * a_sh, av)
                d *= 2
            hv = av * carry + bv
            h_ref[pl.ds(r, SUBLANE), :] = hv
            return jnp.sum(jnp.where(row == (0 if reverse else SUBLANE - 1), hv, 0.0), axis=0, keepdims=True)

        carry_ref[...] = lax.fori_loop(0, tiles, tile, carry_ref[...])

    spec = pl.BlockSpec((tb, c), lambda i: (block(i), 0))
    return pl.pallas_call(
        body, name=name, grid=(nb,), in_specs=[spec, spec], out_specs=spec,
        out_shape=jax.ShapeDtypeStruct((t, c), F32), scratch_shapes=[pltpu.VMEM((1, c), F32)],
        compiler_params=_cparams(("arbitrary",)),
    )(a, b)


@functools.partial(jax.custom_vjp, nondiff_argnums=(2, 3))
def linrec(a, b, n_lat, direction):
    return _linrec_call(a, b, n_lat, "AB"[direction], "linrec_fwd")


def _linrec_fwd(a, b, n_lat, direction):
    h = _linrec_call(a, b, n_lat, "AB"[direction], "linrec_fwd")
    return h, (a, h)


def _linrec_bwd(n_lat, direction, res, dh):
    a, h = res
    zero = jnp.zeros_like(a[:1])
    if direction == 0:
        a_next = jnp.concatenate([a[1:n_lat], zero, a[n_lat + 1:], a[:1]], axis=0)
        h_prev = jnp.concatenate([h[-1:], h[:n_lat - 1], zero, h[n_lat:-1]], axis=0)
    else:
        a_next = jnp.concatenate([zero, a[:-1]], axis=0)
        h_prev = jnp.concatenate([h[1:], zero], axis=0)
    g = _linrec_call(a_next, dh, n_lat, "CD"[direction], "linrec_bwd")
    return g * h_prev, g


linrec.defvjp(_linrec_fwd, _linrec_bwd)


def _attn_fwd_call(q, k, v, scale):
    h, lq, dq = q.shape
    lk, dv = v.shape[1], v.shape[2]
    tq = _tile(lq, 256, SUBLANE)

    def body(q_ref, k_ref, v_ref, o_ref, lse_ref, k_sc, v_sc):
        @pl.when(pl.program_id(1) == 0)
        def _():
            k_sc[...] = k_ref[0].astype(BF16)
            v_sc[...] = v_ref[0].astype(BF16)

        s = lax.dot_general(q_ref[0].astype(BF16), k_sc[...], (((1,), (1,)), ((), ())), preferred_element_type=F32) * scale
        m = jnp.max(s, axis=-1, keepdims=True)
        p = jnp.exp(s - m)
        l = jnp.sum(p, axis=-1, keepdims=True)
        o_ref[0] = jnp.dot(p.astype(BF16), v_sc[...], preferred_element_type=F32) / l
        lse_ref[0] = m + jnp.log(l)

    return pl.pallas_call(
        body, name="attn_fwd", grid=(h, lq // tq),
        in_specs=[pl.BlockSpec((1, tq, dq), lambda a, i: (a, i, 0)), pl.BlockSpec((1, lk, dq), lambda a, i: (a, 0, 0)),
                  pl.BlockSpec((1, lk, dv), lambda a, i: (a, 0, 0))],
        out_specs=[pl.BlockSpec((1, tq, dv), lambda a, i: (a, i, 0)), pl.BlockSpec((1, tq, 1), lambda a, i: (a, i, 0))],
        out_shape=[jax.ShapeDtypeStruct((h, lq, dv), F32), jax.ShapeDtypeStruct((h, lq, 1), F32)],
        scratch_shapes=[pltpu.VMEM((lk, dq), BF16), pltpu.VMEM((lk, dv), BF16)],
        compiler_params=_cparams(("parallel", "arbitrary")),
    )(q, k, v)


def _attn_dq_call(q, k, v, do, lse, delta, scale):
    h, lq, dq = q.shape
    lk, dv = v.shape[1], v.shape[2]
    tq = _tile(lq, 256, SUBLANE)

    def body(q_ref, k_ref, v_ref, do_ref, lse_ref, dl_ref, dq_ref, k_sc, v_sc):
        @pl.when(pl.program_id(1) == 0)
        def _():
            k_sc[...] = k_ref[0].astype(BF16)
            v_sc[...] = v_ref[0].astype(BF16)

        kb = k_sc[...]
        s = lax.dot_general(q_ref[0].astype(BF16), kb, (((1,), (1,)), ((), ())), preferred_element_type=F32) * scale
        p = jnp.exp(s - lse_ref[0])
        dp = lax.dot_general(do_ref[0].astype(BF16), v_sc[...], (((1,), (1,)), ((), ())), preferred_element_type=F32)
        ds = p * (dp - dl_ref[0])
        dq_ref[0] = jnp.dot(ds.astype(BF16), kb, preferred_element_type=F32) * scale

    return pl.pallas_call(
        body, name="attn_dq", grid=(h, lq // tq),
        in_specs=[pl.BlockSpec((1, tq, dq), lambda a, i: (a, i, 0)), pl.BlockSpec((1, lk, dq), lambda a, i: (a, 0, 0)),
                  pl.BlockSpec((1, lk, dv), lambda a, i: (a, 0, 0)), pl.BlockSpec((1, tq, dv), lambda a, i: (a, i, 0)),
                  pl.BlockSpec((1, tq, 1), lambda a, i: (a, i, 0)), pl.BlockSpec((1, tq, 1), lambda a, i: (a, i, 0))],
        out_specs=pl.BlockSpec((1, tq, dq), lambda a, i: (a, i, 0)),
        out_shape=jax.ShapeDtypeStruct((h, lq, dq), F32),
        scratch_shapes=[pltpu.VMEM((lk, dq), BF16), pltpu.VMEM((lk, dv), BF16)],
        compiler_params=_cparams(("parallel", "arbitrary")),
    )(q, k, v, do, lse, delta)


def _attn_dkv_call(q, k, v, do, lse_row, delta_row, scale):
    h, lq, dq = q.shape
    lk, dv = v.shape[1], v.shape[2]
    tk = _tile(lk, 256, SUBLANE)

    def body(q_ref, k_ref, v_ref, do_ref, lse_ref, dl_ref, dk_ref, dv_ref, q_sc, do_sc):
        @pl.when(pl.program_id(1) == 0)
        def _():
            q_sc[...] = q_ref[0].astype(BF16)
            do_sc[...] = do_ref[0].astype(BF16)

        qb = q_sc[...]
        dob = do_sc[...]
        st = lax.dot_general(k_ref[0].astype(BF16), qb, (((1,), (1,)), ((), ())), preferred_element_type=F32) * scale
        pt = jnp.exp(st - lse_ref[0])
        dpt = lax.dot_general(v_ref[0].astype(BF16), dob, (((1,), (1,)), ((), ())), preferred_element_type=F32)
        dst = pt * (dpt - dl_ref[0])
        dk_ref[0] = jnp.dot(dst.astype(BF16), qb, preferred_element_type=F32) * scale
        dv_ref[0] = jnp.dot(pt.astype(BF16), dob, preferred_element_type=F32)

    return pl.pallas_call(
        body, name="attn_dkv", grid=(h, lk // tk),
        in_specs=[pl.BlockSpec((1, lq, dq), lambda a, i: (a, 0, 0)), pl.BlockSpec((1, tk, dq), lambda a, i: (a, i, 0)),
                  pl.BlockSpec((1, tk, dv), lambda a, i: (a, i, 0)), pl.BlockSpec((1, lq, dv), lambda a, i: (a, 0, 0)),
                  pl.BlockSpec((1, 1, lq), lambda a, i: (a, 0, 0)), pl.BlockSpec((1, 1, lq), lambda a, i: (a, 0, 0))],
        out_specs=[pl.BlockSpec((1, tk, dq), lambda a, i: (a, i, 0)), pl.BlockSpec((1, tk, dv), lambda a, i: (a, i, 0))],
        out_shape=[jax.ShapeDtypeStruct((h, lk, dq), F32), jax.ShapeDtypeStruct((h, lk, dv), F32)],
        scratch_shapes=[pltpu.VMEM((lq, dq), BF16), pltpu.VMEM((lq, dv), BF16)],
        compiler_params=_cparams(("parallel", "arbitrary")),
    )(q, k, v, do, lse_row, delta_row)


@jax.custom_vjp
def attention(q, k, v):
    return _attn_fwd_call(q, k, v, q.shape[-1] ** -0.5)[0]


def _attention_fwd(q, k, v):
    o, lse = _attn_fwd_call(q, k, v, q.shape[-1] ** -0.5)
    return o, (q, k, v, o, lse)


def _attention_bwd(res, do):
    q, k, v, o, lse = res
    h, lq, _ = q.shape
    scale = q.shape[-1] ** -0.5
    delta = jnp.sum(do * o, axis=-1, keepdims=True)
    dq = _attn_dq_call(q, k, v, do, lse, delta, scale)
    dk, dv = _attn_dkv_call(q, k, v, do, lse.reshape(h, 1, lq), delta.reshape(h, 1, lq), scale)
    return dq, dk, dv


attention.defvjp(_attention_fwd, _attention_bwd)


def _na_start(r, rows):
    return jnp.clip(r - NA_WIN_H // 2, 0, rows - NA_WIN_H)


def _na_fwd_call(q, k, v, kc, vc, bias):
    n, width = q.shape
    heads, hd = width // NA_HEAD_DIM, NA_HEAD_DIM
    rows, nwin, nctx = n // GRID_W, NA_WIN_H * GRID_W, kc.shape[0]
    scale = hd ** -0.5
    nt = (((1,), (1,)), ((), ()))

    def body(q_ref, k_ref, v_ref, kc_ref, vc_ref, b_ref, o_ref, lse_ref):
        r = pl.program_id(1)
        start = pl.multiple_of(_na_start(r, rows) * GRID_W, GRID_W)
        qb = q_ref[...].astype(BF16)
        kw = k_ref[pl.ds(start, nwin), :].astype(BF16)
        vw = v_ref[pl.ds(start, nwin), :].astype(BF16)
        sw = lax.dot_general(qb, kw, nt, preferred_element_type=F32) * scale + b_ref[0, 0]
        sc = lax.dot_general(qb, kc_ref[...].astype(BF16), nt, preferred_element_type=F32) * scale
        m = jnp.maximum(jnp.max(sw, axis=-1, keepdims=True), jnp.max(sc, axis=-1, keepdims=True))
        pw = jnp.exp(sw - m)
        pc = jnp.exp(sc - m)
        l = jnp.sum(pw, axis=-1, keepdims=True) + jnp.sum(pc, axis=-1, keepdims=True)
        pw = pw / l
        pc = pc / l
        o_ref[...] = (jnp.dot(pw.astype(BF16), vw, preferred_element_type=F32)
                      + jnp.dot(pc.astype(BF16), vc_ref[...].astype(BF16), preferred_element_type=F32))
        lse_ref[0] = m + jnp.log(l)

    full = lambda rws: pl.BlockSpec((rws, hd), lambda a, r: (0, a))
    return pl.pallas_call(
        body, name="natten_fwd", grid=(heads, rows),
        in_specs=[pl.BlockSpec((GRID_W, hd), lambda a, r: (r, a)), full(n), full(n), full(nctx), full(nctx),
                  pl.BlockSpec((1, 1, GRID_W, nwin), lambda a, r: (a, r - _na_start(r, rows), 0, 0))],
        out_specs=[pl.BlockSpec((GRID_W, hd), lambda a, r: (r, a)), pl.BlockSpec((1, GRID_W, 1), lambda a, r: (a, r, 0))],
        out_shape=[jax.ShapeDtypeStruct((n, width), F32), jax.ShapeDtypeStruct((heads, n, 1), F32)],
        compiler_params=_cparams(("parallel", "arbitrary")),
    )(q, k, v, kc, vc, bias)


def _na_bwd_call(q, k, v, kc, vc, bias, bias_t, do, lse, delta, lse_row, delta_row):
    n, width = q.shape
    heads, hd = width // NA_HEAD_DIM, NA_HEAD_DIM
    rows, nwin, nctx = n // GRID_W, NA_WIN_H * GRID_W, kc.shape[0]
    scale = hd ** -0.5
    nt = (((1,), (1,)), ((), ()))

    def body(q_ref, k_ref, v_ref, kc_ref, vc_ref, b_ref, bt_ref, do_ref, lse_ref, dl_ref, lser_ref, dlr_ref,
             dq_ref, dk_ref, dv_ref, dkc_ref, dvc_ref, db_ref):
        r = pl.program_id(1)
        st = _na_start(r, rows)
        start = pl.multiple_of(st * GRID_W, GRID_W)

        @pl.when(r == 0)
        def _():
            dk_ref[...] = jnp.zeros_like(dk_ref)
            dv_ref[...] = jnp.zeros_like(dv_ref)
            dkc_ref[...] = jnp.zeros_like(dkc_ref)
            dvc_ref[...] = jnp.zeros_like(dvc_ref)

        @pl.when((r <= NA_WIN_H // 2) | (r > rows - NA_WIN_H // 2))
        def _():
            db_ref[...] = jnp.zeros_like(db_ref)

        qb = q_ref[...].astype(BF16)
        dob = do_ref[...].astype(BF16)
        kw = k_ref[pl.ds(start, nwin), :].astype(BF16)
        vw = v_ref[pl.ds(start, nwin), :].astype(BF16)
        kcb = kc_ref[...].astype(BF16)
        vcb = vc_ref[...].astype(BF16)
        lse_c, dl_c = lse_ref[0], dl_ref[0]
        pw = jnp.exp(lax.dot_general(qb, kw, nt, preferred_element_type=F32) * scale + b_ref[0, 0] - lse_c)
        pc = jnp.exp(lax.dot_general(qb, kcb, nt, preferred_element_type=F32) * scale - lse_c)
        dsw = pw * (lax.dot_general(dob, vw, nt, preferred_element_type=F32) - dl_c)
        dsc = pc * (lax.dot_general(dob, vcb, nt, preferred_element_type=F32) - dl_c)
        db_ref[0, 0] += dsw
        dq_ref[...] = (jnp.dot(dsw.astype(BF16), kw, preferred_element_type=F32)
                       + jnp.dot(dsc.astype(BF16), kcb, preferred_element_type=F32)) * scale
        lse_r, dl_r = lser_ref[0, 0], dlr_ref[0, 0]
        pwt = jnp.exp(lax.dot_general(kw, qb, nt, preferred_element_type=F32) * scale + bt_ref[0, 0] - lse_r)
        pct = jnp.exp(lax.dot_general(kcb, qb, nt, preferred_element_type=F32) * scale - lse_r)
        dswt = pwt * (lax.dot_general(vw, dob, nt, preferred_element_type=F32) - dl_r)
        dsct = pct * (lax.dot_general(vcb, dob, nt, preferred_element_type=F32) - dl_r)
        dk_ref[pl.ds(start, nwin), :] += jnp.dot(dswt.astype(BF16), qb, preferred_element_type=F32) * scale
        dv_ref[pl.ds(start, nwin), :] += jnp.dot(pwt.astype(BF16), dob, preferred_element_type=F32)
        dkc_ref[...] += jnp.dot(dsct.astype(BF16), qb, preferred_element_type=F32) * scale
        dvc_ref[...] += jnp.dot(pct.astype(BF16), dob, preferred_element_type=F32)

    full = lambda rws: pl.BlockSpec((rws, hd), lambda a, r: (0, a))
    tile = pl.BlockSpec((GRID_W, hd), lambda a, r: (r, a))
    pat = lambda r: r - _na_start(r, rows)
    col = pl.BlockSpec((1, GRID_W, 1), lambda a, r: (a, r, 0))
    rowv = pl.BlockSpec((1, 1, 1, GRID_W), lambda a, r: (a, r, 0, 0))
    return pl.pallas_call(
        body, name="natten_bwd", grid=(heads, rows),
        in_specs=[tile, full(n), full(n), full(nctx), full(nctx),
                  pl.BlockSpec((1, 1, GRID_W, nwin), lambda a, r: (a, pat(r), 0, 0)),
                  pl.BlockSpec((1, 1, nwin, GRID_W), lambda a, r: (a, pat(r), 0, 0)),
                  tile, col, col, rowv, rowv],
        out_specs=[tile, full(n), full(n), full(nctx), full(nctx),
                   pl.BlockSpec((1, 1, GRID_W, nwin), lambda a, r: (a, pat(r), 0, 0))],
        out_shape=[jax.ShapeDtypeStruct((n, width), F32), jax.ShapeDtypeStruct((n, width), F32),
                   jax.ShapeDtypeStruct((n, width), F32), jax.ShapeDtypeStruct((nctx, width), F32),
                   jax.ShapeDtypeStruct((nctx, width), F32), jax.ShapeDtypeStruct(bias.shape, F32)],
        compiler_params=_cparams(("parallel", "arbitrary")),
    )(q, k, v, kc, vc, bias, bias_t, do, lse, delta, lse_row, delta_row)


@jax.custom_vjp
def natten(q, k, v, kc, vc, bias):
    return _na_fwd_call(q, k, v, kc, vc, bias)[0]


def _natten_fwd(q, k, v, kc, vc, bias):
    o, lse = _na_fwd_call(q, k, v, kc, vc, bias)
    return o, (q, k, v, kc, vc, bias, o, lse)


def _natten_bwd(res, do):
    q, k, v, kc, vc, bias, o, lse = res
    n, width = q.shape
    heads, rows = width // NA_HEAD_DIM, n // GRID_W
    delta = jnp.sum((do * o).reshape(n, heads, NA_HEAD_DIM), axis=-1).T.reshape(heads, n, 1)
    return tuple(_na_bwd_call(q, k, v, kc, vc, bias, jnp.swapaxes(bias, 2, 3), do, lse, delta,
                              lse.reshape(heads, rows, 1, GRID_W), delta.reshape(heads, rows, 1, GRID_W)))


natten.defvjp(_natten_fwd, _natten_bwd)


def na_bias_patterns(rpb):
    heads = rpb.shape[0]
    pid = np.arange(NA_WIN_H)[:, None]
    j = np.arange(NA_WIN_H)[None, :]
    row_idx = j - pid + (NA_WIN_H - 1)
    row_hot = (row_idx[..., None] == np.arange(2 * NA_WIN_H - 1)).astype(np.float32)
    cidx = np.arange(GRID_W)
    c_start = np.clip(cidx - NA_WIN_W // 2, 0, GRID_W - NA_WIN_W)
    in_win = (cidx[None, :] >= c_start[:, None]) & (cidx[None, :] < c_start[:, None] + NA_WIN_W)
    col_idx = np.clip(cidx[None, :] - cidx[:, None], -(NA_WIN_W - 1), NA_WIN_W - 1) + (NA_WIN_W - 1)
    col_hot = (col_idx[..., None] == np.arange(2 * NA_WIN_W - 1)).astype(np.float32)
    hi = lax.Precision.HIGHEST
    tmp = jnp.einsum("hab,pja->hpjb", rpb, jnp.asarray(row_hot), precision=hi)
    bias = jnp.einsum("hpjb,qkb->hpqjk", tmp, jnp.asarray(col_hot), precision=hi)
    bias = jnp.where(jnp.asarray(in_win)[None, None, :, None, :], bias, NEG_INF)
    return bias.reshape(heads, NA_WIN_H, GRID_W, NA_WIN_H * GRID_W)


FFN_COLS = 128
FFN_HALO = SUBLANE


def _ffn_chunks(t):
    ch = _tile(t, 544, SUBLANE)
    ext = min(t, ch + 2 * FFN_HALO)
    return [(r0, ch, min(max(r0 - FFN_HALO, 0), t - ext), ext) for r0 in range(0, t, ch)]


def _ffn_conv_parts(x, start, n_lat, t):
    ext = x.shape[0]
    row = start + lax.broadcasted_iota(jnp.int32, x.shape, 0)
    no_prev = (row == 0) | (row == n_lat)
    no_next = (row == n_lat - 1) | (row == t - 1)
    return jnp.where(no_prev, 0.0, pltpu.roll(x, 1, 0)), jnp.where(no_next, 0.0, pltpu.roll(x, ext - 1, 0)), no_prev, no_next


def _sigmoid(x):
    return 1.0 / (1.0 + jnp.exp(-x))


def _ffn_act_fwd_call(uv, ug, cwv, cwg, cbv, cbg, n_lat):
    t, f = uv.shape
    tc = _tile(f, FFN_COLS, LANE)
    chunks = _ffn_chunks(t)

    def body(xv_ref, xg_ref, wv_ref, wg_ref, bv_ref, bg_ref, o_ref):
        def conv(x, start, w_ref, b_ref):
            prev, nxt, _, _ = _ffn_conv_parts(x, start, n_lat, t)
            return prev * w_ref[0:1, :] + x * w_ref[1:2, :] + nxt * w_ref[2:3, :] + b_ref[...]

        for r0, ch, start, ext in chunks:
            cv = conv(xv_ref[start:start + ext, :], start, wv_ref, bv_ref)
            cg = conv(xg_ref[start:start + ext, :], start, wg_ref, bg_ref)
            act = (cg * _sigmoid(cg)) * cv
            o_ref[r0:r0 + ch, :] = act[r0 - start:r0 - start + ch, :]

    col = pl.BlockSpec((t, tc), lambda j: (0, j))
    wsp = pl.BlockSpec((FFN_CONV_W, tc), lambda j: (0, j))
    bsp = pl.BlockSpec((1, tc), lambda j: (0, j))
    return pl.pallas_call(
        body, name="ffn_act_fwd", grid=(f // tc,), in_specs=[col, col, wsp, wsp, bsp, bsp], out_specs=col,
        out_shape=jax.ShapeDtypeStruct((t, f), F32), compiler_params=_cparams(("parallel",)),
    )(uv, ug, cwv, cwg, cbv.reshape(1, f), cbg.reshape(1, f))


def _ffn_act_bwd_call(uv, ug, cwv, cwg, cbv, cbg, dact, n_lat):
    t, f = uv.shape
    tc = _tile(f, FFN_COLS, LANE)
    chunks = _ffn_chunks(t)

    def body(xv_ref, xg_ref, wv_ref, wg_ref, bv_ref, bg_ref, da_ref, dxv_ref, dxg_ref, dw_ref):
        dw_ref[...] = jnp.zeros_like(dw_ref)
        for r0, ch, start, ext in chunks:
            lo = r0 - start
            xv = xv_ref[start:start + ext, :]
            xg = xg_ref[start:start + ext, :]
            da = da_ref[start:start + ext, :]
            pv, nv, no_prev, no_next = _ffn_conv_parts(xv, start, n_lat, t)
            pg, ng, _, _ = _ffn_conv_parts(xg, start, n_lat, t)
            cv = pv * wv_ref[0:1, :] + xv * wv_ref[1:2, :] + nv * wv_ref[2:3, :] + bv_ref[...]
            cg = pg * wg_ref[0:1, :] + xg * wg_ref[1:2, :] + ng * wg_ref[2:3, :] + bg_ref[...]
            sig = _sigmoid(cg)
            dcv = da * (cg * sig)
            dcg = da * cv * (sig * (1.0 + cg * (1.0 - sig)))
            for base, w_ref, dc, parts, dx_ref in ((0, wv_ref, dcv, (pv, xv, nv), dxv_ref), (4, wg_ref, dcg, (pg, xg, ng), dxg_ref)):
                dc_next = jnp.where(no_next, 0.0, pltpu.roll(dc, ext - 1, 0))
                dc_prev = jnp.where(no_prev, 0.0, pltpu.roll(dc, 1, 0))
                dx = dc_next * w_ref[0:1, :] + dc * w_ref[1:2, :] + dc_prev * w_ref[2:3, :]
                dx_ref[r0:r0 + ch, :] = dx[lo:lo + ch, :]
                dci = dc[lo:lo + ch, :]
                for k, part in enumerate(parts):
                    dw_ref[base + k:base + k + 1, :] += jnp.sum(dci * part[lo:lo + ch, :], axis=0, keepdims=True)
                dw_ref[base + 3:base + 4, :] += jnp.sum(dci, axis=0, keepdims=True)

    col = pl.BlockSpec((t, tc), lambda j: (0, j))
    wsp = pl.BlockSpec((FFN_CONV_W, tc), lambda j: (0, j))
    bsp = pl.BlockSpec((1, tc), lambda j: (0, j))
    return pl.pallas_call(
        body, name="ffn_act_bwd", grid=(f // tc,), in_specs=[col, col, wsp, wsp, bsp, bsp, col],
        out_specs=[col, col, pl.BlockSpec((2 * (FFN_CONV_W + 1), tc), lambda j: (0, j))],
        out_shape=[jax.ShapeDtypeStruct((t, f), F32), jax.ShapeDtypeStruct((t, f), F32),
                   jax.ShapeDtypeStruct((2 * (FFN_CONV_W + 1), f), F32)],
        compiler_params=_cparams(("parallel",)),
    )(uv, ug, cwv, cwg, cbv.reshape(1, f), cbg.reshape(1, f), dact)


@functools.partial(jax.custom_vjp, nondiff_argnums=(6,))
def ffn_act(uv, ug, cwv, cwg, cbv, cbg, n_lat):
    return _ffn_act_fwd_call(uv, ug, cwv, cwg, cbv, cbg, n_lat)


def _ffn_act_fwd(uv, ug, cwv, cwg, cbv, cbg, n_lat):
    return _ffn_act_fwd_call(uv, ug, cwv, cwg, cbv, cbg, n_lat), (uv, ug, cwv, cwg, cbv, cbg)


def _ffn_act_bwd(n_lat, res, dact):
    duv, dug, dw = _ffn_act_bwd_call(*res, dact, n_lat)
    nw = FFN_CONV_W
    return duv, dug, dw[:nw], dw[nw + 1:2 * nw + 1], dw[nw], dw[2 * nw + 1]


ffn_act.defvjp(_ffn_act_fwd, _ffn_act_bwd)


def adamw(w, g, m, v):
    shape = w.shape
    last = shape[-1]
    r = int(np.prod(shape[:-1]))
    w2, g2, m2, v2 = (t.reshape(1, r, last) for t in (w, g, m, v))
    tr = _tile(r, max(SUBLANE, (1 << 18) // last), SUBLANE)

    def body(w_ref, g_ref, m_ref, v_ref, d_ref, mo_ref, vo_ref):
        gv = g_ref[...]
        mn = ADAM_B1 * m_ref[...] + (1.0 - ADAM_B1) * gv
        vn = ADAM_B2 * v_ref[...] + (1.0 - ADAM_B2) * (gv * gv)
        m_hat = mn / (1.0 - ADAM_B1 ** ADAM_STEP)
        v_hat = vn / (1.0 - ADAM_B2 ** ADAM_STEP)
        d_ref[...] = -ADAM_LR * (m_hat / (jnp.sqrt(v_hat) + ADAM_EPS) + ADAM_WD * w_ref[...])
        mo_ref[...] = mn
        vo_ref[...] = vn

    spec = pl.BlockSpec((1, tr, last), lambda i: (0, i, 0))
    out = pl.pallas_call(
        body, name="adamw", grid=(r // tr,), in_specs=[spec] * 4, out_specs=[spec] * 3,
        out_shape=[jax.ShapeDtypeStruct((1, r, last), F32)] * 3, compiler_params=_cparams(("parallel",)),
    )(w2, g2, m2, v2)
    return tuple(t.reshape(shape) for t in out)


def _place():
    x, y, c = lax.axis_index("x"), lax.axis_index("y"), lax.axis_index("c")
    chips = [(1 - x, y), (x, 1 - y), (1 - x, 1 - y)]
    return x, y, c, chips


HBM_SPEC = pl.BlockSpec(memory_space=pltpu.HBM)


def all_gather_chips(xs, name):
    def body(x_ref, out_ref, send_sems, recv_sems):
        x, y, c, chips = _place()
        sibling = (x, y, 1 - c)

        def piece(chip, half):
            return out_ref.at[2 * chip[0] + chip[1], half]

        def copy(k, chip, half, to, src=None):
            dst = piece(chip, half)
            return pltpu.make_async_remote_copy(src_ref=dst if src is None else src, dst_ref=dst, send_sem=send_sems.at[k],
                                                recv_sem=recv_sems.at[k], device_id=to, device_id_type=MESH)

        first = [copy(k, (x, y), c, (*chip, c), src=x_ref.at[c]) for k, chip in enumerate(chips)]
        for cp in first:
            cp.start()
        passed = [copy(3 + k, chip, c, sibling) for k, chip in enumerate(chips)]
        for k, chip in enumerate(chips):
            copy(k, chip, c, sibling).wait_recv()
            passed[k].start()
        for k, chip in enumerate(chips):
            copy(3 + k, chip, 1 - c, sibling).wait_recv()
        for cp in first + passed:
            cp.wait_send()

    out = pl.pallas_call(
        body, name=name, in_specs=[HBM_SPEC], out_specs=HBM_SPEC,
        out_shape=jax.ShapeDtypeStruct((N_CHIPS,) + xs.shape, xs.dtype),
        scratch_shapes=[pltpu.SemaphoreType.DMA((6,)), pltpu.SemaphoreType.DMA((6,))],
        compiler_params=pltpu.CompilerParams(has_side_effects=True),
    )(xs)
    me = 2 * lax.axis_index("x") + lax.axis_index("y")
    return lax.dynamic_update_slice(out, xs[None], (me,) + (0,) * xs.ndim)


def _rs_to_sibling(g, name):
    _, _, r, b = g.shape

    def body(g_ref, out_ref, send_sems, recv_sems):
        x, y, c, _ = _place()
        cps = [pltpu.make_async_remote_copy(src_ref=g_ref.at[s, 1 - c], dst_ref=out_ref.at[s], send_sem=send_sems.at[s],
                                            recv_sem=recv_sems.at[s], device_id=(x, y, 1 - c), device_id_type=MESH)
               for s in range(N_CHIPS)]
        for cp in cps:
            cp.start()
        for cp in cps:
            cp.wait()

    return pl.pallas_call(
        body, name=name, in_specs=[HBM_SPEC], out_specs=HBM_SPEC, out_shape=jax.ShapeDtypeStruct((N_CHIPS, r, b), g.dtype),
        scratch_shapes=[pltpu.SemaphoreType.DMA((N_CHIPS,)), pltpu.SemaphoreType.DMA((N_CHIPS,))],
        compiler_params=pltpu.CompilerParams(has_side_effects=True),
    )(g)


def _rs_chip_sum(g, la, c_idx, name):
    _, _, r, b = g.shape
    tr = _tile(r, max(16, (1 << 18) // b), 16)

    def body(c_ref, g_ref, la_ref, o_ref):
        o_ref[...] = (g_ref[:, 0] + la_ref[...]).astype(BF16)

    return pl.pallas_call(
        body, name=name,
        grid_spec=pltpu.PrefetchScalarGridSpec(
            num_scalar_prefetch=1, grid=(N_CHIPS, r // tr),
            in_specs=[pl.BlockSpec((1, 1, tr, b), lambda s, i, c_ref: (s, c_ref[0], i, 0)),
                      pl.BlockSpec((1, tr, b), lambda s, i, c_ref: (s, i, 0))],
            out_specs=pl.BlockSpec((1, tr, b), lambda s, i, c_ref: (s, i, 0))),
        out_shape=jax.ShapeDtypeStruct((N_CHIPS, r, b), BF16), compiler_params=_cparams(("parallel", "parallel")),
    )(c_idx, g, la)


def _rs_to_owners(hb, name):
    _, r, b = hb.shape

    def body(h_ref, out_ref, send_sems, recv_sems):
        x, y, c, chips = _place()
        cps = [pltpu.make_async_remote_copy(src_ref=h_ref.at[2 * chip[0] + chip[1]], dst_ref=out_ref.at[k],
                                            send_sem=send_sems.at[k], recv_sem=recv_sems.at[k], device_id=(*chip, c),
                                            device_id_type=MESH) for k, chip in enumerate(chips)]
        for cp in cps:
            cp.start()
        for cp in cps:
            cp.wait()

    return pl.pallas_call(
        body, name=name, in_specs=[HBM_SPEC], out_specs=HBM_SPEC, out_shape=jax.ShapeDtypeStruct((3, r, b), hb.dtype),
        scratch_shapes=[pltpu.SemaphoreType.DMA((3,)), pltpu.SemaphoreType.DMA((3,))],
        compiler_params=pltpu.CompilerParams(has_side_effects=True),
    )(hb)


def _rs_final_sum(g, la, lb, sc_idx, name):
    _, _, r, b = g.shape
    tr = _tile(r, max(16, (1 << 18) // b), 16)

    def body(i_ref, g_ref, la_ref, lb_ref, o_ref):
        o_ref[...] = (g_ref[0, 0] + la_ref[0]) + lb_ref[0].astype(F32) + lb_ref[1].astype(F32) + lb_ref[2].astype(F32)

    return pl.pallas_call(
        body, name=name,
        grid_spec=pltpu.PrefetchScalarGridSpec(
            num_scalar_prefetch=1, grid=(r // tr,),
            in_specs=[pl.BlockSpec((1, 1, tr, b), lambda i, i_ref: (i_ref[0], i_ref[1], i, 0)),
                      pl.BlockSpec((1, tr, b), lambda i, i_ref: (i_ref[0], i, 0)),
                      pl.BlockSpec((3, tr, b), lambda i, i_ref: (0, i, 0))],
            out_specs=pl.BlockSpec((tr, b), lambda i, i_ref: (i, 0))),
        out_shape=jax.ShapeDtypeStruct((r, b), F32), compiler_params=_cparams(("parallel",)),
    )(sc_idx, g, la, lb)


def _rs_join_halves(f, name):
    r, b = f.shape

    def body(f_ref, out_ref, send_sem, recv_sem):
        x, y, c, _ = _place()
        cp = pltpu.make_async_remote_copy(src_ref=f_ref, dst_ref=out_ref.at[c], send_sem=send_sem, recv_sem=recv_sem,
                                          device_id=(x, y, 1 - c), device_id_type=MESH)
        cp.start()
        pltpu.make_async_remote_copy(src_ref=f_ref, dst_ref=out_ref.at[1 - c], send_sem=send_sem, recv_sem=recv_sem,
                                     device_id=(x, y, 1 - c), device_id_type=MESH).wait_recv()
        cp.wait_send()

    out = pl.pallas_call(
        body, name=name, in_specs=[HBM_SPEC], out_specs=HBM_SPEC, out_shape=jax.ShapeDtypeStruct((2, r, b), f.dtype),
        scratch_shapes=[pltpu.SemaphoreType.DMA, pltpu.SemaphoreType.DMA],
        compiler_params=pltpu.CompilerParams(has_side_effects=True),
    )(f)
    return lax.dynamic_update_slice(out, f[None], (lax.axis_index("c"), 0, 0))


def reduce_scatter_devices(g, tag):
    c = lax.axis_index("c")
    me = 2 * lax.axis_index("x") + lax.axis_index("y")
    la = _rs_to_sibling(g, "rs_sibling_" + tag)
    hb = _rs_chip_sum(g, la, jnp.stack([c]).astype(jnp.int32), "rs_chipsum_" + tag)
    lb = _rs_to_owners(hb, "rs_owners_" + tag)
    f = _rs_final_sum(g, la, lb, jnp.stack([me, c]).astype(jnp.int32), "rs_final_" + tag)
    return _rs_join_halves(f, "rs_join_" + tag)


W_IN_SIZES = ("na", "lru_x", "lru_g", "cq", "ckv", "kr", "gates")


def _w_in_layout(d_model, na_width, lru_width, q_rank, kv_rank):
    sizes = dict(na=3 * na_width, lru_x=lru_width, lru_g=lru_width, cq=q_rank, ckv=kv_rank, kr=MLA_ROPE_DIM,
                 gates=N_BRANCH * d_model)
    head = sizes["na"] + 2 * lru_width + q_rank + kv_rank
    total = head + MLA_ROPE_DIM + sizes["gates"]
    padded = -(-total // LANE) * LANE
    return sizes, head, total, padded


def _permute_w_in(w, head, total, padded):
    parts = [w[:, :head], w[:, head + MLA_ROPE_DIM:total], w[:, head:head + MLA_ROPE_DIM]]
    if padded > total:
        parts.append(jnp.zeros((w.shape[0], padded - total), w.dtype))
    return jnp.concatenate(parts, axis=1)


def _unpermute_w_in(g, head, total):
    n_gates = total - head - MLA_ROPE_DIM
    return jnp.concatenate([g[:, :head], g[:, head + n_gates:head + n_gates + MLA_ROPE_DIM], g[:, head:head + n_gates]], axis=1)


@functools.partial(jax.custom_vjp, nondiff_argnums=(1,))
def split_cols(z, sizes):
    out, o = [], 0
    for sz in sizes:
        out.append(z[:, o:o + sz])
        o += sz
    return tuple(out)


def _split_cols_fwd(z, sizes):
    return split_cols(z, sizes), None


def _split_cols_bwd(sizes, res, cts):
    return (jnp.concatenate(cts, axis=1),)


split_cols.defvjp(_split_cols_fwd, _split_cols_bwd)


def _dwconv(x, w, b):
    width = w.shape[0]
    n = x.shape[0]
    left = width // 2
    xp = jnp.pad(x, ((left, width - 1 - left), (0, 0)))
    return sum(xp[i:i + n] * w[i] for i in range(width)) + b


def _dwconv_seg(x, w, b, n_lat):
    return jnp.concatenate([_dwconv(x[:n_lat], w, b), _dwconv(x[n_lat:], w, b)], axis=0)


def _rope(x, cos, sin):
    half = x.shape[-1] // 2
    x1, x2 = x[..., :half], x[..., half:]
    return jnp.concatenate([x1 * cos - x2 * sin, x1 * sin + x2 * cos], axis=-1)


def _rope_tables(n_lat, n_ctx):
    t = jnp.arange(n_lat, dtype=jnp.int32)
    row = (t // GRID_W).astype(F32)
    col = (t % GRID_W).astype(F32)
    n_freq = MLA_ROPE_DIM // 4
    inv_freq = ROPE_THETA ** (-jnp.arange(n_freq, dtype=F32) / n_freq)
    ang = jnp.concatenate([row[:, None] * inv_freq, col[:, None] * inv_freq], axis=-1)
    ones = jnp.ones((n_ctx, MLA_ROPE_DIM // 2), F32)
    return jnp.concatenate([jnp.cos(ang), ones], axis=0), jnp.concatenate([jnp.sin(ang), 0.0 * ones], axis=0)


def _heads_first(t, heads):
    n = t.shape[0]
    return t.reshape(n, heads, -1).transpose(1, 0, 2)


def _heads_last(t):
    return t.transpose(1, 0, 2).reshape(t.shape[1], -1)


def _lru_coeffs(u, w_a, b_a, w_x, b_x, lam):
    r = jax.nn.sigmoid(bd_dot(u, w_a) + b_a)
    i = jax.nn.sigmoid(bd_dot(u, w_x) + b_x)
    log_a = -LRU_C * r * jax.nn.softplus(-lam)
    return jnp.exp(log_a), jnp.sqrt(-jnp.expm1(2.0 * log_a)) * (i * u)


def _forward_loss(big, prox, small, x, ctx, silu_c, target):
    n_lat, d_model = x.shape
    n_ctx = ctx.shape[0]
    na_width = NA_HEADS * NA_HEAD_DIM
    lru_width = small["lru_conv_b"].shape[1]
    q_rank, kv_rank = small["mla_q_norm"].shape[1], small["mla_kv_norm"].shape[1]
    _, head, total, padded = _w_in_layout(d_model, na_width, lru_width, q_rank, kv_rank)
    cos, sin = _rope_tables(n_lat, n_ctx)
    is_ctx = (jnp.arange(n_lat + n_ctx) >= n_lat)[:, None]

    def by_row(v2):
        return jnp.where(is_ctx, v2[1][None, :], v2[0][None, :])

    s_rows = jnp.concatenate([silu_c, jax.nn.silu(small["c_ctx"])[None, :],
                              jnp.zeros((MOD_ROWS - 2, d_model), F32)], axis=0)
    xs = jnp.concatenate([x, ctx], axis=0)
    for l in range(DEPTH):
        mod = (pdot(s_rows, big["w_mod"][l], prox["w_mod"][l]) + small["b_mod"][l])[:2]
        sh1, sc1, g1, sh2, sc2, g2 = split_cols(mod, (d_model,) * N_MOD)
        h = norm_mod(xs, small["norm_mix"][l], sh1, sc1, n_lat)
        z = pdot(h, big["w_in"][l], prox["w_in"][l])
        names = ("q_a", "k_a", "v_a", "lru_x", "lru_g", "cq", "ckv", "gate_a", "gate_b", "gate_c", "kr", "pad")
        widths = (na_width,) * 3 + (lru_width,) * 2 + (q_rank, kv_rank) + (d_model,) * 3 + (MLA_ROPE_DIM, padded - total)
        cols = dict(zip(names, split_cols(z, widths)))
        q_a, k_a, v_a = cols["q_a"], cols["k_a"], cols["v_a"]
        out_a_lat = natten(q_a[:n_lat], k_a[:n_lat], v_a[:n_lat], k_a[n_lat:], v_a[n_lat:], na_bias_patterns(small["na_rpb"][l]))
        out_a_ctx = _heads_last(attention(*(_heads_first(t[n_lat:], NA_HEADS) for t in (q_a, k_a, v_a))))
        out_a = jnp.concatenate([out_a_lat, out_a_ctx], axis=0)
        u = _dwconv_seg(cols["lru_x"], small["lru_conv_w"][l], small["lru_conv_b"][l], n_lat)
        hs = [linrec(*_lru_coeffs(u, small["lru_w_a"][l, d], small["lru_b_a"][l, d], small["lru_w_x"][l, d],
                                  small["lru_b_x"][l, d], small["lru_lam"][l, d]), n_lat, d) for d in range(2)]
        y_b = hs[0] + hs[1]
        out_b = jax.nn.gelu(cols["lru_g"]) * y_b
        q_m = pdot(rmsnorm_p(cols["cq"], small["mla_q_norm"][l]), big["mla_w_q_up"][l], prox["mla_w_q_up"][l])
        q_m = q_m.reshape(-1, MLA_HEADS, MLA_NOPE_DIM + MLA_ROPE_DIM)
        q_m = jnp.concatenate([q_m[..., :MLA_NOPE_DIM], _rope(q_m[..., MLA_NOPE_DIM:], cos[:, None, :], sin[:, None, :])], axis=-1)
        kv = pdot(rmsnorm_p(cols["ckv"], small["mla_kv_norm"][l]), big["mla_w_kv_up"][l], prox["mla_w_kv_up"][l])
        kv = kv.reshape(-1, MLA_HEADS, MLA_NOPE_DIM + MLA_V_DIM)
        k_rope = jnp.broadcast_to(_rope(cols["kr"], cos, sin)[:, None, :], (n_lat + n_ctx, MLA_HEADS, MLA_ROPE_DIM))
        k_m = jnp.concatenate([kv[..., :MLA_NOPE_DIM], k_rope], axis=-1).transpose(1, 0, 2)
        v_m = kv[..., MLA_NOPE_DIM:].transpose(1, 0, 2)
        q_m = q_m.transpose(1, 0, 2)
        out_c = jnp.concatenate([_heads_last(attention(q_m[:, :n_lat], k_m, v_m)),
                                 _heads_last(attention(q_m[:, n_lat:], k_m[:, n_lat:], v_m[:, n_lat:]))], axis=0)
        y = sum(jax.nn.sigmoid(cols[gate]) * pdot(br, big["w_branch"][l][i], prox["w_branch"][l][i])
                for i, (gate, br) in enumerate(zip(("gate_a", "gate_b", "gate_c"), (out_a, out_b, out_c))))
        xs = xs + by_row(g1) * pdot(y, big["w_out"][l], prox["w_out"][l])
        h2 = norm_mod(xs, small["norm_ffn"][l], sh2, sc2, n_lat)
        d_ff = big["ffn_w_down"][l].shape[0]
        halves = (d_ff, d_ff)
        w_val, w_gate = split_cols(big["ffn_w_up"][l], halves)
        p_val, p_gate = split_cols(prox["ffn_w_up"][l], halves)
        cw_val, cw_gate = split_cols(small["ffn_conv_w"][l], halves)
        cb_val, cb_gate = split_cols(small["ffn_conv_b"][l][None, :], halves)
        act = ffn_act(pdot(h2, w_val, p_val), pdot(h2, w_gate, p_gate), cw_val, cw_gate, cb_val[0], cb_gate[0], n_lat)
        xs = xs + by_row(g2) * pdot(act, big["ffn_w_down"][l], prox["ffn_w_down"][l])
    y_out = rmsnorm_p(xs[:n_lat], small["norm_final"])
    return 0.5 * jnp.sum(jnp.mean(jnp.square(y_out - target), axis=-1))


WEIGHTS = ['c_ctx', 'w_mod', 'b_mod', 'norm_mix', 'norm_ffn', 'w_in', 'na_rpb', 'lru_conv_w', 'lru_conv_b', 'lru_w_a',
           'lru_b_a', 'lru_w_x', 'lru_b_x', 'lru_lam', 'mla_q_norm', 'mla_kv_norm', 'mla_w_q_up', 'mla_w_kv_up', 'w_branch',
           'w_out', 'ffn_w_up', 'ffn_conv_w', 'ffn_conv_b', 'ffn_w_down', 'norm_final']
BIG = {'w_mod': 2, 'w_in': 2, 'mla_w_q_up': 2, 'mla_w_kv_up': 2, 'w_branch': 3, 'w_out': 1, 'ffn_w_up': 2, 'ffn_w_down': 1}
SMALL_SHARDED = {'lru_conv_w': 2, 'lru_b_a': 2, 'lru_b_x': 2, 'lru_lam': 2, 'ffn_conv_w': 2}
PACK_LANES = 128
PACK_ROWS = 16


def _gathered_to_full(g, ax):
    t = jnp.moveaxis(g, 0, ax)
    return t.reshape(t.shape[:ax] + (t.shape[ax] * t.shape[ax + 1],) + t.shape[ax + 2:])


def _full_to_stacked(g, ax):
    n = g.shape[ax]
    t = g.reshape(g.shape[:ax] + (N_CHIPS, n // N_CHIPS) + g.shape[ax + 1:])
    return jnp.moveaxis(t, ax, 0)


def _canon(shape):
    return (2, int(np.prod(shape[:-1])) // 2, shape[-1])


def _pack(parts, multiple):
    flat = jnp.concatenate([p.reshape(-1) for p in parts])
    n = flat.shape[0]
    padded = -(-n // multiple) * multiple
    return jnp.pad(flat, (0, padded - n))


def _unpack(flat, shapes):
    out, o = [], 0
    for s in shapes:
        n = int(np.prod(s))
        out.append(flat[o:o + n].reshape(s))
        o += n
    return out


def kernel(x, c, ctx, c_ctx, w_mod, b_mod, norm_mix, norm_ffn, w_in, na_rpb, lru_conv_w, lru_conv_b, lru_w_a, lru_b_a, lru_w_x, lru_b_x, lru_lam, mla_q_norm, mla_kv_norm, mla_w_q_up, mla_w_kv_up, w_branch, w_out, ffn_w_up, ffn_conv_w, ffn_conv_b, ffn_w_down, norm_final, loss_target, m_c_ctx, m_w_mod, m_b_mod, m_norm_mix, m_norm_ffn, m_w_in, m_na_rpb, m_lru_conv_w, m_lru_conv_b, m_lru_w_a, m_lru_b_a, m_lru_w_x, m_lru_b_x, m_lru_lam, m_mla_q_norm, m_mla_kv_norm, m_mla_w_q_up, m_mla_w_kv_up, m_w_branch, m_w_out, m_ffn_w_up, m_ffn_conv_w, m_ffn_conv_b, m_ffn_w_down, m_norm_final, v_c_ctx, v_w_mod, v_b_mod, v_norm_mix, v_norm_ffn, v_w_in, v_na_rpb, v_lru_conv_w, v_lru_conv_b, v_lru_w_a, v_lru_b_a, v_lru_w_x, v_lru_b_x, v_lru_lam, v_mla_q_norm, v_mla_kv_norm, v_mla_w_q_up, v_mla_w_kv_up, v_w_branch, v_w_out, v_ffn_w_up, v_ffn_conv_w, v_ffn_conv_b, v_ffn_w_down, v_norm_final):
    args = dict(locals())
    w = {n: args[n] for n in WEIGHTS}
    m = {n: args["m_" + n] for n in WEIGHTS}
    v = {n: args["v_" + n] for n in WEIGHTS}
    me = 2 * lax.axis_index("x") + lax.axis_index("y")

    full = {}
    for name, ax in BIG.items():
        shard = w[name]
        g = all_gather_chips(shard.astype(BF16).reshape(_canon(shard.shape)), "ag_" + name)
        full[name] = _gathered_to_full(g.reshape((N_CHIPS,) + shard.shape), ax)
    packed = _pack([w[n] for n in SMALL_SHARDED], 2 * PACK_ROWS * PACK_LANES).reshape(2, -1, PACK_LANES)
    g = all_gather_chips(packed, "ag_small").reshape(N_CHIPS, -1)
    for name, t in zip(SMALL_SHARDED, zip(*[_unpack(g[s], [w[n].shape for n in SMALL_SHARDED]) for s in range(N_CHIPS)])):
        full[name] = _gathered_to_full(jnp.stack(t), SMALL_SHARDED[name])
    small = {n: full.get(n, w[n]) for n in WEIGHTS if n not in BIG}

    d_model = x.shape[-1]
    _, head, total, padded = _w_in_layout(d_model, NA_HEADS * NA_HEAD_DIM, lru_conv_b.shape[1], mla_q_norm.shape[1], mla_kv_norm.shape[1])
    big = {n: [full[n][l] for l in range(DEPTH)] for n in BIG}
    big["w_in"] = [_permute_w_in(t, head, total, padded) for t in big["w_in"]]
    big["w_branch"] = [[t[i] for i in range(N_BRANCH)] for t in big["w_branch"]]
    prox = jax.tree.map(lambda t: jnp.zeros(t.shape, F32), big)

    loss, (g_prox, g_small, g_x) = jax.value_and_grad(_forward_loss, argnums=(1, 2, 3))(
        big, prox, small, x[0], ctx[0], jax.nn.silu(c), loss_target[0])
    loss = lax.psum(loss, ("x", "y", "c"))

    grads = {}
    g_prox["w_in"] = [_unpermute_w_in(t, head, total) for t in g_prox["w_in"]]
    g_prox["w_branch"] = [jnp.stack(t) for t in g_prox["w_branch"]]
    for name, ax in BIG.items():
        stacked = _full_to_stacked(jnp.stack(g_prox[name]), ax)
        red = reduce_scatter_devices(stacked.reshape((N_CHIPS,) + _canon(w[name].shape)), name)
        grads[name] = red.reshape(w[name].shape)
    small_names = [n for n in WEIGHTS if n not in BIG]
    packed = _pack([g_small[n] for n in small_names], N_CHIPS * 2 * PACK_ROWS * PACK_LANES).reshape(N_CHIPS, 2, -1, PACK_LANES)
    red = all_gather_chips(reduce_scatter_devices(packed, "small"), "ag_small_grads").reshape(-1)
    for name, t in zip(small_names, _unpack(red, [g_small[n].shape for n in small_names])):
        if name in SMALL_SHARDED:
            ax = SMALL_SHARDED[name]
            t = lax.dynamic_slice_in_dim(t, me * w[name].shape[ax], w[name].shape[ax], axis=ax)
        grads[name] = t

    upd = {n: adamw(w[n], grads[n], m[n], v[n]) for n in WEIGHTS}
    return (loss, g_x[None], *[grads[n] for n in WEIGHTS], *[upd[n][0] for n in WEIGHTS],
            *[upd[n][1] for n in WEIGHTS], *[upd[n][2] for n in WEIGHTS])
```

```python
import functools
import math

import numpy as np
import jax
import jax.numpy as jnp
from jax import lax
from jax.experimental import pallas as pl
from jax.experimental.pallas import tpu as pltpu

F32 = jnp.float32
BF16 = jnp.bfloat16
MESH = pl.DeviceIdType.MESH

DEPTH = 4
GRID_W = 64
NORM_EPS = 1e-6
NEG_INF = -1e30
N_MOD = 6
NA_HEADS = 8
NA_HEAD_DIM = 128
NA_WIN_H = 8
NA_WIN_W = 16
LRU_BLOCKS = 8
LRU_CONV_W = 4
LRU_C = 8.0
MLA_HEADS = 8
MLA_NOPE_DIM = 128
MLA_ROPE_DIM = 64
MLA_V_DIM = 128
ROPE_THETA = 10000.0
N_BRANCH = 3
FFN_CONV_W = 3
ADAM_LR = 0.001
ADAM_B1 = 0.9
ADAM_B2 = 0.999
ADAM_EPS = 1e-08
ADAM_WD = 0.01
ADAM_STEP = 10

N_CHIPS = 4
LANE = 128
SUBLANE = 8
VMEM_LIMIT = 48 * 1024 * 1024
MOD_ROWS = 128


def _cparams(sem=None, **kw):
    if sem is not None:
        kw["dimension_semantics"] = sem
    return pltpu.CompilerParams(vmem_limit_bytes=VMEM_LIMIT, **kw)


def _tile(d, cap, align):
    best = None
    t = align
    while t <= min(d, cap):
        if d % t == 0:
            best = t
        t += align
    return d if best is None else best


FULL_K_MAX = 2304


def _mm_rows(a, b, *, tb, emit, name):
    m, k = a.shape
    n = b.shape[0] if tb else b.shape[1]
    tm = _tile(m, 544, 16)
    tn = _tile(n, 640, LANE)
    dims = (((1,), (1 if tb else 0,)), ((), ()))

    def body(a_ref, b_ref, o_ref, *rest):
        a_sc = rest[-1]

        @pl.when(pl.program_id(1) == 0)
        def _():
            a_sc[...] = a_ref[...].astype(BF16)
            if emit:
                rest[0][...] = a_sc[...]

        o_ref[...] = lax.dot_general(a_sc[...], b_ref[...].astype(BF16), dims, preferred_element_type=F32)

    b_spec = pl.BlockSpec((tn, k), lambda i, j: (j, 0)) if tb else pl.BlockSpec((k, tn), lambda i, j: (0, j))
    out_specs = [pl.BlockSpec((tm, tn), lambda i, j: (i, j))]
    out_shape = [jax.ShapeDtypeStruct((m, n), F32)]
    if emit:
        out_specs.append(pl.BlockSpec((tm, k), lambda i, j: (i, 0)))
        out_shape.append(jax.ShapeDtypeStruct((m, k), BF16))
    out = pl.pallas_call(
        body, name=name, grid=(m // tm, n // tn),
        in_specs=[pl.BlockSpec((tm, k), lambda i, j: (i, 0)), b_spec], out_specs=out_specs, out_shape=out_shape,
        scratch_shapes=[pltpu.VMEM((tm, k), BF16)], compiler_params=_cparams(("parallel", "arbitrary")),
    )(a, b)
    return out if emit else out[0]


def _mm_acc(a, b, *, ta=False, tb=False, name):
    m, k = (a.shape[1], a.shape[0]) if ta else a.shape
    n = b.shape[0] if tb else b.shape[1]
    tm = _tile(m, 2048 if ta else 1088, 16)
    tn = _tile(n, 640 if ta else 2048, LANE)
    tk = _tile(k, 1088, 16) if ta else _tile(k, 640, LANE)
    dims = (((0 if ta else 1,), (1 if tb else 0,)), ((), ()))

    def body(a_ref, b_ref, o_ref):
        @pl.when(pl.program_id(2) == 0)
        def _():
            o_ref[...] = jnp.zeros_like(o_ref)

        o_ref[...] += lax.dot_general(a_ref[...].astype(BF16), b_ref[...].astype(BF16), dims, preferred_element_type=F32)

    a_spec = pl.BlockSpec((tk, tm), lambda i, j, kk: (kk, i)) if ta else pl.BlockSpec((tm, tk), lambda i, j, kk: (i, kk))
    b_spec = pl.BlockSpec((tn, tk), lambda i, j, kk: (j, kk)) if tb else pl.BlockSpec((tk, tn), lambda i, j, kk: (kk, j))
    return pl.pallas_call(
        body, name=name, grid=(m // tm, n // tn, k // tk),
        in_specs=[a_spec, b_spec], out_specs=pl.BlockSpec((tm, tn), lambda i, j, kk: (i, j)),
        out_shape=jax.ShapeDtypeStruct((m, n), F32),
        compiler_params=_cparams(("parallel", "parallel", "arbitrary")),
    )(a, b)


def _pdot_fwd(a, w, wp):
    if a.shape[1] <= FULL_K_MAX:
        out, a_bf = _mm_rows(a, w, tb=False, emit=True, name="mm_fwd")
        return out, (a_bf, w)
    return _mm_acc(a, w, name="mm_fwd_acc"), (a, w)


@jax.custom_vjp
def pdot(a, w, wp):
    return _pdot_fwd(a, w, wp)[0]


def _pdot_dgrad(w, g):
    if w.shape[1] <= FULL_K_MAX:
        return _mm_rows(g, w, tb=True, emit=False, name="mm_dgrad")
    return _mm_acc(g, w, tb=True, name="mm_dgrad_acc")


def _pdot_bwd(res, g):
    a, w = res
    return _pdot_dgrad(w, g), jnp.zeros_like(w), _mm_acc(a, g, ta=True, name="mm_wgrad")


pdot.defvjp(_pdot_fwd, _pdot_bwd)


@jax.custom_vjp
def pdot_act(a, w):
    return _pdot_fwd(a, w, None)[0]


def _pdot_act_fwd(a, w):
    return _pdot_fwd(a, w, None)[0], w


def _pdot_act_bwd(w, g):
    return _pdot_dgrad(w, g), jnp.zeros_like(w)


pdot_act.defvjp(_pdot_act_fwd, _pdot_act_bwd)


def _bd_mm(x, w, name):
    t, c = x.shape
    nb, bd, _ = w.shape
    tm = _tile(t, 1088, 16)

    def body(x_ref, w_ref, o_ref):
        o_ref[...] = jnp.dot(x_ref[...].astype(BF16), w_ref[0].astype(BF16), preferred_element_type=F32)

    return pl.pallas_call(
        body, name=name, grid=(t // tm, nb),
        in_specs=[pl.BlockSpec((tm, bd), lambda i, k: (i, k)), pl.BlockSpec((1, bd, bd), lambda i, k: (k, 0, 0))],
        out_specs=pl.BlockSpec((tm, bd), lambda i, k: (i, k)),
        out_shape=jax.ShapeDtypeStruct((t, c), F32),
        compiler_params=_cparams(("parallel", "parallel")),
    )(x, w)


def _bd_wgrad(x, g, nb, name):
    t, c = x.shape
    bd = c // nb
    tk = _tile(t, 1088, 16)
    nt = t // tk

    def body(x_ref, g_ref, o_ref):
        @pl.when(pl.program_id(1) == 0)
        def _():
            o_ref[...] = jnp.zeros_like(o_ref)

        xt = x_ref[...].T.astype(BF16)
        o_ref[0] += jnp.dot(xt, g_ref[...].astype(BF16), preferred_element_type=F32)

    return pl.pallas_call(
        body, name=name, grid=(nb, nt),
        in_specs=[pl.BlockSpec((tk, bd), lambda k, i: (i, k)), pl.BlockSpec((tk, bd), lambda k, i: (i, k))],
        out_specs=pl.BlockSpec((1, bd, bd), lambda k, i: (k, 0, 0)),
        out_shape=jax.ShapeDtypeStruct((nb, bd, bd), F32),
        compiler_params=_cparams(("parallel", "arbitrary")),
    )(x, g)


@jax.custom_vjp
def bd_dot(x, w):
    return _bd_mm(x, w, "bd_fwd")


def _bd_fwd(x, w):
    return _bd_mm(x, w, "bd_fwd"), (x, w)


def _bd_bwd(res, g):
    x, w = res
    return _bd_mm(g, jnp.swapaxes(w, 1, 2), "bd_dgrad"), _bd_wgrad(x, g, w.shape[0], "bd_wgrad")


bd_dot.defvjp(_bd_fwd, _bd_bwd)


def _nm_tiles(t, seg_rows):
    tm = _tile(math.gcd(t, seg_rows), 256, SUBLANE)
    return tm, seg_rows // tm


def _nm_fwd_call(x, g, shift, scale, seg_rows):
    t, d = x.shape
    tm, seg_blocks = _nm_tiles(t, seg_rows)
    nseg = shift.shape[0]

    def seg(i):
        return jnp.minimum(i // seg_blocks, nseg - 1)

    def body(x_ref, g_ref, sh_ref, sc_ref, o_ref):
        xv = x_ref[...]
        y = xv * lax.rsqrt(jnp.mean(xv * xv, axis=-1, keepdims=True) + NORM_EPS)
        o_ref[...] = (y * g_ref[...]) * (1.0 + sc_ref[0]) + sh_ref[0]

    return pl.pallas_call(
        body, name="norm_mod_fwd", grid=(t // tm,),
        in_specs=[pl.BlockSpec((tm, d), lambda i: (i, 0)), pl.BlockSpec((1, d), lambda i: (0, 0)),
                  pl.BlockSpec((1, 1, d), lambda i: (seg(i), 0, 0)), pl.BlockSpec((1, 1, d), lambda i: (seg(i), 0, 0))],
        out_specs=pl.BlockSpec((tm, d), lambda i: (i, 0)),
        out_shape=jax.ShapeDtypeStruct((t, d), F32),
        compiler_params=_cparams(("parallel",)),
    )(x, g.reshape(1, d), shift.reshape(nseg, 1, d), scale.reshape(nseg, 1, d))


def _nm_bwd_call(x, g, scale, dy, seg_rows):
    t, d = x.shape
    tm, seg_blocks = _nm_tiles(t, seg_rows)
    nseg = scale.shape[0]

    def seg(i):
        return jnp.minimum(i // seg_blocks, nseg - 1)

    def body(x_ref, g_ref, sc_ref, dy_ref, dx_ref, dg_ref, dsh_ref, dsc_ref):
        i = pl.program_id(0)

        @pl.when(i == 0)
        def _():
            dg_ref[...] = jnp.zeros_like(dg_ref)

        @pl.when((i == 0) | (i == seg_blocks))
        def _():
            dsh_ref[...] = jnp.zeros_like(dsh_ref)
            dsc_ref[...] = jnp.zeros_like(dsc_ref)

        xv = x_ref[...]
        dyv = dy_ref[...]
        gv = g_ref[...]
        rstd = lax.rsqrt(jnp.mean(xv * xv, axis=-1, keepdims=True) + NORM_EPS)
        xhat = xv * rstd
        dsh_ref[0] += jnp.sum(dyv, axis=0, keepdims=True)
        dsc_ref[0] += jnp.sum(dyv * (xhat * gv), axis=0, keepdims=True)
        dn = dyv * (1.0 + sc_ref[0])
        dg_ref[...] += jnp.sum(dn * xhat, axis=0, keepdims=True)
        dxh = dn * gv
        dx_ref[...] = rstd * (dxh - xhat * jnp.mean(dxh * xhat, axis=-1, keepdims=True))

    return pl.pallas_call(
        body, name="norm_mod_bwd", grid=(t // tm,),
        in_specs=[pl.BlockSpec((tm, d), lambda i: (i, 0)), pl.BlockSpec((1, d), lambda i: (0, 0)),
                  pl.BlockSpec((1, 1, d), lambda i: (seg(i), 0, 0)), pl.BlockSpec((tm, d), lambda i: (i, 0))],
        out_specs=[pl.BlockSpec((tm, d), lambda i: (i, 0)), pl.BlockSpec((1, d), lambda i: (0, 0)),
                   pl.BlockSpec((1, 1, d), lambda i: (seg(i), 0, 0)), pl.BlockSpec((1, 1, d), lambda i: (seg(i), 0, 0))],
        out_shape=[jax.ShapeDtypeStruct((t, d), F32), jax.ShapeDtypeStruct((1, d), F32),
                   jax.ShapeDtypeStruct((nseg, 1, d), F32), jax.ShapeDtypeStruct((nseg, 1, d), F32)],
        compiler_params=_cparams(("arbitrary",)),
    )(x, g.reshape(1, d), scale.reshape(nseg, 1, d), dy)


@functools.partial(jax.custom_vjp, nondiff_argnums=(4,))
def norm_mod(x, g, shift, scale, seg_rows):
    return _nm_fwd_call(x, g, shift, scale, seg_rows)


def _norm_mod_fwd(x, g, shift, scale, seg_rows):
    return _nm_fwd_call(x, g, shift, scale, seg_rows), (x, g, scale)


def _norm_mod_bwd(seg_rows, res, dy):
    x, g, scale = res
    dx, dg, dsh, dsc = _nm_bwd_call(x, g, scale, dy, seg_rows)
    return dx, dg.reshape(g.shape), dsh.reshape(scale.shape), dsc.reshape(scale.shape)


norm_mod.defvjp(_norm_mod_fwd, _norm_mod_bwd)


def rmsnorm_p(x, g):
    z = jnp.zeros((1, x.shape[1]), F32)
    return norm_mod(x, g, z, z, x.shape[0])


def _linrec_call(a, b, n_lat, mode, name):
    t, c = a.shape
    tb = _tile(math.gcd(t, n_lat), 512, SUBLANE)
    nb, nl = t // tb, n_lat // tb
    reverse = mode in ("B", "C")
    tiles = tb // SUBLANE

    def block(i):
        if mode == "A":
            return (i + nl) % nb
        if mode == "B":
            return nb - 1 - i
        if mode == "C":
            return jnp.where(i < nl, nl - 1 - i, nb - 1 - (i - nl))
        return i

    def body(a_ref, b_ref, h_ref, carry_ref):
        @pl.when(pl.program_id(0) == 0)
        def _():
            carry_ref[...] = jnp.zeros_like(carry_ref)

        def tile(i, carry):
            r = pl.multiple_of((tiles - 1 - i if reverse else i) * SUBLANE, SUBLANE)
            av = a_ref[pl.ds(r, SUBLANE), :]
            bv = b_ref[pl.ds(r, SUBLANE), :]
            row = lax.broadcasted_iota(jnp.int32, av.shape, 0)
            d = 1
            while d < SUBLANE:
                shift = SUBLANE - d if reverse else d
                a_sh = pltpu.roll(av, shift, 0)
                b_sh = pltpu.roll(bv, shift, 0)
                m = (row < SUBLANE - d) if reverse else (row >= d)
                bv = jnp.where(m, av * b_sh + bv, bv)
                av = jnp.where(m, av * a_sh, av)
                d *= 2
            hv = av * carry + bv
            h_ref[pl.ds(r, SUBLANE), :] = hv
            return jnp.sum(jnp.where(row == (0 if reverse else SUBLANE - 1), hv, 0.0), axis=0, keepdims=True)

        carry_ref[...] = lax.fori_loop(0, tiles, tile, carry_ref[...])

    spec = pl.BlockSpec((tb, c), lambda i: (block(i), 0))
    return pl.pallas_call(
        body, name=name, grid=(nb,), in_specs=[spec, spec], out_specs=spec,
        out_shape=jax.ShapeDtypeStruct((t, c), F32), scratch_shapes=[pltpu.VMEM((1, c), F32)],
        compiler_params=_cparams(("arbitrary",)),
    )(a, b)


@functools.partial(jax.custom_vjp, nondiff_argnums=(2, 3))
def linrec(a, b, n_lat, direction):
    return _linrec_call(a, b, n_lat, "AB"[direction], "linrec_fwd")


def _linrec_fwd(a, b, n_lat, direction):
    h = _linrec_call(a, b, n_lat, "AB"[direction], "linrec_fwd")
    return h, (a, h)


def _linrec_bwd(n_lat, direction, res, dh):
    a, h = res
    zero = jnp.zeros_like(a[:1])
    if direction == 0:
        a_next = jnp.concatenate([a[1:n_lat], zero, a[n_lat + 1:], a[:1]], axis=0)
        h_prev = jnp.concatenate([h[-1:], h[:n_lat - 1], zero, h[n_lat:-1]], axis=0)
    else:
        a_next = jnp.concatenate([zero, a[:-1]], axis=0)
        h_prev = jnp.concatenate([h[1:], zero], axis=0)
    g = _linrec_call(a_next, dh, n_lat, "CD"[direction], "linrec_bwd")
    return g * h_prev, g


linrec.defvjp(_linrec_fwd, _linrec_bwd)


def _attn_fwd_call(q, k, v, scale):
    h, lq, dq = q.shape
    lk, dv = v.shape[1], v.shape[2]
    tq = _tile(lq, 256, SUBLANE)

    def body(q_ref, k_ref, v_ref, o_ref, lse_ref, k_sc, v_sc):
        @pl.when(pl.program_id(1) == 0)
        def _():
            k_sc[...] = k_ref[0].astype(BF16)
            v_sc[...] = v_ref[0].astype(BF16)

        s = lax.dot_general(q_ref[0].astype(BF16), k_sc[...], (((1,), (1,)), ((), ())), preferred_element_type=F32) * scale
        m = jnp.max(s, axis=-1, keepdims=True)
        p = jnp.exp(s - m)
        l = jnp.sum(p, axis=-1, keepdims=True)
        o_ref[0] = jnp.dot(p.astype(BF16), v_sc[...], preferred_element_type=F32) / l
        lse_ref[0] = m + jnp.log(l)

    return pl.pallas_call(
        body, name="attn_fwd", grid=(h, lq // tq),
        in_specs=[pl.BlockSpec((1, tq, dq), lambda a, i: (a, i, 0)), pl.BlockSpec((1, lk, dq), lambda a, i: (a, 0, 0)),
                  pl.BlockSpec((1, lk, dv), lambda a, i: (a, 0, 0))],
        out_specs=[pl.BlockSpec((1, tq, dv), lambda a, i: (a, i, 0)), pl.BlockSpec((1, tq, 1), lambda a, i: (a, i, 0))],
        out_shape=[jax.ShapeDtypeStruct((h, lq, dv), F32), jax.ShapeDtypeStruct((h, lq, 1), F32)],
        scratch_shapes=[pltpu.VMEM((lk, dq), BF16), pltpu.VMEM((lk, dv), BF16)],
        compiler_params=_cparams(("parallel", "arbitrary")),
    )(q, k, v)


def _attn_dq_call(q, k, v, do, lse, delta, scale):
    h, lq, dq = q.shape
    lk, dv = v.shape[1], v.shape[2]
    tq = _tile(lq, 256, SUBLANE)

    def body(q_ref, k_ref, v_ref, do_ref, lse_ref, dl_ref, dq_ref, k_sc, v_sc):
        @pl.when(pl.program_id(1) == 0)
        def _():
            k_sc[...] = k_ref[0].astype(BF16)
            v_sc[...] = v_ref[0].astype(BF16)

        kb = k_sc[...]
        s = lax.dot_general(q_ref[0].astype(BF16), kb, (((1,), (1,)), ((), ())), preferred_element_type=F32) * scale
        p = jnp.exp(s - lse_ref[0])
        dp = lax.dot_general(do_ref[0].astype(BF16), v_sc[...], (((1,), (1,)), ((), ())), preferred_element_type=F32)
        ds = p * (dp - dl_ref[0])
        dq_ref[0] = jnp.dot(ds.astype(BF16), kb, preferred_element_type=F32) * scale

    return pl.pallas_call(
        body, name="attn_dq", grid=(h, lq // tq),
        in_specs=[pl.BlockSpec((1, tq, dq), lambda a, i: (a, i, 0)), pl.BlockSpec((1, lk, dq), lambda a, i: (a, 0, 0)),
                  pl.BlockSpec((1, lk, dv), lambda a, i: (a, 0, 0)), pl.BlockSpec((1, tq, dv), lambda a, i: (a, i, 0)),
                  pl.BlockSpec((1, tq, 1), lambda a, i: (a, i, 0)), pl.BlockSpec((1, tq, 1), lambda a, i: (a, i, 0))],
        out_specs=pl.BlockSpec((1, tq, dq), lambda a, i: (a, i, 0)),
        out_shape=jax.ShapeDtypeStruct((h, lq, dq), F32),
        scratch_shapes=[pltpu.VMEM((lk, dq), BF16), pltpu.VMEM((lk, dv), BF16)],
        compiler_params=_cparams(("parallel", "arbitrary")),
    )(q, k, v, do, lse, delta)


def _attn_dkv_call(q, k, v, do, lse_row, delta_row, scale):
    h, lq, dq = q.shape
    lk, dv = v.shape[1], v.shape[2]
    tk = _tile(lk, 256, SUBLANE)

    def body(q_ref, k_ref, v_ref, do_ref, lse_ref, dl_ref, dk_ref, dv_ref, q_sc, do_sc):
        @pl.when(pl.program_id(1) == 0)
        def _():
            q_sc[...] = q_ref[0].astype(BF16)
            do_sc[...] = do_ref[0].astype(BF16)

        qb = q_sc[...]
        dob = do_sc[...]
        st = lax.dot_general(k_ref[0].astype(BF16), qb, (((1,), (1,)), ((), ())), preferred_element_type=F32) * scale
        pt = jnp.exp(st - lse_ref[0])
        dpt = lax.dot_general(v_ref[0].astype(BF16), dob, (((1,), (1,)), ((), ())), preferred_element_type=F32)
        dst = pt * (dpt - dl_ref[0])
        dk_ref[0] = jnp.dot(dst.astype(BF16), qb, preferred_element_type=F32) * scale
        dv_ref[0] = jnp.dot(pt.astype(BF16), dob, preferred_element_type=F32)

    return pl.pallas_call(
        body, name="attn_dkv", grid=(h, lk // tk),
        in_specs=[pl.BlockSpec((1, lq, dq), lambda a, i: (a, 0, 0)), pl.BlockSpec((1, tk, dq), lambda a, i: (a, i, 0)),
                  pl.BlockSpec((1, tk, dv), lambda a, i: (a, i, 0)), pl.BlockSpec((1, lq, dv), lambda a, i: (a, 0, 0)),
                  pl.BlockSpec((1, 1, lq), lambda a, i: (a, 0, 0)), pl.BlockSpec((1, 1, lq), lambda a, i: (a, 0, 0))],
        out_specs=[pl.BlockSpec((1, tk, dq), lambda a, i: (a, i, 0)), pl.BlockSpec((1, tk, dv), lambda a, i: (a, i, 0))],
        out_shape=[jax.ShapeDtypeStruct((h, lk, dq), F32), jax.ShapeDtypeStruct((h, lk, dv), F32)],
        scratch_shapes=[pltpu.VMEM((lq, dq), BF16), pltpu.VMEM((lq, dv), BF16)],
        compiler_params=_cparams(("parallel", "arbitrary")),
    )(q, k, v, do, lse_row, delta_row)


@jax.custom_vjp
def attention(q, k, v):
    return _attn_fwd_call(q, k, v, q.shape[-1] ** -0.5)[0]


def _attention_fwd(q, k, v):
    o, lse = _attn_fwd_call(q, k, v, q.shape[-1] ** -0.5)
    return o, (q, k, v, o, lse)


def _attention_bwd(res, do):
    q, k, v, o, lse = res
    h, lq, _ = q.shape
    scale = q.shape[-1] ** -0.5
    delta = jnp.sum(do * o, axis=-1, keepdims=True)
    dq = _attn_dq_call(q, k, v, do, lse, delta, scale)
    dk, dv = _attn_dkv_call(q, k, v, do, lse.reshape(h, 1, lq), delta.reshape(h, 1, lq), scale)
    return dq, dk, dv


attention.defvjp(_attention_fwd, _attention_bwd)


def _na_start(r, rows):
    return jnp.clip(r - NA_WIN_H // 2, 0, rows - NA_WIN_H)


def _na_fwd_call(q, k, v, kc, vc, bias):
    n, width = q.shape
    heads, hd = width // NA_HEAD_DIM, NA_HEAD_DIM
    rows, nwin, nctx = n // GRID_W, NA_WIN_H * GRID_W, kc.shape[0]
    scale = hd ** -0.5
    nt = (((1,), (1,)), ((), ()))

    def body(q_ref, k_ref, v_ref, kc_ref, vc_ref, b_ref, o_ref, lse_ref):
        r = pl.program_id(1)
        start = pl.multiple_of(_na_start(r, rows) * GRID_W, GRID_W)
        qb = q_ref[...].astype(BF16)
        kw = k_ref[pl.ds(start, nwin), :].astype(BF16)
        vw = v_ref[pl.ds(start, nwin), :].astype(BF16)
        sw = lax.dot_general(qb, kw, nt, preferred_element_type=F32) * scale + b_ref[0, 0]
        sc = lax.dot_general(qb, kc_ref[...].astype(BF16), nt, preferred_element_type=F32) * scale
        m = jnp.maximum(jnp.max(sw, axis=-1, keepdims=True), jnp.max(sc, axis=-1, keepdims=True))
        pw = jnp.exp(sw - m)
        pc = jnp.exp(sc - m)
        l = jnp.sum(pw, axis=-1, keepdims=True) + jnp.sum(pc, axis=-1, keepdims=True)
        pw = pw / l
        pc = pc / l
        o_ref[...] = (jnp.dot(pw.astype(BF16), vw, preferred_element_type=F32)
                      + jnp.dot(pc.astype(BF16), vc_ref[...].astype(BF16), preferred_element_type=F32))
        lse_ref[0] = m + jnp.log(l)

    full = lambda rws: pl.BlockSpec((rws, hd), lambda a, r: (0, a))
    return pl.pallas_call(
        body, name="natten_fwd", grid=(heads, rows),
        in_specs=[pl.BlockSpec((GRID_W, hd), lambda a, r: (r, a)), full(n), full(n), full(nctx), full(nctx),
                  pl.BlockSpec((1, 1, GRID_W, nwin), lambda a, r: (a, r - _na_start(r, rows), 0, 0))],
        out_specs=[pl.BlockSpec((GRID_W, hd), lambda a, r: (r, a)), pl.BlockSpec((1, GRID_W, 1), lambda a, r: (a, r, 0))],
        out_shape=[jax.ShapeDtypeStruct((n, width), F32), jax.ShapeDtypeStruct((heads, n, 1), F32)],
        compiler_params=_cparams(("parallel", "arbitrary")),
    )(q, k, v, kc, vc, bias)


def _na_bwd_call(q, k, v, kc, vc, bias, bias_t, do, lse, delta, lse_row, delta_row):
    n, width = q.shape
    heads, hd = width // NA_HEAD_DIM, NA_HEAD_DIM
    rows, nwin, nctx = n // GRID_W, NA_WIN_H * GRID_W, kc.shape[0]
    scale = hd ** -0.5
    nt = (((1,), (1,)), ((), ()))

    def body(q_ref, k_ref, v_ref, kc_ref, vc_ref, b_ref, bt_ref, do_ref, lse_ref, dl_ref, lser_ref, dlr_ref,
             dq_ref, dk_ref, dv_ref, dkc_ref, dvc_ref, db_ref):
        r = pl.program_id(1)
        st = _na_start(r, rows)
        start = pl.multiple_of(st * GRID_W, GRID_W)

        @pl.when(r == 0)
        def _():
            dk_ref[...] = jnp.zeros_like(dk_ref)
            dv_ref[...] = jnp.zeros_like(dv_ref)
            dkc_ref[...] = jnp.zeros_like(dkc_ref)
            dvc_ref[...] = jnp.zeros_like(dvc_ref)

        @pl.when((r <= NA_WIN_H // 2) | (r > rows - NA_WIN_H // 2))
        def _():
            db_ref[...] = jnp.zeros_like(db_ref)

        qb = q_ref[...].astype(BF16)
        dob = do_ref[...].astype(BF16)
        kw = k_ref[pl.ds(start, nwin), :].astype(BF16)
        vw = v_ref[pl.ds(start, nwin), :].astype(BF16)
        kcb = kc_ref[...].astype(BF16)
        vcb = vc_ref[...].astype(BF16)
        lse_c, dl_c = lse_ref[0], dl_ref[0]
        pw = jnp.exp(lax.dot_general(qb, kw, nt, preferred_element_type=F32) * scale + b_ref[0, 0] - lse_c)
        pc = jnp.exp(lax.dot_general(qb, kcb, nt, preferred_element_type=F32) * scale - lse_c)
        dsw = pw * (lax.dot_general(dob, vw, nt, preferred_element_type=F32) - dl_c)
        dsc = pc * (lax.dot_general(dob, vcb, nt, preferred_element_type=F32) - dl_c)
        db_ref[0, 0] += dsw
        dq_ref[...] = (jnp.dot(dsw.astype(BF16), kw, preferred_element_type=F32)
                       + jnp.dot(dsc.astype(BF16), kcb, preferred_element_type=F32)) * scale
        lse_r, dl_r = lser_ref[0, 0], dlr_ref[0, 0]
        pwt = jnp.exp(lax.dot_general(kw, qb, nt, preferred_element_type=F32) * scale + bt_ref[0, 0] - lse_r)
        pct = jnp.exp(lax.dot_general(kcb, qb, nt, preferred_element_type=F32) * scale - lse_r)
        dswt = pwt * (lax.dot_general(vw, dob, nt, preferred_element_type=F32) - dl_r)
        dsct = pct * (lax.dot_general(vcb, dob, nt, preferred_element_type=F32) - dl_r)
        dk_ref[pl.ds(start, nwin), :] += jnp.dot(dswt.astype(BF16), qb, preferred_element_type=F32) * scale
        dv_ref[pl.ds(start, nwin), :] += jnp.dot(pwt.astype(BF16), dob, preferred_element_type=F32)
        dkc_ref[...] += jnp.dot(dsct.astype(BF16), qb, preferred_element_type=F32) * scale
        dvc_ref[...] += jnp.dot(pct.astype(BF16), dob, preferred_element_type=F32)

    full = lambda rws: pl.BlockSpec((rws, hd), lambda a, r: (0, a))
    tile = pl.BlockSpec((GRID_W, hd), lambda a, r: (r, a))
    pat = lambda r: r - _na_start(r, rows)
    col = pl.BlockSpec((1, GRID_W, 1), lambda a, r: (a, r, 0))
    rowv = pl.BlockSpec((1, 1, 1, GRID_W), lambda a, r: (a, r, 0, 0))
    return pl.pallas_call(
        body, name="natten_bwd", grid=(heads, rows),
        in_specs=[tile, full(n), full(n), full(nctx), full(nctx),
                  pl.BlockSpec((1, 1, GRID_W, nwin), lambda a, r: (a, pat(r), 0, 0)),
                  pl.BlockSpec((1, 1, nwin, GRID_W), lambda a, r: (a, pat(r), 0, 0)),
                  tile, col, col, rowv, rowv],
        out_specs=[tile, full(n), full(n), full(nctx), full(nctx),
                   pl.BlockSpec((1, 1, GRID_W, nwin), lambda a, r: (a, pat(r), 0, 0))],
        out_shape=[jax.ShapeDtypeStruct((n, width), F32), jax.ShapeDtypeStruct((n, width), F32),
                   jax.ShapeDtypeStruct((n, width), F32), jax.ShapeDtypeStruct((nctx, width), F32),
                   jax.ShapeDtypeStruct((nctx, width), F32), jax.ShapeDtypeStruct(bias.shape, F32)],
        compiler_params=_cparams(("parallel", "arbitrary")),
    )(q, k, v, kc, vc, bias, bias_t, do, lse, delta, lse_row, delta_row)


@jax.custom_vjp
def natten(q, k, v, kc, vc, bias):
    return _na_fwd_call(q, k, v, kc, vc, bias)[0]


def _natten_fwd(q, k, v, kc, vc, bias):
    o, lse = _na_fwd_call(q, k, v, kc, vc, bias)
    return o, (q, k, v, kc, vc, bias, o, lse)


def _natten_bwd(res, do):
    q, k, v, kc, vc, bias, o, lse = res
    n, width = q.shape
    heads, rows = width // NA_HEAD_DIM, n // GRID_W
    delta = jnp.sum((do * o).reshape(n, heads, NA_HEAD_DIM), axis=-1).T.reshape(heads, n, 1)
    return tuple(_na_bwd_call(q, k, v, kc, vc, bias, jnp.swapaxes(bias, 2, 3), do, lse, delta,
                              lse.reshape(heads, rows, 1, GRID_W), delta.reshape(heads, rows, 1, GRID_W)))


natten.defvjp(_natten_fwd, _natten_bwd)


def na_bias_patterns(rpb):
    heads = rpb.shape[0]
    pid = np.arange(NA_WIN_H)[:, None]
    j = np.arange(NA_WIN_H)[None, :]
    row_idx = j - pid + (NA_WIN_H - 1)
    row_hot = (row_idx[..., None] == np.arange(2 * NA_WIN_H - 1)).astype(np.float32)
    cidx = np.arange(GRID_W)
    c_start = np.clip(cidx - NA_WIN_W // 2, 0, GRID_W - NA_WIN_W)
    in_win = (cidx[None, :] >= c_start[:, None]) & (cidx[None, :] < c_start[:, None] + NA_WIN_W)
    col_idx = np.clip(cidx[None, :] - cidx[:, None], -(NA_WIN_W - 1), NA_WIN_W - 1) + (NA_WIN_W - 1)
    col_hot = (col_idx[..., None] == np.arange(2 * NA_WIN_W - 1)).astype(np.float32)
    hi = lax.Precision.HIGHEST
    tmp = jnp.einsum("hab,pja->hpjb", rpb, jnp.asarray(row_hot), precision=hi)
    bias = jnp.einsum("hpjb,qkb->hpqjk", tmp, jnp.asarray(col_hot), precision=hi)
    bias = jnp.where(jnp.asarray(in_win)[None, None, :, None, :], bias, NEG_INF)
    return bias.reshape(heads, NA_WIN_H, GRID_W, NA_WIN_H * GRID_W)


FFN_COLS = 128
FFN_HALO = SUBLANE


def _ffn_chunks(t):
    ch = _tile(t, 544, SUBLANE)
    ext = min(t, ch + 2 * FFN_HALO)
    return [(r0, ch, min(max(r0 - FFN_HALO, 0), t - ext), ext) for r0 in range(0, t, ch)]


def _ffn_conv_parts(x, start, n_lat, t):
    ext = x.shape[0]
    row = start + lax.broadcasted_iota(jnp.int32, x.shape, 0)
    no_prev = (row == 0) | (row == n_lat)
    no_next = (row == n_lat - 1) | (row == t - 1)
    return jnp.where(no_prev, 0.0, pltpu.roll(x, 1, 0)), jnp.where(no_next, 0.0, pltpu.roll(x, ext - 1, 0)), no_prev, no_next


def _sigmoid(x):
    return 1.0 / (1.0 + jnp.exp(-x))


def _ffn_act_fwd_call(uv, ug, cwv, cwg, cbv, cbg, n_lat):
    t, f = uv.shape
    tc = _tile(f, FFN_COLS, LANE)
    chunks = _ffn_chunks(t)

    def body(xv_ref, xg_ref, wv_ref, wg_ref, bv_ref, bg_ref, o_ref):
        def conv(x, start, w_ref, b_ref):
            prev, nxt, _, _ = _ffn_conv_parts(x, start, n_lat, t)
            return prev * w_ref[0:1, :] + x * w_ref[1:2, :] + nxt * w_ref[2:3, :] + b_ref[...]

        for r0, ch, start, ext in chunks:
            cv = conv(xv_ref[start:start + ext, :], start, wv_ref, bv_ref)
            cg = conv(xg_ref[start:start + ext, :], start, wg_ref, bg_ref)
            act = (cg * _sigmoid(cg)) * cv
            o_ref[r0:r0 + ch, :] = act[r0 - start:r0 - start + ch, :]

    col = pl.BlockSpec((t, tc), lambda j: (0, j))
    wsp = pl.BlockSpec((FFN_CONV_W, tc), lambda j: (0, j))
    bsp = pl.BlockSpec((1, tc), lambda j: (0, j))
    return pl.pallas_call(
        body, name="ffn_act_fwd", grid=(f // tc,), in_specs=[col, col, wsp, wsp, bsp, bsp], out_specs=col,
        out_shape=jax.ShapeDtypeStruct((t, f), F32), compiler_params=_cparams(("parallel",)),
    )(uv, ug, cwv, cwg, cbv.reshape(1, f), cbg.reshape(1, f))


def _ffn_act_bwd_call(uv, ug, cwv, cwg, cbv, cbg, dact, n_lat):
    t, f = uv.shape
    tc = _tile(f, FFN_COLS, LANE)
    chunks = _ffn_chunks(t)

    def body(xv_ref, xg_ref, wv_ref, wg_ref, bv_ref, bg_ref, da_ref, dxv_ref, dxg_ref, dw_ref):
        dw_ref[...] = jnp.zeros_like(dw_ref)
        for r0, ch, start, ext in chunks:
            lo = r0 - start
            xv = xv_ref[start:start + ext, :]
            xg = xg_ref[start:start + ext, :]
            da = da_ref[start:start + ext, :]
            pv, nv, no_prev, no_next = _ffn_conv_parts(xv, start, n_lat, t)
            pg, ng, _, _ = _ffn_conv_parts(xg, start, n_lat, t)
            cv = pv * wv_ref[0:1, :] + xv * wv_ref[1:2, :] + nv * wv_ref[2:3, :] + bv_ref[...]
            cg = pg * wg_ref[0:1, :] + xg * wg_ref[1:2, :] + ng * wg_ref[2:3, :] + bg_ref[...]
            sig = _sigmoid(cg)
            dcv = da * (cg * sig)
            dcg = da * cv * (sig * (1.0 + cg * (1.0 - sig)))
            for base, w_ref, dc, parts, dx_ref in ((0, wv_ref, dcv, (pv, xv, nv), dxv_ref), (4, wg_ref, dcg, (pg, xg, ng), dxg_ref)):
                dc_next = jnp.where(no_next, 0.0, pltpu.roll(dc, ext - 1, 0))
                dc_prev = jnp.where(no_prev, 0.0, pltpu.roll(dc, 1, 0))
                dx = dc_next * w_ref[0:1, :] + dc * w_ref[1:2, :] + dc_prev * w_ref[2:3, :]
                dx_ref[r0:r0 + ch, :] = dx[lo:lo + ch, :]
                dci = dc[lo:lo + ch, :]
                for k, part in enumerate(parts):
                    dw_ref[base + k:base + k + 1, :] += jnp.sum(dci * part[lo:lo + ch, :], axis=0, keepdims=True)
                dw_ref[base + 3:base + 4, :] += jnp.sum(dci, axis=0, keepdims=True)

    col = pl.BlockSpec((t, tc), lambda j: (0, j))
    wsp = pl.BlockSpec((FFN_CONV_W, tc), lambda j: (0, j))
    bsp = pl.BlockSpec((1, tc), lambda j: (0, j))
    return pl.pallas_call(
        body, name="ffn_act_bwd", grid=(f // tc,), in_specs=[col, col, wsp, wsp, bsp, bsp, col],
        out_specs=[col, col, pl.BlockSpec((2 * (FFN_CONV_W + 1), tc), lambda j: (0, j))],
        out_shape=[jax.ShapeDtypeStruct((t, f), F32), jax.ShapeDtypeStruct((t, f), F32),
                   jax.ShapeDtypeStruct((2 * (FFN_CONV_W + 1), f), F32)],
        compiler_params=_cparams(("parallel",)),
    )(uv, ug, cwv, cwg, cbv.reshape(1, f), cbg.reshape(1, f), dact)


@functools.partial(jax.custom_vjp, nondiff_argnums=(6,))
def ffn_act(uv, ug, cwv, cwg, cbv, cbg, n_lat):
    return _ffn_act_fwd_call(uv, ug, cwv, cwg, cbv, cbg, n_lat)


def _ffn_act_fwd(uv, ug, cwv, cwg, cbv, cbg, n_lat):
    return _ffn_act_fwd_call(uv, ug, cwv, cwg, cbv, cbg, n_lat), (uv, ug, cwv, cwg, cbv, cbg)


def _ffn_act_bwd(n_lat, res, dact):
    duv, dug, dw = _ffn_act_bwd_call(*res, dact, n_lat)
    nw = FFN_CONV_W
    return duv, dug, dw[:nw], dw[nw + 1:2 * nw + 1], dw[nw], dw[2 * nw + 1]


ffn_act.defvjp(_ffn_act_fwd, _ffn_act_bwd)


def adamw(w, g, m, v):
    shape = w.shape
    last = shape[-1]
    r = int(np.prod(shape[:-1]))
    w2, g2, m2, v2 = (t.reshape(1, r, last) for t in (w, g, m, v))
    tr = _tile(r, max(SUBLANE, (1 << 18) // last), SUBLANE)

    def body(w_ref, g_ref, m_ref, v_ref, d_ref, mo_ref, vo_ref):
        gv = g_ref[...]
        mn = ADAM_B1 * m_ref[...] + (1.0 - ADAM_B1) * gv
        vn = ADAM_B2 * v_ref[...] + (1.0 - ADAM_B2) * (gv * gv)
        m_hat = mn / (1.0 - ADAM_B1 ** ADAM_STEP)
        v_hat = vn / (1.0 - ADAM_B2 ** ADAM_STEP)
        d_ref[...] = -ADAM_LR * (m_hat / (jnp.sqrt(v_hat) + ADAM_EPS) + ADAM_WD * w_ref[...])
        mo_ref[...] = mn
        vo_ref[...] = vn

    spec = pl.BlockSpec((1, tr, last), lambda i: (0, i, 0))
    out = pl.pallas_call(
        body, name="adamw", grid=(r // tr,), in_specs=[spec] * 4, out_specs=[spec] * 3,
        out_shape=[jax.ShapeDtypeStruct((1, r, last), F32)] * 3, compiler_params=_cparams(("parallel",)),
    )(w2, g2, m2, v2)
    return tuple(t.reshape(shape) for t in out)


def _place():
    x, y, c = lax.axis_index("x"), lax.axis_index("y"), lax.axis_index("c")
    chips = [(1 - x, y), (x, 1 - y), (1 - x, 1 - y)]
    return x, y, c, chips


HBM_SPEC = pl.BlockSpec(memory_space=pltpu.HBM)


def all_gather_chips(xs, name):
    def body(x_ref, out_ref, send_sems, recv_sems):
        x, y, c, chips = _place()
        sibling = (x, y, 1 - c)

        def piece(chip, half):
            return out_ref.at[2 * chip[0] + chip[1], half]

        def copy(k, chip, half, to, src=None):
            dst = piece(chip, half)
            return pltpu.make_async_remote_copy(src_ref=dst if src is None else src, dst_ref=dst, send_sem=send_sems.at[k],
                                                recv_sem=recv_sems.at[k], device_id=to, device_id_type=MESH)

        first = [copy(k, (x, y), c, (*chip, c), src=x_ref.at[c]) for k, chip in enumerate(chips)]
        for cp in first:
            cp.start()
        passed = [copy(3 + k, chip, c, sibling) for k, chip in enumerate(chips)]
        for k, chip in enumerate(chips):
            copy(k, chip, c, sibling).wait_recv()
            passed[k].start()
        for k, chip in enumerate(chips):
            copy(3 + k, chip, 1 - c, sibling).wait_recv()
        for cp in first + passed:
            cp.wait_send()

    out = pl.pallas_call(
        body, name=name, in_specs=[HBM_SPEC], out_specs=HBM_SPEC,
        out_shape=jax.ShapeDtypeStruct((N_CHIPS,) + xs.shape, xs.dtype),
        scratch_shapes=[pltpu.SemaphoreType.DMA((6,)), pltpu.SemaphoreType.DMA((6,))],
        compiler_params=pltpu.CompilerParams(has_side_effects=True),
    )(xs)
    me = 2 * lax.axis_index("x") + lax.axis_index("y")
    return lax.dynamic_update_slice(out, xs[None], (me,) + (0,) * xs.ndim)


def _rs_to_sibling(g, name):
    _, _, r, b = g.shape

    def body(g_ref, out_ref, send_sems, recv_sems):
        x, y, c, _ = _place()
        cps = [pltpu.make_async_remote_copy(src_ref=g_ref.at[s, 1 - c], dst_ref=out_ref.at[s], send_sem=send_sems.at[s],
                                            recv_sem=recv_sems.at[s], device_id=(x, y, 1 - c), device_id_type=MESH)
               for s in range(N_CHIPS)]
        for cp in cps:
            cp.start()
        for cp in cps:
            cp.wait()

    return pl.pallas_call(
        body, name=name, in_specs=[HBM_SPEC], out_specs=HBM_SPEC, out_shape=jax.ShapeDtypeStruct((N_CHIPS, r, b), g.dtype),
        scratch_shapes=[pltpu.SemaphoreType.DMA((N_CHIPS,)), pltpu.SemaphoreType.DMA((N_CHIPS,))],
        compiler_params=pltpu.CompilerParams(has_side_effects=True),
    )(g)


def _rs_chip_sum(g, la, c_idx, name):
    _, _, r, b = g.shape
    tr = _tile(r, max(16, (1 << 18) // b), 16)

    def body(c_ref, g_ref, la_ref, o_ref):
        o_ref[...] = (g_ref[:, 0] + la_ref[...]).astype(BF16)

    return pl.pallas_call(
        body, name=name,
        grid_spec=pltpu.PrefetchScalarGridSpec(
            num_scalar_prefetch=1, grid=(N_CHIPS, r // tr),
            in_specs=[pl.BlockSpec((1, 1, tr, b), lambda s, i, c_ref: (s, c_ref[0], i, 0)),
                      pl.BlockSpec((1, tr, b), lambda s, i, c_ref: (s, i, 0))],
            out_specs=pl.BlockSpec((1, tr, b), lambda s, i, c_ref: (s, i, 0))),
        out_shape=jax.ShapeDtypeStruct((N_CHIPS, r, b), BF16), compiler_params=_cparams(("parallel", "parallel")),
    )(c_idx, g, la)


def _rs_to_owners(hb, name):
    _, r, b = hb.shape

    def body(h_ref, out_ref, send_sems, recv_sems):
        x, y, c, chips = _place()
        cps = [pltpu.make_async_remote_copy(src_ref=h_ref.at[2 * chip[0] + chip[1]], dst_ref=out_ref.at[k],
                                            send_sem=send_sems.at[k], recv_sem=recv_sems.at[k], device_id=(*chip, c),
                                            device_id_type=MESH) for k, chip in enumerate(chips)]
        for cp in cps:
            cp.start()
        for cp in cps:
            cp.wait()

    return pl.pallas_call(
        body, name=name, in_specs=[HBM_SPEC], out_specs=HBM_SPEC, out_shape=jax.ShapeDtypeStruct((3, r, b), hb.dtype),
        scratch_shapes=[pltpu.SemaphoreType.DMA((3,)), pltpu.SemaphoreType.DMA((3,))],
        compiler_params=pltpu.CompilerParams(has_side_effects=True),
    )(hb)


def _rs_final_sum(g, la, lb, sc_idx, name):
    _, _, r, b = g.shape
    tr = _tile(r, max(16, (1 << 18) // b), 16)

    def body(i_ref, g_ref, la_ref, lb_ref, o_ref):
        o_ref[...] = (g_ref[0, 0] + la_ref[0]) + lb_ref[0].astype(F32) + lb_ref[1].astype(F32) + lb_ref[2].astype(F32)

    return pl.pallas_call(
        body, name=name,
        grid_spec=pltpu.PrefetchScalarGridSpec(
            num_scalar_prefetch=1, grid=(r // tr,),
            in_specs=[pl.BlockSpec((1, 1, tr, b), lambda i, i_ref: (i_ref[0], i_ref[1], i, 0)),
                      pl.BlockSpec((1, tr, b), lambda i, i_ref: (i_ref[0], i, 0)),
                      pl.BlockSpec((3, tr, b), lambda i, i_ref: (0, i, 0))],
            out_specs=pl.BlockSpec((tr, b), lambda i, i_ref: (i, 0))),
        out_shape=jax.ShapeDtypeStruct((r, b), F32), compiler_params=_cparams(("parallel",)),
    )(sc_idx, g, la, lb)


def _rs_join_halves(f, name):
    r, b = f.shape

    def body(f_ref, out_ref, send_sem, recv_sem):
        x, y, c, _ = _place()
        cp = pltpu.make_async_remote_copy(src_ref=f_ref, dst_ref=out_ref.at[c], send_sem=send_sem, recv_sem=recv_sem,
                                          device_id=(x, y, 1 - c), device_id_type=MESH)
        cp.start()
        pltpu.make_async_remote_copy(src_ref=f_ref, dst_ref=out_ref.at[1 - c], send_sem=send_sem, recv_sem=recv_sem,
                                     device_id=(x, y, 1 - c), device_id_type=MESH).wait_recv()
        cp.wait_send()

    out = pl.pallas_call(
        body, name=name, in_specs=[HBM_SPEC], out_specs=HBM_SPEC, out_shape=jax.ShapeDtypeStruct((2, r, b), f.dtype),
        scratch_shapes=[pltpu.SemaphoreType.DMA, pltpu.SemaphoreType.DMA],
        compiler_params=pltpu.CompilerParams(has_side_effects=True),
    )(f)
    return lax.dynamic_update_slice(out, f[None], (lax.axis_index("c"), 0, 0))


def reduce_scatter_devices(g, tag):
    c = lax.axis_index("c")
    me = 2 * lax.axis_index("x") + lax.axis_index("y")
    la = _rs_to_sibling(g, "rs_sibling_" + tag)
    hb = _rs_chip_sum(g, la, jnp.stack([c]).astype(jnp.int32), "rs_chipsum_" + tag)
    lb = _rs_to_owners(hb, "rs_owners_" + tag)
    f = _rs_final_sum(g, la, lb, jnp.stack([me, c]).astype(jnp.int32), "rs_final_" + tag)
    return _rs_join_halves(f, "rs_join_" + tag)


def _w_in_layout(d_model, na_width, lru_width, q_rank, kv_rank):
    head = 3 * na_width + 2 * lru_width + q_rank + kv_rank
    total = head + MLA_ROPE_DIM + N_BRANCH * d_model
    padded = -(-total // LANE) * LANE
    return None, head, total, padded


def _col_pieces(shards, a, b):
    n4 = shards[0].shape[1]
    out = []
    for s, t in enumerate(shards):
        lo, hi = max(a, s * n4), min(b, (s + 1) * n4)
        if lo < hi:
            out.append(t[:, lo - s * n4:hi - s * n4])
    return out


def _assemble_w_in(shards, head, total, padded):
    parts = (_col_pieces(shards, 0, head) + _col_pieces(shards, head + MLA_ROPE_DIM, total)
             + _col_pieces(shards, head, head + MLA_ROPE_DIM))
    if padded > total:
        parts.append(jnp.zeros((shards[0].shape[0], padded - total), shards[0].dtype))
    return jnp.concatenate(parts, axis=1)


def _w_in_shard_grad(g, s, n4, head, total):
    n_gates = total - head - MLA_ROPE_DIM
    a, b = s * n4, (s + 1) * n4
    parts = []
    for lo, hi, shift in ((0, head, 0), (head, head + MLA_ROPE_DIM, n_gates), (head + MLA_ROPE_DIM, total, -MLA_ROPE_DIM)):
        lo, hi = max(a, lo), min(b, hi)
        if lo < hi:
            parts.append(g[:, lo + shift:hi + shift])
    return jnp.concatenate(parts, axis=1)


@functools.partial(jax.custom_vjp, nondiff_argnums=(1,))
def split_cols(z, sizes):
    out, o = [], 0
    for sz in sizes:
        out.append(z[:, o:o + sz])
        o += sz
    return tuple(out)


def _split_cols_fwd(z, sizes):
    return split_cols(z, sizes), None


def _split_cols_bwd(sizes, res, cts):
    return (jnp.concatenate(cts, axis=1),)


split_cols.defvjp(_split_cols_fwd, _split_cols_bwd)


def _dwconv(x, w, b):
    width = w.shape[0]
    n = x.shape[0]
    left = width // 2
    xp = jnp.pad(x, ((left, width - 1 - left), (0, 0)))
    return sum(xp[i:i + n] * w[i] for i in range(width)) + b


def _dwconv_seg(x, w, b, n_lat):
    return jnp.concatenate([_dwconv(x[:n_lat], w, b), _dwconv(x[n_lat:], w, b)], axis=0)


def _rope(x, cos, sin):
    half = x.shape[-1] // 2
    x1, x2 = x[..., :half], x[..., half:]
    return jnp.concatenate([x1 * cos - x2 * sin, x1 * sin + x2 * cos], axis=-1)


def _rope_tables(n_lat, n_ctx):
    t = jnp.arange(n_lat, dtype=jnp.int32)
    row = (t // GRID_W).astype(F32)
    col = (t % GRID_W).astype(F32)
    n_freq = MLA_ROPE_DIM // 4
    inv_freq = ROPE_THETA ** (-jnp.arange(n_freq, dtype=F32) / n_freq)
    ang = jnp.concatenate([row[:, None] * inv_freq, col[:, None] * inv_freq], axis=-1)
    ones = jnp.ones((n_ctx, MLA_ROPE_DIM // 2), F32)
    return jnp.concatenate([jnp.cos(ang), ones], axis=0), jnp.concatenate([jnp.sin(ang), 0.0 * ones], axis=0)


def _heads_first(t, heads):
    n = t.shape[0]
    return t.reshape(n, heads, -1).transpose(1, 0, 2)


def _heads_last(t):
    return t.transpose(1, 0, 2).reshape(t.shape[1], -1)


def _lru_coeffs(u, w_a, b_a, w_x, b_x, lam):
    r = jax.nn.sigmoid(bd_dot(u, w_a) + b_a)
    i = jax.nn.sigmoid(bd_dot(u, w_x) + b_x)
    log_a = -LRU_C * r * jax.nn.softplus(-lam)
    return jnp.exp(log_a), jnp.sqrt(-jnp.expm1(2.0 * log_a)) * (i * u)


def _forward_loss(big, prox, small, mod_probe, x, ctx, silu_c, target):
    n_lat, d_model = x.shape
    n_ctx = ctx.shape[0]
    na_width = NA_HEADS * NA_HEAD_DIM
    lru_width = small["lru_conv_b"].shape[1]
    q_rank, kv_rank = small["mla_q_norm"].shape[1], small["mla_kv_norm"].shape[1]
    _, head, total, padded = _w_in_layout(d_model, na_width, lru_width, q_rank, kv_rank)
    cos, sin = _rope_tables(n_lat, n_ctx)
    is_ctx = (jnp.arange(n_lat + n_ctx) >= n_lat)[:, None]

    def by_row(v2):
        return jnp.where(is_ctx, v2[1][None, :], v2[0][None, :])

    s_rows = jnp.concatenate([silu_c, jax.nn.silu(small["c_ctx"])[None, :],
                              jnp.zeros((MOD_ROWS - 2, d_model), F32)], axis=0)
    xs = jnp.concatenate([x, ctx], axis=0)
    for l in range(DEPTH):
        mod = (pdot_act(s_rows, big["w_mod"][l]) + small["b_mod"][l])[:2] + mod_probe[l]
        sh1, sc1, g1, sh2, sc2, g2 = split_cols(mod, (d_model,) * N_MOD)
        h = norm_mod(xs, small["norm_mix"][l], sh1, sc1, n_lat)
        z = pdot(h, big["w_in"][l], prox["w_in"][l])
        names = ("q_a", "k_a", "v_a", "lru_x", "lru_g", "cq", "ckv", "gate_a", "gate_b", "gate_c", "kr", "pad")
        widths = (na_width,) * 3 + (lru_width,) * 2 + (q_rank, kv_rank) + (d_model,) * 3 + (MLA_ROPE_DIM, padded - total)
        cols = dict(zip(names, split_cols(z, widths)))
        q_a, k_a, v_a = cols["q_a"], cols["k_a"], cols["v_a"]
        out_a_lat = natten(q_a[:n_lat], k_a[:n_lat], v_a[:n_lat], k_a[n_lat:], v_a[n_lat:], na_bias_patterns(small["na_rpb"][l]))
        out_a_ctx = _heads_last(attention(*(_heads_first(t[n_lat:], NA_HEADS) for t in (q_a, k_a, v_a))))
        out_a = jnp.concatenate([out_a_lat, out_a_ctx], axis=0)
        u = _dwconv_seg(cols["lru_x"], small["lru_conv_w"][l], small["lru_conv_b"][l], n_lat)
        hs = [linrec(*_lru_coeffs(u, small["lru_w_a"][l, d], small["lru_b_a"][l, d], small["lru_w_x"][l, d],
                                  small["lru_b_x"][l, d], small["lru_lam"][l, d]), n_lat, d) for d in range(2)]
        y_b = hs[0] + hs[1]
        out_b = jax.nn.gelu(cols["lru_g"]) * y_b
        q_m = pdot(rmsnorm_p(cols["cq"], small["mla_q_norm"][l]), big["mla_w_q_up"][l], prox["mla_w_q_up"][l])
        q_m = q_m.reshape(-1, MLA_HEADS, MLA_NOPE_DIM + MLA_ROPE_DIM)
        q_m = jnp.concatenate([q_m[..., :MLA_NOPE_DIM], _rope(q_m[..., MLA_NOPE_DIM:], cos[:, None, :], sin[:, None, :])], axis=-1)
        kv = pdot(rmsnorm_p(cols["ckv"], small["mla_kv_norm"][l]), big["mla_w_kv_up"][l], prox["mla_w_kv_up"][l])
        kv = kv.reshape(-1, MLA_HEADS, MLA_NOPE_DIM + MLA_V_DIM)
        k_rope = jnp.broadcast_to(_rope(cols["kr"], cos, sin)[:, None, :], (n_lat + n_ctx, MLA_HEADS, MLA_ROPE_DIM))
        k_m = jnp.concatenate([kv[..., :MLA_NOPE_DIM], k_rope], axis=-1).transpose(1, 0, 2)
        v_m = kv[..., MLA_NOPE_DIM:].transpose(1, 0, 2)
        q_m = q_m.transpose(1, 0, 2)
        out_c = jnp.concatenate([_heads_last(attention(q_m[:, :n_lat], k_m, v_m)),
                                 _heads_last(attention(q_m[:, n_lat:], k_m[:, n_lat:], v_m[:, n_lat:]))], axis=0)
        y = sum(jax.nn.sigmoid(cols[gate]) * pdot(br, big["w_branch"][l][i], prox["w_branch"][l][i])
                for i, (gate, br) in enumerate(zip(("gate_a", "gate_b", "gate_c"), (out_a, out_b, out_c))))
        xs = xs + by_row(g1) * pdot(y, big["w_out"][l], prox["w_out"][l])
        h2 = norm_mod(xs, small["norm_ffn"][l], sh2, sc2, n_lat)
        d_ff = big["ffn_w_down"][l].shape[0]
        halves = (d_ff, d_ff)
        (w_val, w_gate), (p_val, p_gate) = big["ffn_w_up"][l], prox["ffn_w_up"][l]
        cw_val, cw_gate = split_cols(small["ffn_conv_w"][l], halves)
        cb_val, cb_gate = split_cols(small["ffn_conv_b"][l][None, :], halves)
        act = ffn_act(pdot(h2, w_val, p_val), pdot(h2, w_gate, p_gate), cw_val, cw_gate, cb_val[0], cb_gate[0], n_lat)
        xs = xs + by_row(g2) * pdot(act, big["ffn_w_down"][l], prox["ffn_w_down"][l])
    y_out = rmsnorm_p(xs[:n_lat], small["norm_final"])
    return 0.5 * jnp.sum(jnp.mean(jnp.square(y_out - target), axis=-1))


WEIGHTS = ['c_ctx', 'w_mod', 'b_mod', 'norm_mix', 'norm_ffn', 'w_in', 'na_rpb', 'lru_conv_w', 'lru_conv_b', 'lru_w_a',
           'lru_b_a', 'lru_w_x', 'lru_b_x', 'lru_lam', 'mla_q_norm', 'mla_kv_norm', 'mla_w_q_up', 'mla_w_kv_up', 'w_branch',
           'w_out', 'ffn_w_up', 'ffn_conv_w', 'ffn_conv_b', 'ffn_w_down', 'norm_final']
BIG = {'w_mod': 2, 'w_in': 2, 'mla_w_q_up': 2, 'mla_w_kv_up': 2, 'w_branch': 3, 'w_out': 1, 'ffn_w_up': 2, 'ffn_w_down': 1}
SMALL_SHARDED = {'lru_conv_w': 2, 'lru_b_a': 2, 'lru_b_x': 2, 'lru_lam': 2, 'ffn_conv_w': 2}
PACK_LANES = 128
PACK_ROWS = 16


def _gathered_to_full(g, ax):
    t = jnp.moveaxis(g, 0, ax)
    return t.reshape(t.shape[:ax] + (t.shape[ax] * t.shape[ax + 1],) + t.shape[ax + 2:])


def _canon(shape):
    return (2, int(np.prod(shape[:-1])) // 2, shape[-1])


def _pack(parts, multiple):
    flat = jnp.concatenate([p.reshape(-1) for p in parts])
    n = flat.shape[0]
    padded = -(-n // multiple) * multiple
    return jnp.pad(flat, (0, padded - n))


def _unpack(flat, shapes):
    out, o = [], 0
    for s in shapes:
        n = int(np.prod(s))
        out.append(flat[o:o + n].reshape(s))
        o += n
    return out


def kernel(x, c, ctx, c_ctx, w_mod, b_mod, norm_mix, norm_ffn, w_in, na_rpb, lru_conv_w, lru_conv_b, lru_w_a, lru_b_a, lru_w_x, lru_b_x, lru_lam, mla_q_norm, mla_kv_norm, mla_w_q_up, mla_w_kv_up, w_branch, w_out, ffn_w_up, ffn_conv_w, ffn_conv_b, ffn_w_down, norm_final, loss_target, m_c_ctx, m_w_mod, m_b_mod, m_norm_mix, m_norm_ffn, m_w_in, m_na_rpb, m_lru_conv_w, m_lru_conv_b, m_lru_w_a, m_lru_b_a, m_lru_w_x, m_lru_b_x, m_lru_lam, m_mla_q_norm, m_mla_kv_norm, m_mla_w_q_up, m_mla_w_kv_up, m_w_branch, m_w_out, m_ffn_w_up, m_ffn_conv_w, m_ffn_conv_b, m_ffn_w_down, m_norm_final, v_c_ctx, v_w_mod, v_b_mod, v_norm_mix, v_norm_ffn, v_w_in, v_na_rpb, v_lru_conv_w, v_lru_conv_b, v_lru_w_a, v_lru_b_a, v_lru_w_x, v_lru_b_x, v_lru_lam, v_mla_q_norm, v_mla_kv_norm, v_mla_w_q_up, v_mla_w_kv_up, v_w_branch, v_w_out, v_ffn_w_up, v_ffn_conv_w, v_ffn_conv_b, v_ffn_w_down, v_norm_final):
    args = dict(locals())
    w = {n: args[n] for n in WEIGHTS}
    m = {n: args["m_" + n] for n in WEIGHTS}
    v = {n: args["v_" + n] for n in WEIGHTS}
    me = 2 * lax.axis_index("x") + lax.axis_index("y")

    d_model = x.shape[-1]
    _, head, total, padded = _w_in_layout(d_model, NA_HEADS * NA_HEAD_DIM, lru_conv_b.shape[1], mla_q_norm.shape[1], mla_kv_norm.shape[1])
    full, big = {}, {}
    for name, ax in BIG.items():
        shard = w[name]
        g = all_gather_chips(shard.astype(BF16).reshape(_canon(shard.shape)), "ag_" + name).reshape((N_CHIPS,) + shard.shape)
        per_layer = [[g[s, l] for s in range(N_CHIPS)] for l in range(DEPTH)]
        if name == "w_in":
            big[name] = [_assemble_w_in(t, head, total, padded) for t in per_layer]
        elif name == "ffn_w_up":
            big[name] = [tuple(jnp.concatenate(t[i:i + N_CHIPS // 2], axis=1) for i in (0, N_CHIPS // 2)) for t in per_layer]
        elif name == "w_branch":
            big[name] = [[jnp.concatenate([sh[i] for sh in t], axis=1) for i in range(N_BRANCH)] for t in per_layer]
        else:
            big[name] = [jnp.concatenate(t, axis=ax - 1) for t in per_layer]
    packed = _pack([w[n] for n in SMALL_SHARDED], 2 * PACK_ROWS * PACK_LANES).reshape(2, -1, PACK_LANES)
    g = all_gather_chips(packed, "ag_small").reshape(N_CHIPS, -1)
    for name, t in zip(SMALL_SHARDED, zip(*[_unpack(g[s], [w[n].shape for n in SMALL_SHARDED]) for s in range(N_CHIPS)])):
        full[name] = _gathered_to_full(jnp.stack(t), SMALL_SHARDED[name])
    small = {n: full.get(n, w[n]) for n in WEIGHTS if n not in BIG}

    prox = jax.tree.map(lambda t: jnp.zeros(t.shape, F32), {n: t for n, t in big.items() if n != "w_mod"})
    silu_c = jax.nn.silu(c)
    probe = jnp.zeros((DEPTH, 2, N_MOD * d_model), F32)
    loss, (g_prox, g_small, g_mod, g_x) = jax.value_and_grad(_forward_loss, argnums=(1, 2, 3, 4))(
        big, prox, small, probe, x[0], ctx[0], silu_c, loss_target[0])
    loss = lax.psum(loss, ("x", "y", "c"))

    def shard_grad(name, l, s):
        g, n4 = g_prox[name][l], w[name].shape[BIG[name]]
        if name == "w_in":
            return _w_in_shard_grad(g, s, n4, head, total)
        if name == "ffn_w_up":
            half = N_CHIPS // 2
            return lax.slice_in_dim(g[s // half], (s % half) * n4, (s % half + 1) * n4, axis=1)
        if name == "w_branch":
            return jnp.stack([t[:, s * n4:(s + 1) * n4] for t in g])
        return lax.slice_in_dim(g, s * n4, (s + 1) * n4, axis=BIG[name] - 1)

    grads = {}
    for name in BIG:
        if name == "w_mod":
            continue
        stacked = jnp.stack([jnp.stack([shard_grad(name, l, s) for l in range(DEPTH)]) for s in range(N_CHIPS)])
        red = reduce_scatter_devices(stacked.reshape((N_CHIPS,) + _canon(w[name].shape)), name)
        grads[name] = red.reshape(w[name].shape)

    mine = _pack([silu_c, jax.nn.silu(c_ctx), g_mod], PACK_ROWS * PACK_LANES).reshape(-1, PACK_LANES)
    rows = all_gather_chips(_rs_join_halves(mine, "ag_mod_pair"), "ag_mod").reshape(2 * N_CHIPS, -1)
    u_all = jnp.pad(rows[:, :2 * d_model].reshape(-1, d_model), ((0, MOD_ROWS - 4 * N_CHIPS), (0, 0)))
    dm_all = rows[:, 2 * d_model:2 * d_model + g_mod.size].reshape(2 * N_CHIPS, DEPTH, 2, N_MOD * d_model)
    n4 = w_mod.shape[2]
    grads["w_mod"] = jnp.stack([
        _mm_acc(u_all, jnp.pad(lax.dynamic_slice_in_dim(dm_all[:, l].reshape(4 * N_CHIPS, -1), me * n4, n4, axis=1),
                               ((0, MOD_ROWS - 4 * N_CHIPS), (0, 0))), ta=True, name="mm_wgrad_mod") for l in range(DEPTH)])
    small_names = [n for n in WEIGHTS if n not in BIG]
    packed = _pack([g_small[n] for n in small_names], N_CHIPS * 2 * PACK_ROWS * PACK_LANES).reshape(N_CHIPS, 2, -1, PACK_LANES)
    red = all_gather_chips(reduce_scatter_devices(packed, "small"), "ag_small_grads").reshape(-1)
    for name, t in zip(small_names, _unpack(red, [g_small[n].shape for n in small_names])):
        if name in SMALL_SHARDED:
            ax = SMALL_SHARDED[name]
            t = lax.dynamic_slice_in_dim(t, me * w[name].shape[ax], w[name].shape[ax], axis=ax)
        grads[name] = t

    upd = {n: adamw(w[n], grads[n], m[n], v[n]) for n in WEIGHTS}
    return (loss, g_x[None], *[grads[n] for n in WEIGHTS], *[upd[n][0] for n in WEIGHTS],
            *[upd[n][1] for n in WEIGHTS], *[upd[n][2] for n in WEIGHTS])
```

```python
import functools
import math

import numpy as np
import jax
import jax.numpy as jnp
from jax import lax
from jax.experimental import pallas as pl
from jax.experimental.pallas import tpu as pltpu

F32 = jnp.float32
BF16 = jnp.bfloat16
MESH = pl.DeviceIdType.MESH

DEPTH = 4
GRID_W = 64
NORM_EPS = 1e-6
NEG_INF = -1e30
N_MOD = 6
NA_HEADS = 8
NA_HEAD_DIM = 128
NA_WIN_H = 8
NA_WIN_W = 16
LRU_BLOCKS = 8
LRU_CONV_W = 4
LRU_C = 8.0
MLA_HEADS = 8
MLA_NOPE_DIM = 128
MLA_ROPE_DIM = 64
MLA_V_DIM = 128
ROPE_THETA = 10000.0
N_BRANCH = 3
FFN_CONV_W = 3
ADAM_LR = 0.001
ADAM_B1 = 0.9
ADAM_B2 = 0.999
ADAM_EPS = 1e-08
ADAM_WD = 0.01
ADAM_STEP = 10

N_CHIPS = 4
LANE = 128
SUBLANE = 8
VMEM_LIMIT = 48 * 1024 * 1024
MOD_ROWS = 128


def _cparams(sem=None, **kw):
    if sem is not None:
        kw["dimension_semantics"] = sem
    return pltpu.CompilerParams(vmem_limit_bytes=VMEM_LIMIT, **kw)


def _tile(d, cap, align):
    best = None
    t = align
    while t <= min(d, cap):
        if d % t == 0:
            best = t
        t += align
    return d if best is None else best


FULL_K_MAX = 2304


def _mm_rows(a, b, *, tb, emit, name):
    m, k = a.shape
    n = b.shape[0] if tb else b.shape[1]
    tm = _tile(m, 544, 16)
    tn = _tile(n, 640, LANE)
    dims = (((1,), (1 if tb else 0,)), ((), ()))

    def body(a_ref, b_ref, o_ref, *rest):
        a_sc = rest[-1]

        @pl.when(pl.program_id(1) == 0)
        def _():
            a_sc[...] = a_ref[...].astype(BF16)
            if emit:
                rest[0][...] = a_sc[...]

        o_ref[...] = lax.dot_general(a_sc[...], b_ref[...].astype(BF16), dims, preferred_element_type=F32)

    b_spec = pl.BlockSpec((tn, k), lambda i, j: (j, 0)) if tb else pl.BlockSpec((k, tn), lambda i, j: (0, j))
    out_specs = [pl.BlockSpec((tm, tn), lambda i, j: (i, j))]
    out_shape = [jax.ShapeDtypeStruct((m, n), F32)]
    if emit:
        out_specs.append(pl.BlockSpec((tm, k), lambda i, j: (i, 0)))
        out_shape.append(jax.ShapeDtypeStruct((m, k), BF16))
    out = pl.pallas_call(
        body, name=name, grid=(m // tm, n // tn),
        in_specs=[pl.BlockSpec((tm, k), lambda i, j: (i, 0)), b_spec], out_specs=out_specs, out_shape=out_shape,
        scratch_shapes=[pltpu.VMEM((tm, k), BF16)], compiler_params=_cparams(("parallel", "arbitrary")),
    )(a, b)
    return out if emit else out[0]


def _mm_acc(a, b, *, ta=False, tb=False, name):
    m, k = (a.shape[1], a.shape[0]) if ta else a.shape
    n = b.shape[0] if tb else b.shape[1]
    tm = _tile(m, 2048 if ta else 1088, 16)
    tn = _tile(n, 640 if ta else 2048, LANE)
    tk = _tile(k, 1088, 16) if ta else _tile(k, 640, LANE)
    dims = (((0 if ta else 1,), (1 if tb else 0,)), ((), ()))

    def body(a_ref, b_ref, o_ref):
        @pl.when(pl.program_id(2) == 0)
        def _():
            o_ref[...] = jnp.zeros_like(o_ref)

        o_ref[...] += lax.dot_general(a_ref[...].astype(BF16), b_ref[...].astype(BF16), dims, preferred_element_type=F32)

    a_spec = pl.BlockSpec((tk, tm), lambda i, j, kk: (kk, i)) if ta else pl.BlockSpec((tm, tk), lambda i, j, kk: (i, kk))
    b_spec = pl.BlockSpec((tn, tk), lambda i, j, kk: (j, kk)) if tb else pl.BlockSpec((tk, tn), lambda i, j, kk: (kk, j))
    return pl.pallas_call(
        body, name=name, grid=(m // tm, n // tn, k // tk),
        in_specs=[a_spec, b_spec], out_specs=pl.BlockSpec((tm, tn), lambda i, j, kk: (i, j)),
        out_shape=jax.ShapeDtypeStruct((m, n), F32),
        compiler_params=_cparams(("parallel", "parallel", "arbitrary")),
    )(a, b)


def _pdot_fwd(a, w, wp):
    if a.shape[1] <= FULL_K_MAX:
        out, a_bf = _mm_rows(a, w, tb=False, emit=True, name="mm_fwd")
        return out, (a_bf, w)
    return _mm_acc(a, w, name="mm_fwd_acc"), (a, w)


@jax.custom_vjp
def pdot(a, w, wp):
    return _pdot_fwd(a, w, wp)[0]


def _pdot_dgrad(w, g):
    if w.shape[1] <= FULL_K_MAX:
        return _mm_rows(g, w, tb=True, emit=False, name="mm_dgrad")
    return _mm_acc(g, w, tb=True, name="mm_dgrad_acc")


def _pdot_bwd(res, g):
    a, w = res
    return _pdot_dgrad(w, g), jnp.zeros_like(w), _mm_acc(a, g, ta=True, name="mm_wgrad")


pdot.defvjp(_pdot_fwd, _pdot_bwd)


@jax.custom_vjp
def pdot_act(a, w):
    return _pdot_fwd(a, w, None)[0]


def _pdot_act_fwd(a, w):
    return _pdot_fwd(a, w, None)[0], w


def _pdot_act_bwd(w, g):
    return _pdot_dgrad(w, g), jnp.zeros_like(w)


pdot_act.defvjp(_pdot_act_fwd, _pdot_act_bwd)


def _bd_mm(x, w, name):
    t, c = x.shape
    nb, bd, _ = w.shape
    tm = _tile(t, 1088, 16)

    def body(x_ref, w_ref, o_ref):
        o_ref[...] = jnp.dot(x_ref[...].astype(BF16), w_ref[0].astype(BF16), preferred_element_type=F32)

    return pl.pallas_call(
        body, name=name, grid=(t // tm, nb),
        in_specs=[pl.BlockSpec((tm, bd), lambda i, k: (i, k)), pl.BlockSpec((1, bd, bd), lambda i, k: (k, 0, 0))],
        out_specs=pl.BlockSpec((tm, bd), lambda i, k: (i, k)),
        out_shape=jax.ShapeDtypeStruct((t, c), F32),
        compiler_params=_cparams(("parallel", "parallel")),
    )(x, w)


def _bd_wgrad(x, g, nb, name):
    t, c = x.shape
    bd = c // nb
    tk = _tile(t, 1088, 16)
    nt = t // tk

    def body(x_ref, g_ref, o_ref):
        @pl.when(pl.program_id(1) == 0)
        def _():
            o_ref[...] = jnp.zeros_like(o_ref)

        xt = x_ref[...].T.astype(BF16)
        o_ref[0] += jnp.dot(xt, g_ref[...].astype(BF16), preferred_element_type=F32)

    return pl.pallas_call(
        body, name=name, grid=(nb, nt),
        in_specs=[pl.BlockSpec((tk, bd), lambda k, i: (i, k)), pl.BlockSpec((tk, bd), lambda k, i: (i, k))],
        out_specs=pl.BlockSpec((1, bd, bd), lambda k, i: (k, 0, 0)),
        out_shape=jax.ShapeDtypeStruct((nb, bd, bd), F32),
        compiler_params=_cparams(("parallel", "arbitrary")),
    )(x, g)


@jax.custom_vjp
def bd_dot(x, w):
    return _bd_mm(x, w, "bd_fwd")


def _bd_fwd(x, w):
    return _bd_mm(x, w, "bd_fwd"), (x, w)


def _bd_bwd(res, g):
    x, w = res
    return _bd_mm(g, jnp.swapaxes(w, 1, 2), "bd_dgrad"), _bd_wgrad(x, g, w.shape[0], "bd_wgrad")


bd_dot.defvjp(_bd_fwd, _bd_bwd)


def _nm_tiles(t, seg_rows):
    tm = _tile(math.gcd(t, seg_rows), 256, SUBLANE)
    return tm, seg_rows // tm


def _nm_fwd_call(x, g, shift, scale, seg_rows):
    t, d = x.shape
    tm, seg_blocks = _nm_tiles(t, seg_rows)
    nseg = shift.shape[0]

    def seg(i):
        return jnp.minimum(i // seg_blocks, nseg - 1)

    def body(x_ref, g_ref, sh_ref, sc_ref, o_ref):
        xv = x_ref[...]
        y = xv * lax.rsqrt(jnp.mean(xv * xv, axis=-1, keepdims=True) + NORM_EPS)
        o_ref[...] = (y * g_ref[...]) * (1.0 + sc_ref[0]) + sh_ref[0]

    return pl.pallas_call(
        body, name="norm_mod_fwd", grid=(t // tm,),
        in_specs=[pl.BlockSpec((tm, d), lambda i: (i, 0)), pl.BlockSpec((1, d), lambda i: (0, 0)),
                  pl.BlockSpec((1, 1, d), lambda i: (seg(i), 0, 0)), pl.BlockSpec((1, 1, d), lambda i: (seg(i), 0, 0))],
        out_specs=pl.BlockSpec((tm, d), lambda i: (i, 0)),
        out_shape=jax.ShapeDtypeStruct((t, d), F32),
        compiler_params=_cparams(("parallel",)),
    )(x, g.reshape(1, d), shift.reshape(nseg, 1, d), scale.reshape(nseg, 1, d))


def _nm_bwd_call(x, g, scale, dy, seg_rows):
    t, d = x.shape
    tm, seg_blocks = _nm_tiles(t, seg_rows)
    nseg = scale.shape[0]

    def seg(i):
        return jnp.minimum(i // seg_blocks, nseg - 1)

    def body(x_ref, g_ref, sc_ref, dy_ref, dx_ref, dg_ref, dsh_ref, dsc_ref):
        i = pl.program_id(0)

        @pl.when(i == 0)
        def _():
            dg_ref[...] = jnp.zeros_like(dg_ref)

        @pl.when((i == 0) | (i == seg_blocks))
        def _():
            dsh_ref[...] = jnp.zeros_like(dsh_ref)
            dsc_ref[...] = jnp.zeros_like(dsc_ref)

        xv = x_ref[...]
        dyv = dy_ref[...]
        gv = g_ref[...]
        rstd = lax.rsqrt(jnp.mean(xv * xv, axis=-1, keepdims=True) + NORM_EPS)
        xhat = xv * rstd
        dsh_ref[0] += jnp.sum(dyv, axis=0, keepdims=True)
        dsc_ref[0] += jnp.sum(dyv * (xhat * gv), axis=0, keepdims=True)
        dn = dyv * (1.0 + sc_ref[0])
        dg_ref[...] += jnp.sum(dn * xhat, axis=0, keepdims=True)
        dxh = dn * gv
        dx_ref[...] = rstd * (dxh - xhat * jnp.mean(dxh * xhat, axis=-1, keepdims=True))

    return pl.pallas_call(
        body, name="norm_mod_bwd", grid=(t // tm,),
        in_specs=[pl.BlockSpec((tm, d), lambda i: (i, 0)), pl.BlockSpec((1, d), lambda i: (0, 0)),
                  pl.BlockSpec((1, 1, d), lambda i: (seg(i), 0, 0)), pl.BlockSpec((tm, d), lambda i: (i, 0))],
        out_specs=[pl.BlockSpec((tm, d), lambda i: (i, 0)), pl.BlockSpec((1, d), lambda i: (0, 0)),
                   pl.BlockSpec((1, 1, d), lambda i: (seg(i), 0, 0)), pl.BlockSpec((1, 1, d), lambda i: (seg(i), 0, 0))],
        out_shape=[jax.ShapeDtypeStruct((t, d), F32), jax.ShapeDtypeStruct((1, d), F32),
                   jax.ShapeDtypeStruct((nseg, 1, d), F32), jax.ShapeDtypeStruct((nseg, 1, d), F32)],
        compiler_params=_cparams(("arbitrary",)),
    )(x, g.reshape(1, d), scale.reshape(nseg, 1, d), dy)


@functools.partial(jax.custom_vjp, nondiff_argnums=(4,))
def norm_mod(x, g, shift, scale, seg_rows):
    return _nm_fwd_call(x, g, shift, scale, seg_rows)


def _norm_mod_fwd(x, g, shift, scale, seg_rows):
    return _nm_fwd_call(x, g, shift, scale, seg_rows), (x, g, scale)


def _norm_mod_bwd(seg_rows, res, dy):
    x, g, scale = res
    dx, dg, dsh, dsc = _nm_bwd_call(x, g, scale, dy, seg_rows)
    return dx, dg.reshape(g.shape), dsh.reshape(scale.shape), dsc.reshape(scale.shape)


norm_mod.defvjp(_norm_mod_fwd, _norm_mod_bwd)


def rmsnorm_p(x, g):
    z = jnp.zeros((1, x.shape[1]), F32)
    return norm_mod(x, g, z, z, x.shape[0])


def _linrec_call(a, b, n_lat, mode, name):
    t, c = a.shape
    tb = _tile(math.gcd(t, n_lat), 512, SUBLANE)
    nb, nl = t // tb, n_lat // tb
    reverse = mode in ("B", "C")
    tiles = tb // SUBLANE

    def block(i):
        if mode == "A":
            return (i + nl) % nb
        if mode == "B":
            return nb - 1 - i
        if mode == "C":
            return jnp.where(i < nl, nl - 1 - i, nb - 1 - (i - nl))
        return i

    def body(a_ref, b_ref, h_ref, carry_ref):
        @pl.when(pl.program_id(0) == 0)
        def _():
            carry_ref[...] = jnp.zeros_like(carry_ref)

        def tile(i, carry):
            r = pl.multiple_of((tiles - 1 - i if reverse else i) * SUBLANE, SUBLANE)
            av = a_ref[pl.ds(r, SUBLANE), :]
            bv = b_ref[pl.ds(r, SUBLANE), :]
            row = lax.broadcasted_iota(jnp.int32, av.shape, 0)
            d = 1
            while d < SUBLANE:
                shift = SUBLANE - d if reverse else d
                a_sh = pltpu.roll(av, shift, 0)
                b_sh = pltpu.roll(bv, shift, 0)
                m = (row < SUBLANE - d) if reverse else (row >= d)
                bv = jnp.where(m, av * b_sh + bv, bv)
                av = jnp.where(m, av * a_sh, av)
                d *= 2
            hv = av * carry + bv
            h_ref[pl.ds(r, SUBLANE), :] = hv
            return jnp.sum(jnp.where(row == (0 if reverse else SUBLANE - 1), hv, 0.0), axis=0, keepdims=True)

        carry_ref[...] = lax.fori_loop(0, tiles, tile, carry_ref[...])

    spec = pl.BlockSpec((tb, c), lambda i: (block(i), 0))
    return pl.pallas_call(
        body, name=name, grid=(nb,), in_specs=[spec, spec], out_specs=spec,
        out_shape=jax.ShapeDtypeStruct((t, c), F32), scratch_shapes=[pltpu.VMEM((1, c), F32)],
        compiler_params=_cparams(("arbitrary",)),
    )(a, b)


@functools.partial(jax.custom_vjp, nondiff_argnums=(2, 3))
def linrec(a, b, n_lat, direction):
    return _linrec_call(a, b, n_lat, "AB"[direction], "linrec_fwd")


def _linrec_fwd(a, b, n_lat, direction):
    h = _linrec_call(a, b, n_lat, "AB"[direction], "linrec_fwd")
    return h, (a, h)


def _linrec_bwd(n_lat, direction, res, dh):
    a, h = res
    zero = jnp.zeros_like(a[:1])
    if direction == 0:
        a_next = jnp.concatenate([a[1:n_lat], zero, a[n_lat + 1:], a[:1]], axis=0)
        h_prev = jnp.concatenate([h[-1:], h[:n_lat - 1], zero, h[n_lat:-1]], axis=0)
    else:
        a_next = jnp.concatenate([zero, a[:-1]], axis=0)
        h_prev = jnp.concatenate([h[1:], zero], axis=0)
    g = _linrec_call(a_next, dh, n_lat, "CD"[direction], "linrec_bwd")
    return g * h_prev, g


linrec.defvjp(_linrec_fwd, _linrec_bwd)


def _attn_fwd_call(q, k, v, scale):
    h, lq, dq = q.shape
    lk, dv = v.shape[1], v.shape[2]
    tq = _tile(lq, 256, SUBLANE)

    def body(q_ref, k_ref, v_ref, o_ref, lse_ref, k_sc, v_sc):
        @pl.when(pl.program_id(1) == 0)
        def _():
            k_sc[...] = k_ref[0].astype(BF16)
            v_sc[...] = v_ref[0].astype(BF16)

        s = lax.dot_general(q_ref[0].astype(BF16), k_sc[...], (((1,), (1,)), ((), ())), preferred_element_type=F32) * scale
        m = jnp.max(s, axis=-1, keepdims=True)
        p = jnp.exp(s - m)
        l = jnp.sum(p, axis=-1, keepdims=True)
        o_ref[0] = jnp.dot(p.astype(BF16), v_sc[...], preferred_element_type=F32) / l
        lse_ref[0] = m + jnp.log(l)

    return pl.pallas_call(
        body, name="attn_fwd", grid=(h, lq // tq),
        in_specs=[pl.BlockSpec((1, tq, dq), lambda a, i: (a, i, 0)), pl.BlockSpec((1, lk, dq), lambda a, i: (a, 0, 0)),
                  pl.BlockSpec((1, lk, dv), lambda a, i: (a, 0, 0))],
        out_specs=[pl.BlockSpec((1, tq, dv), lambda a, i: (a, i, 0)), pl.BlockSpec((1, tq, 1), lambda a, i: (a, i, 0))],
        out_shape=[jax.ShapeDtypeStruct((h, lq, dv), F32), jax.ShapeDtypeStruct((h, lq, 1), F32)],
        scratch_shapes=[pltpu.VMEM((lk, dq), BF16), pltpu.VMEM((lk, dv), BF16)],
        compiler_params=_cparams(("parallel", "arbitrary")),
    )(q, k, v)


def _attn_dq_call(q, k, v, do, lse, delta, scale):
    h, lq, dq = q.shape
    lk, dv = v.shape[1], v.shape[2]
    tq = _tile(lq, 256, SUBLANE)

    def body(q_ref, k_ref, v_ref, do_ref, lse_ref, dl_ref, dq_ref, k_sc, v_sc):
        @pl.when(pl.program_id(1) == 0)
        def _():
            k_sc[...] = k_ref[0].astype(BF16)
            v_sc[...] = v_ref[0].astype(BF16)

        kb = k_sc[...]
        s = lax.dot_general(q_ref[0].astype(BF16), kb, (((1,), (1,)), ((), ())), preferred_element_type=F32) * scale
        p = jnp.exp(s - lse_ref[0])
        dp = lax.dot_general(do_ref[0].astype(BF16), v_sc[...], (((1,), (1,)), ((), ())), preferred_element_type=F32)
        ds = p * (dp - dl_ref[0])
        dq_ref[0] = jnp.dot(ds.astype(BF16), kb, preferred_element_type=F32) * scale

    return pl.pallas_call(
        body, name="attn_dq", grid=(h, lq // tq),
        in_specs=[pl.BlockSpec((1, tq, dq), lambda a, i: (a, i, 0)), pl.BlockSpec((1, lk, dq), lambda a, i: (a, 0, 0)),
                  pl.BlockSpec((1, lk, dv), lambda a, i: (a, 0, 0)), pl.BlockSpec((1, tq, dv), lambda a, i: (a, i, 0)),
                  pl.BlockSpec((1, tq, 1), lambda a, i: (a, i, 0)), pl.BlockSpec((1, tq, 1), lambda a, i: (a, i, 0))],
        out_specs=pl.BlockSpec((1, tq, dq), lambda a, i: (a, i, 0)),
        out_shape=jax.ShapeDtypeStruct((h, lq, dq), F32),
        scratch_shapes=[pltpu.VMEM((lk, dq), BF16), pltpu.VMEM((lk, dv), BF16)],
        compiler_params=_cparams(("parallel", "arbitrary")),
    )(q, k, v, do, lse, delta)


def _attn_dkv_call(q, k, v, do, lse_row, delta_row, scale):
    h, lq, dq = q.shape
    lk, dv = v.shape[1], v.shape[2]
    tk = _tile(lk, 256, SUBLANE)

    def body(q_ref, k_ref, v_ref, do_ref, lse_ref, dl_ref, dk_ref, dv_ref, q_sc, do_sc):
        @pl.when(pl.program_id(1) == 0)
        def _():
            q_sc[...] = q_ref[0].astype(BF16)
            do_sc[...] = do_ref[0].astype(BF16)

        qb = q_sc[...]
        dob = do_sc[...]
        st = lax.dot_general(k_ref[0].astype(BF16), qb, (((1,), (1,)), ((), ())), preferred_element_type=F32) * scale
        pt = jnp.exp(st - lse_ref[0])
        dpt = lax.dot_general(v_ref[0].astype(BF16), dob, (((1,), (1,)), ((), ())), preferred_element_type=F32)
        dst = pt * (dpt - dl_ref[0])
        dk_ref[0] = jnp.dot(dst.astype(BF16), qb, preferred_element_type=F32) * scale
        dv_ref[0] = jnp.dot(pt.astype(BF16), dob, preferred_element_type=F32)

    return pl.pallas_call(
        body, name="attn_dkv", grid=(h, lk // tk),
        in_specs=[pl.BlockSpec((1, lq, dq), lambda a, i: (a, 0, 0)), pl.BlockSpec((1, tk, dq), lambda a, i: (a, i, 0)),
                  pl.BlockSpec((1, tk, dv), lambda a, i: (a, i, 0)), pl.BlockSpec((1, lq, dv), lambda a, i: (a, 0, 0)),
                  pl.BlockSpec((1, 1, lq), lambda a, i: (a, 0, 0)), pl.BlockSpec((1, 1, lq), lambda a, i: (a, 0, 0))],
        out_specs=[pl.BlockSpec((1, tk, dq), lambda a, i: (a, i, 0)), pl.BlockSpec((1, tk, dv), lambda a, i: (a, i, 0))],
        out_shape=[jax.ShapeDtypeStruct((h, lk, dq), F32), jax.ShapeDtypeStruct((h, lk, dv), F32)],
        scratch_shapes=[pltpu.VMEM((lq, dq), BF16), pltpu.VMEM((lq, dv), BF16)],
        compiler_params=_cparams(("parallel", "arbitrary")),
    )(q, k, v, do, lse_row, delta_row)


@jax.custom_vjp
def attention(q, k, v):
    return _attn_fwd_call(q, k, v, q.shape[-1] ** -0.5)[0]


def _attention_fwd(q, k, v):
    o, lse = _attn_fwd_call(q, k, v, q.shape[-1] ** -0.5)
    return o, (q, k, v, o, lse)


def _attention_bwd(res, do):
    q, k, v, o, lse = res
    h, lq, _ = q.shape
    scale = q.shape[-1] ** -0.5
    delta = jnp.sum(do * o, axis=-1, keepdims=True)
    dq = _attn_dq_call(q, k, v, do, lse, delta, scale)
    dk, dv = _attn_dkv_call(q, k, v, do, lse.reshape(h, 1, lq), delta.reshape(h, 1, lq), scale)
    return dq, dk, dv


attention.defvjp(_attention_fwd, _attention_bwd)


def _na_start(r, rows):
    return jnp.clip(r - NA_WIN_H // 2, 0, rows - NA_WIN_H)


NA_HEAD_BLOCK = 2


def _na_head_views(refs, hh, hd):
    return [t.at[:, pl.ds(hh * hd, hd)] if len(t.shape) == 2 else t.at[pl.ds(hh, 1)] for t in refs]


def _na_fwd_call(q, k, v, kc, vc, bias):
    n, width = q.shape
    heads, hd = width // NA_HEAD_DIM, NA_HEAD_DIM
    hb = math.gcd(heads, NA_HEAD_BLOCK)
    rows, nwin, nctx = n // GRID_W, NA_WIN_H * GRID_W, kc.shape[0]
    scale = hd ** -0.5
    nt = (((1,), (1,)), ((), ()))

    def one_head(q_ref, k_ref, v_ref, kc_ref, vc_ref, b_ref, o_ref, lse_ref):
        r = pl.program_id(1)
        start = pl.multiple_of(_na_start(r, rows) * GRID_W, GRID_W)
        qb = q_ref[...].astype(BF16)
        kw = k_ref[pl.ds(start, nwin), :].astype(BF16)
        vw = v_ref[pl.ds(start, nwin), :].astype(BF16)
        sw = lax.dot_general(qb, kw, nt, preferred_element_type=F32) * scale + b_ref[0, 0]
        sc = lax.dot_general(qb, kc_ref[...].astype(BF16), nt, preferred_element_type=F32) * scale
        m = jnp.maximum(jnp.max(sw, axis=-1, keepdims=True), jnp.max(sc, axis=-1, keepdims=True))
        pw = jnp.exp(sw - m)
        pc = jnp.exp(sc - m)
        l = jnp.sum(pw, axis=-1, keepdims=True) + jnp.sum(pc, axis=-1, keepdims=True)
        pw = pw / l
        pc = pc / l
        o_ref[...] = (jnp.dot(pw.astype(BF16), vw, preferred_element_type=F32)
                      + jnp.dot(pc.astype(BF16), vc_ref[...].astype(BF16), preferred_element_type=F32))
        lse_ref[0] = m + jnp.log(l)

    def body(*refs):
        for hh in range(hb):
            one_head(*_na_head_views(refs, hh, hd))

    full = lambda rws: pl.BlockSpec((rws, hb * hd), lambda a, r: (0, a))
    return pl.pallas_call(
        body, name="natten_fwd", grid=(heads // hb, rows),
        in_specs=[pl.BlockSpec((GRID_W, hb * hd), lambda a, r: (r, a)), full(n), full(n), full(nctx), full(nctx),
                  pl.BlockSpec((hb, 1, GRID_W, nwin), lambda a, r: (a, r - _na_start(r, rows), 0, 0))],
        out_specs=[pl.BlockSpec((GRID_W, hb * hd), lambda a, r: (r, a)), pl.BlockSpec((hb, GRID_W, 1), lambda a, r: (a, r, 0))],
        out_shape=[jax.ShapeDtypeStruct((n, width), F32), jax.ShapeDtypeStruct((heads, n, 1), F32)],
        compiler_params=_cparams(("parallel", "arbitrary")),
    )(q, k, v, kc, vc, bias)


def _na_bwd_call(q, k, v, kc, vc, bias, bias_t, do, lse, delta, lse_row, delta_row):
    n, width = q.shape
    heads, hd = width // NA_HEAD_DIM, NA_HEAD_DIM
    hb = math.gcd(heads, NA_HEAD_BLOCK)
    rows, nwin, nctx = n // GRID_W, NA_WIN_H * GRID_W, kc.shape[0]
    scale = hd ** -0.5
    nt = (((1,), (1,)), ((), ()))

    def one_head(q_ref, k_ref, v_ref, kc_ref, vc_ref, b_ref, bt_ref, do_ref, lse_ref, dl_ref, lser_ref, dlr_ref,
                 dq_ref, dk_ref, dv_ref, dkc_ref, dvc_ref, db_ref):
        r = pl.program_id(1)
        st = _na_start(r, rows)
        start = pl.multiple_of(st * GRID_W, GRID_W)

        @pl.when(r == 0)
        def _():
            dk_ref[...] = jnp.zeros_like(dk_ref)
            dv_ref[...] = jnp.zeros_like(dv_ref)
            dkc_ref[...] = jnp.zeros_like(dkc_ref)
            dvc_ref[...] = jnp.zeros_like(dvc_ref)

        @pl.when((r <= NA_WIN_H // 2) | (r > rows - NA_WIN_H // 2))
        def _():
            db_ref[...] = jnp.zeros_like(db_ref)

        qb = q_ref[...].astype(BF16)
        dob = do_ref[...].astype(BF16)
        kw = k_ref[pl.ds(start, nwin), :].astype(BF16)
        vw = v_ref[pl.ds(start, nwin), :].astype(BF16)
        kcb = kc_ref[...].astype(BF16)
        vcb = vc_ref[...].astype(BF16)
        lse_c, dl_c = lse_ref[0], dl_ref[0]
        pw = jnp.exp(lax.dot_general(qb, kw, nt, preferred_element_type=F32) * scale + b_ref[0, 0] - lse_c)
        pc = jnp.exp(lax.dot_general(qb, kcb, nt, preferred_element_type=F32) * scale - lse_c)
        dsw = pw * (lax.dot_general(dob, vw, nt, preferred_element_type=F32) - dl_c)
        dsc = pc * (lax.dot_general(dob, vcb, nt, preferred_element_type=F32) - dl_c)
        db_ref[0, 0] += dsw
        dq_ref[...] = (jnp.dot(dsw.astype(BF16), kw, preferred_element_type=F32)
                       + jnp.dot(dsc.astype(BF16), kcb, preferred_element_type=F32)) * scale
        lse_r, dl_r = lser_ref[0, 0], dlr_ref[0, 0]
        pwt = jnp.exp(lax.dot_general(kw, qb, nt, preferred_element_type=F32) * scale + bt_ref[0, 0] - lse_r)
        pct = jnp.exp(lax.dot_general(kcb, qb, nt, preferred_element_type=F32) * scale - lse_r)
        dswt = pwt * (lax.dot_general(vw, dob, nt, preferred_element_type=F32) - dl_r)
        dsct = pct * (lax.dot_general(vcb, dob, nt, preferred_element_type=F32) - dl_r)
        dk_ref[pl.ds(start, nwin), :] += jnp.dot(dswt.astype(BF16), qb, preferred_element_type=F32) * scale
        dv_ref[pl.ds(start, nwin), :] += jnp.dot(pwt.astype(BF16), dob, preferred_element_type=F32)
        dkc_ref[...] += jnp.dot(dsct.astype(BF16), qb, preferred_element_type=F32) * scale
        dvc_ref[...] += jnp.dot(pct.astype(BF16), dob, preferred_element_type=F32)

    def body(*refs):
        for hh in range(hb):
            one_head(*_na_head_views(refs, hh, hd))

    full = lambda rws: pl.BlockSpec((rws, hb * hd), lambda a, r: (0, a))
    tile = pl.BlockSpec((GRID_W, hb * hd), lambda a, r: (r, a))
    pat = lambda r: r - _na_start(r, rows)
    col = pl.BlockSpec((hb, GRID_W, 1), lambda a, r: (a, r, 0))
    rowv = pl.BlockSpec((hb, 1, 1, GRID_W), lambda a, r: (a, r, 0, 0))
    return pl.pallas_call(
        body, name="natten_bwd", grid=(heads // hb, rows),
        in_specs=[tile, full(n), full(n), full(nctx), full(nctx),
                  pl.BlockSpec((hb, 1, GRID_W, nwin), lambda a, r: (a, pat(r), 0, 0)),
                  pl.BlockSpec((hb, 1, nwin, GRID_W), lambda a, r: (a, pat(r), 0, 0)),
                  tile, col, col, rowv, rowv],
        out_specs=[tile, full(n), full(n), full(nctx), full(nctx),
                   pl.BlockSpec((hb, 1, GRID_W, nwin), lambda a, r: (a, pat(r), 0, 0))],
        out_shape=[jax.ShapeDtypeStruct((n, width), F32), jax.ShapeDtypeStruct((n, width), F32),
                   jax.ShapeDtypeStruct((n, width), F32), jax.ShapeDtypeStruct((nctx, width), F32),
                   jax.ShapeDtypeStruct((nctx, width), F32), jax.ShapeDtypeStruct(bias.shape, F32)],
        compiler_params=_cparams(("parallel", "arbitrary")),
    )(q, k, v, kc, vc, bias, bias_t, do, lse, delta, lse_row, delta_row)


@jax.custom_vjp
def natten(q, k, v, kc, vc, bias):
    return _na_fwd_call(q, k, v, kc, vc, bias)[0]


def _natten_fwd(q, k, v, kc, vc, bias):
    o, lse = _na_fwd_call(q, k, v, kc, vc, bias)
    return o, (q, k, v, kc, vc, bias, o, lse)


def _natten_bwd(res, do):
    q, k, v, kc, vc, bias, o, lse = res
    n, width = q.shape
    heads, rows = width // NA_HEAD_DIM, n // GRID_W
    delta = jnp.sum((do * o).reshape(n, heads, NA_HEAD_DIM), axis=-1).T.reshape(heads, n, 1)
    return tuple(_na_bwd_call(q, k, v, kc, vc, bias, jnp.swapaxes(bias, 2, 3), do, lse, delta,
                              lse.reshape(heads, rows, 1, GRID_W), delta.reshape(heads, rows, 1, GRID_W)))


natten.defvjp(_natten_fwd, _natten_bwd)


def na_bias_patterns(rpb):
    heads = rpb.shape[0]
    pid = np.arange(NA_WIN_H)[:, None]
    j = np.arange(NA_WIN_H)[None, :]
    row_idx = j - pid + (NA_WIN_H - 1)
    row_hot = (row_idx[..., None] == np.arange(2 * NA_WIN_H - 1)).astype(np.float32)
    cidx = np.arange(GRID_W)
    c_start = np.clip(cidx - NA_WIN_W // 2, 0, GRID_W - NA_WIN_W)
    in_win = (cidx[None, :] >= c_start[:, None]) & (cidx[None, :] < c_start[:, None] + NA_WIN_W)
    col_idx = np.clip(cidx[None, :] - cidx[:, None], -(NA_WIN_W - 1), NA_WIN_W - 1) + (NA_WIN_W - 1)
    col_hot = (col_idx[..., None] == np.arange(2 * NA_WIN_W - 1)).astype(np.float32)
    hi = lax.Precision.HIGHEST
    tmp = jnp.einsum("hab,pja->hpjb", rpb, jnp.asarray(row_hot), precision=hi)
    bias = jnp.einsum("hpjb,qkb->hpqjk", tmp, jnp.asarray(col_hot), precision=hi)
    bias = jnp.where(jnp.asarray(in_win)[None, None, :, None, :], bias, NEG_INF)
    return bias.reshape(heads, NA_WIN_H, GRID_W, NA_WIN_H * GRID_W)


FFN_COLS = 128
FFN_HALO = SUBLANE


def _ffn_chunks(t):
    ch = _tile(t, 544, SUBLANE)
    ext = min(t, ch + 2 * FFN_HALO)
    return [(r0, ch, min(max(r0 - FFN_HALO, 0), t - ext), ext) for r0 in range(0, t, ch)]


def _ffn_conv_parts(x, start, n_lat, t):
    ext = x.shape[0]
    row = start + lax.broadcasted_iota(jnp.int32, x.shape, 0)
    no_prev = (row == 0) | (row == n_lat)
    no_next = (row == n_lat - 1) | (row == t - 1)
    return jnp.where(no_prev, 0.0, pltpu.roll(x, 1, 0)), jnp.where(no_next, 0.0, pltpu.roll(x, ext - 1, 0)), no_prev, no_next


def _sigmoid(x):
    return 1.0 / (1.0 + jnp.exp(-x))


def _ffn_act_fwd_call(uv, ug, cwv, cwg, cbv, cbg, n_lat):
    t, f = uv.shape
    tc = _tile(f, FFN_COLS, LANE)
    chunks = _ffn_chunks(t)

    def body(xv_ref, xg_ref, wv_ref, wg_ref, bv_ref, bg_ref, o_ref):
        def conv(x, start, w_ref, b_ref):
            prev, nxt, _, _ = _ffn_conv_parts(x, start, n_lat, t)
            return prev * w_ref[0:1, :] + x * w_ref[1:2, :] + nxt * w_ref[2:3, :] + b_ref[...]

        for r0, ch, start, ext in chunks:
            cv = conv(xv_ref[start:start + ext, :], start, wv_ref, bv_ref)
            cg = conv(xg_ref[start:start + ext, :], start, wg_ref, bg_ref)
            act = (cg * _sigmoid(cg)) * cv
            o_ref[r0:r0 + ch, :] = act[r0 - start:r0 - start + ch, :]

    col = pl.BlockSpec((t, tc), lambda j: (0, j))
    wsp = pl.BlockSpec((FFN_CONV_W, tc), lambda j: (0, j))
    bsp = pl.BlockSpec((1, tc), lambda j: (0, j))
    return pl.pallas_call(
        body, name="ffn_act_fwd", grid=(f // tc,), in_specs=[col, col, wsp, wsp, bsp, bsp], out_specs=col,
        out_shape=jax.ShapeDtypeStruct((t, f), F32), compiler_params=_cparams(("parallel",)),
    )(uv, ug, cwv, cwg, cbv.reshape(1, f), cbg.reshape(1, f))


def _ffn_act_bwd_call(uv, ug, cwv, cwg, cbv, cbg, dact, n_lat):
    t, f = uv.shape
    tc = _tile(f, FFN_COLS, LANE)
    chunks = _ffn_chunks(t)

    def body(xv_ref, xg_ref, wv_ref, wg_ref, bv_ref, bg_ref, da_ref, dxv_ref, dxg_ref, dw_ref):
        dw_ref[...] = jnp.zeros_like(dw_ref)
        for r0, ch, start, ext in chunks:
            lo = r0 - start
            xv = xv_ref[start:start + ext, :]
            xg = xg_ref[start:start + ext, :]
            da = da_ref[start:start + ext, :]
            pv, nv, no_prev, no_next = _ffn_conv_parts(xv, start, n_lat, t)
            pg, ng, _, _ = _ffn_conv_parts(xg, start, n_lat, t)
            cv = pv * wv_ref[0:1, :] + xv * wv_ref[1:2, :] + nv * wv_ref[2:3, :] + bv_ref[...]
            cg = pg * wg_ref[0:1, :] + xg * wg_ref[1:2, :] + ng * wg_ref[2:3, :] + bg_ref[...]
            sig = _sigmoid(cg)
            dcv = da * (cg * sig)
            dcg = da * cv * (sig * (1.0 + cg * (1.0 - sig)))
            for base, w_ref, dc, parts, dx_ref in ((0, wv_ref, dcv, (pv, xv, nv), dxv_ref), (4, wg_ref, dcg, (pg, xg, ng), dxg_ref)):
                dc_next = jnp.where(no_next, 0.0, pltpu.roll(dc, ext - 1, 0))
                dc_prev = jnp.where(no_prev, 0.0, pltpu.roll(dc, 1, 0))
                dx = dc_next * w_ref[0:1, :] + dc * w_ref[1:2, :] + dc_prev * w_ref[2:3, :]
                dx_ref[r0:r0 + ch, :] = dx[lo:lo + ch, :]
                dci = dc[lo:lo + ch, :]
                for k, part in enumerate(parts):
                    dw_ref[base + k:base + k + 1, :] += jnp.sum(dci * part[lo:lo + ch, :], axis=0, keepdims=True)
                dw_ref[base + 3:base + 4, :] += jnp.sum(dci, axis=0, keepdims=True)

    col = pl.BlockSpec((t, tc), lambda j: (0, j))
    wsp = pl.BlockSpec((FFN_CONV_W, tc), lambda j: (0, j))
    bsp = pl.BlockSpec((1, tc), lambda j: (0, j))
    return pl.pallas_call(
        body, name="ffn_act_bwd", grid=(f // tc,), in_specs=[col, col, wsp, wsp, bsp, bsp, col],
        out_specs=[col, col, pl.BlockSpec((2 * (FFN_CONV_W + 1), tc), lambda j: (0, j))],
        out_shape=[jax.ShapeDtypeStruct((t, f), F32), jax.ShapeDtypeStruct((t, f), F32),
                   jax.ShapeDtypeStruct((2 * (FFN_CONV_W + 1), f), F32)],
        compiler_params=_cparams(("parallel",)),
    )(uv, ug, cwv, cwg, cbv.reshape(1, f), cbg.reshape(1, f), dact)


@functools.partial(jax.custom_vjp, nondiff_argnums=(6,))
def ffn_act(uv, ug, cwv, cwg, cbv, cbg, n_lat):
    return _ffn_act_fwd_call(uv, ug, cwv, cwg, cbv, cbg, n_lat)


def _ffn_act_fwd(uv, ug, cwv, cwg, cbv, cbg, n_lat):
    return _ffn_act_fwd_call(uv, ug, cwv, cwg, cbv, cbg, n_lat), (uv, ug, cwv, cwg, cbv, cbg)


def _ffn_act_bwd(n_lat, res, dact):
    duv, dug, dw = _ffn_act_bwd_call(*res, dact, n_lat)
    nw = FFN_CONV_W
    return duv, dug, dw[:nw], dw[nw + 1:2 * nw + 1], dw[nw], dw[2 * nw + 1]


ffn_act.defvjp(_ffn_act_fwd, _ffn_act_bwd)


def adamw(w, g, m, v):
    shape = w.shape
    last = shape[-1]
    r = int(np.prod(shape[:-1]))
    w2, g2, m2, v2 = (t.reshape(1, r, last) for t in (w, g, m, v))
    tr = _tile(r, max(SUBLANE, (1 << 18) // last), SUBLANE)

    def body(w_ref, g_ref, m_ref, v_ref, d_ref, mo_ref, vo_ref):
        gv = g_ref[...]
        mn = ADAM_B1 * m_ref[...] + (1.0 - ADAM_B1) * gv
        vn = ADAM_B2 * v_ref[...] + (1.0 - ADAM_B2) * (gv * gv)
        m_hat = mn / (1.0 - ADAM_B1 ** ADAM_STEP)
        v_hat = vn / (1.0 - ADAM_B2 ** ADAM_STEP)
        d_ref[...] = -ADAM_LR * (m_hat / (jnp.sqrt(v_hat) + ADAM_EPS) + ADAM_WD * w_ref[...])
        mo_ref[...] = mn
        vo_ref[...] = vn

    spec = pl.BlockSpec((1, tr, last), lambda i: (0, i, 0))
    out = pl.pallas_call(
        body, name="adamw", grid=(r // tr,), in_specs=[spec] * 4, out_specs=[spec] * 3,
        out_shape=[jax.ShapeDtypeStruct((1, r, last), F32)] * 3, compiler_params=_cparams(("parallel",)),
    )(w2, g2, m2, v2)
    return tuple(t.reshape(shape) for t in out)


def _place():
    x, y, c = lax.axis_index("x"), lax.axis_index("y"), lax.axis_index("c")
    chips = [(1 - x, y), (x, 1 - y), (1 - x, 1 - y)]
    return x, y, c, chips


HBM_SPEC = pl.BlockSpec(memory_space=pltpu.HBM)


def all_gather_chips(xs, name):
    def body(x_ref, out_ref, send_sems, recv_sems):
        x, y, c, chips = _place()
        sibling = (x, y, 1 - c)

        def piece(chip, half):
            return out_ref.at[2 * chip[0] + chip[1], half]

        def copy(k, chip, half, to, src=None):
            dst = piece(chip, half)
            return pltpu.make_async_remote_copy(src_ref=dst if src is None else src, dst_ref=dst, send_sem=send_sems.at[k],
                                                recv_sem=recv_sems.at[k], device_id=to, device_id_type=MESH)

        first = [copy(k, (x, y), c, (*chip, c), src=x_ref.at[c]) for k, chip in enumerate(chips)]
        for cp in first:
            cp.start()
        passed = [copy(3 + k, chip, c, sibling) for k, chip in enumerate(chips)]
        for k, chip in enumerate(chips):
            copy(k, chip, c, sibling).wait_recv()
            passed[k].start()
        for k, chip in enumerate(chips):
            copy(3 + k, chip, 1 - c, sibling).wait_recv()
        for cp in first + passed:
            cp.wait_send()

    out = pl.pallas_call(
        body, name=name, in_specs=[HBM_SPEC], out_specs=HBM_SPEC,
        out_shape=jax.ShapeDtypeStruct((N_CHIPS,) + xs.shape, xs.dtype),
        scratch_shapes=[pltpu.SemaphoreType.DMA((6,)), pltpu.SemaphoreType.DMA((6,))],
        compiler_params=pltpu.CompilerParams(has_side_effects=True),
    )(xs)
    me = 2 * lax.axis_index("x") + lax.axis_index("y")
    return lax.dynamic_update_slice(out, xs[None], (me,) + (0,) * xs.ndim)


def _rs_to_sibling(g, name):
    _, _, r, b = g.shape

    def body(g_ref, out_ref, send_sems, recv_sems):
        x, y, c, _ = _place()
        cps = [pltpu.make_async_remote_copy(src_ref=g_ref.at[s, 1 - c], dst_ref=out_ref.at[s], send_sem=send_sems.at[s],
                                            recv_sem=recv_sems.at[s], device_id=(x, y, 1 - c), device_id_type=MESH)
               for s in range(N_CHIPS)]
        for cp in cps:
            cp.start()
        for cp in cps:
            cp.wait()

    return pl.pallas_call(
        body, name=name, in_specs=[HBM_SPEC], out_specs=HBM_SPEC, out_shape=jax.ShapeDtypeStruct((N_CHIPS, r, b), g.dtype),
        scratch_shapes=[pltpu.SemaphoreType.DMA((N_CHIPS,)), pltpu.SemaphoreType.DMA((N_CHIPS,))],
        compiler_params=pltpu.CompilerParams(has_side_effects=True),
    )(g)


def _rs_chip_sum(g, la, c_idx, name):
    _, _, r, b = g.shape
    tr = _tile(r, max(16, (1 << 18) // b), 16)

    def body(c_ref, g_ref, la_ref, o_ref):
        o_ref[...] = (g_ref[:, 0] + la_ref[...]).astype(BF16)

    return pl.pallas_call(
        body, name=name,
        grid_spec=pltpu.PrefetchScalarGridSpec(
            num_scalar_prefetch=1, grid=(N_CHIPS, r // tr),
            in_specs=[pl.BlockSpec((1, 1, tr, b), lambda s, i, c_ref: (s, c_ref[0], i, 0)),
                      pl.BlockSpec((1, tr, b), lambda s, i, c_ref: (s, i, 0))],
            out_specs=pl.BlockSpec((1, tr, b), lambda s, i, c_ref: (s, i, 0))),
        out_shape=jax.ShapeDtypeStruct((N_CHIPS, r, b), BF16), compiler_params=_cparams(("parallel", "parallel")),
    )(c_idx, g, la)


def _rs_to_owners(hb, name):
    _, r, b = hb.shape

    def body(h_ref, out_ref, send_sems, recv_sems):
        x, y, c, chips = _place()
        cps = [pltpu.make_async_remote_copy(src_ref=h_ref.at[2 * chip[0] + chip[1]], dst_ref=out_ref.at[k],
                                            send_sem=send_sems.at[k], recv_sem=recv_sems.at[k], device_id=(*chip, c),
                                            device_id_type=MESH) for k, chip in enumerate(chips)]
        for cp in cps:
            cp.start()
        for cp in cps:
            cp.wait()

    return pl.pallas_call(
        body, name=name, in_specs=[HBM_SPEC], out_specs=HBM_SPEC, out_shape=jax.ShapeDtypeStruct((3, r, b), hb.dtype),
        scratch_shapes=[pltpu.SemaphoreType.DMA((3,)), pltpu.SemaphoreType.DMA((3,))],
        compiler_params=pltpu.CompilerParams(has_side_effects=True),
    )(hb)


def _rs_final_sum(g, la, lb, sc_idx, name):
    _, _, r, b = g.shape
    tr = _tile(r, max(16, (1 << 18) // b), 16)

    def body(i_ref, g_ref, la_ref, lb_ref, o_ref):
        o_ref[...] = (g_ref[0, 0] + la_ref[0]) + lb_ref[0].astype(F32) + lb_ref[1].astype(F32) + lb_ref[2].astype(F32)

    return pl.pallas_call(
        body, name=name,
        grid_spec=pltpu.PrefetchScalarGridSpec(
            num_scalar_prefetch=1, grid=(r // tr,),
            in_specs=[pl.BlockSpec((1, 1, tr, b), lambda i, i_ref: (i_ref[0], i_ref[1], i, 0)),
                      pl.BlockSpec((1, tr, b), lambda i, i_ref: (i_ref[0], i, 0)),
                      pl.BlockSpec((3, tr, b), lambda i, i_ref: (0, i, 0))],
            out_specs=pl.BlockSpec((tr, b), lambda i, i_ref: (i, 0))),
        out_shape=jax.ShapeDtypeStruct((r, b), F32), compiler_params=_cparams(("parallel",)),
    )(sc_idx, g, la, lb)


def _rs_join_halves(f, name):
    r, b = f.shape

    def body(f_ref, out_ref, send_sem, recv_sem):
        x, y, c, _ = _place()
        cp = pltpu.make_async_remote_copy(src_ref=f_ref, dst_ref=out_ref.at[c], send_sem=send_sem, recv_sem=recv_sem,
                                          device_id=(x, y, 1 - c), device_id_type=MESH)
        cp.start()
        pltpu.make_async_remote_copy(src_ref=f_ref, dst_ref=out_ref.at[1 - c], send_sem=send_sem, recv_sem=recv_sem,
                                     device_id=(x, y, 1 - c), device_id_type=MESH).wait_recv()
        cp.wait_send()

    out = pl.pallas_call(
        body, name=name, in_specs=[HBM_SPEC], out_specs=HBM_SPEC, out_shape=jax.ShapeDtypeStruct((2, r, b), f.dtype),
        scratch_shapes=[pltpu.SemaphoreType.DMA, pltpu.SemaphoreType.DMA],
        compiler_params=pltpu.CompilerParams(has_side_effects=True),
    )(f)
    return lax.dynamic_update_slice(out, f[None], (lax.axis_index("c"), 0, 0))


def reduce_scatter_devices(g, tag):
    c = lax.axis_index("c")
    me = 2 * lax.axis_index("x") + lax.axis_index("y")
    la = _rs_to_sibling(g, "rs_sibling_" + tag)
    hb = _rs_chip_sum(g, la, jnp.stack([c]).astype(jnp.int32), "rs_chipsum_" + tag)
    lb = _rs_to_owners(hb, "rs_owners_" + tag)
    f = _rs_final_sum(g, la, lb, jnp.stack([me, c]).astype(jnp.int32), "rs_final_" + tag)
    return _rs_join_halves(f, "rs_join_" + tag)


def _w_in_layout(d_model, na_width, lru_width, q_rank, kv_rank):
    head = 3 * na_width + 2 * lru_width + q_rank + kv_rank
    total = head + MLA_ROPE_DIM + N_BRANCH * d_model
    padded = -(-total // LANE) * LANE
    return None, head, total, padded


def _col_pieces(shards, a, b):
    n4 = shards[0].shape[1]
    out = []
    for s, t in enumerate(shards):
        lo, hi = max(a, s * n4), min(b, (s + 1) * n4)
        if lo < hi:
            out.append(t[:, lo - s * n4:hi - s * n4])
    return out


def _assemble_w_in(shards, head, total, padded):
    parts = (_col_pieces(shards, 0, head) + _col_pieces(shards, head + MLA_ROPE_DIM, total)
             + _col_pieces(shards, head, head + MLA_ROPE_DIM))
    if padded > total:
        parts.append(jnp.zeros((shards[0].shape[0], padded - total), shards[0].dtype))
    return jnp.concatenate(parts, axis=1)


def _w_in_shard_grad(g, s, n4, head, total):
    n_gates = total - head - MLA_ROPE_DIM
    a, b = s * n4, (s + 1) * n4
    parts = []
    for lo, hi, shift in ((0, head, 0), (head, head + MLA_ROPE_DIM, n_gates), (head + MLA_ROPE_DIM, total, -MLA_ROPE_DIM)):
        lo, hi = max(a, lo), min(b, hi)
        if lo < hi:
            parts.append(g[:, lo + shift:hi + shift])
    return jnp.concatenate(parts, axis=1)


@functools.partial(jax.custom_vjp, nondiff_argnums=(1,))
def split_cols(z, sizes):
    out, o = [], 0
    for sz in sizes:
        out.append(z[:, o:o + sz])
        o += sz
    return tuple(out)


def _split_cols_fwd(z, sizes):
    return split_cols(z, sizes), None


def _split_cols_bwd(sizes, res, cts):
    return (jnp.concatenate(cts, axis=1),)


split_cols.defvjp(_split_cols_fwd, _split_cols_bwd)


def _dwconv(x, w, b):
    width = w.shape[0]
    n = x.shape[0]
    left = width // 2
    xp = jnp.pad(x, ((left, width - 1 - left), (0, 0)))
    return sum(xp[i:i + n] * w[i] for i in range(width)) + b


def _dwconv_seg(x, w, b, n_lat):
    return jnp.concatenate([_dwconv(x[:n_lat], w, b), _dwconv(x[n_lat:], w, b)], axis=0)


def _rope(x, cos, sin):
    half = x.shape[-1] // 2
    x1, x2 = x[..., :half], x[..., half:]
    return jnp.concatenate([x1 * cos - x2 * sin, x1 * sin + x2 * cos], axis=-1)


def _rope_tables(n_lat, n_ctx):
    t = jnp.arange(n_lat, dtype=jnp.int32)
    row = (t // GRID_W).astype(F32)
    col = (t % GRID_W).astype(F32)
    n_freq = MLA_ROPE_DIM // 4
    inv_freq = ROPE_THETA ** (-jnp.arange(n_freq, dtype=F32) / n_freq)
    ang = jnp.concatenate([row[:, None] * inv_freq, col[:, None] * inv_freq], axis=-1)
    ones = jnp.ones((n_ctx, MLA_ROPE_DIM // 2), F32)
    return jnp.concatenate([jnp.cos(ang), ones], axis=0), jnp.concatenate([jnp.sin(ang), 0.0 * ones], axis=0)


def _heads_first(t, heads):
    n = t.shape[0]
    return t.reshape(n, heads, -1).transpose(1, 0, 2)


def _heads_last(t):
    return t.transpose(1, 0, 2).reshape(t.shape[1], -1)


def _lru_coeffs(u, w_a, b_a, w_x, b_x, lam):
    r = jax.nn.sigmoid(bd_dot(u, w_a) + b_a)
    i = jax.nn.sigmoid(bd_dot(u, w_x) + b_x)
    log_a = -LRU_C * r * jax.nn.softplus(-lam)
    return jnp.exp(log_a), jnp.sqrt(-jnp.expm1(2.0 * log_a)) * (i * u)


def _forward_loss(big, prox, small, mod_probe, x, ctx, silu_c, target):
    n_lat, d_model = x.shape
    n_ctx = ctx.shape[0]
    na_width = NA_HEADS * NA_HEAD_DIM
    lru_width = small["lru_conv_b"].shape[1]
    q_rank, kv_rank = small["mla_q_norm"].shape[1], small["mla_kv_norm"].shape[1]
    _, head, total, padded = _w_in_layout(d_model, na_width, lru_width, q_rank, kv_rank)
    cos, sin = _rope_tables(n_lat, n_ctx)
    is_ctx = (jnp.arange(n_lat + n_ctx) >= n_lat)[:, None]

    def by_row(v2):
        return jnp.where(is_ctx, v2[1][None, :], v2[0][None, :])

    s_rows = jnp.concatenate([silu_c, jax.nn.silu(small["c_ctx"])[None, :],
                              jnp.zeros((MOD_ROWS - 2, d_model), F32)], axis=0)
    xs = jnp.concatenate([x, ctx], axis=0)
    for l in range(DEPTH):
        mod = (pdot_act(s_rows, big["w_mod"][l]) + small["b_mod"][l])[:2] + mod_probe[l]
        sh1, sc1, g1, sh2, sc2, g2 = split_cols(mod, (d_model,) * N_MOD)
        h = norm_mod(xs, small["norm_mix"][l], sh1, sc1, n_lat)
        z = pdot(h, big["w_in"][l], prox["w_in"][l])
        names = ("q_a", "k_a", "v_a", "lru_x", "lru_g", "cq", "ckv", "gate_a", "gate_b", "gate_c", "kr", "pad")
        widths = (na_width,) * 3 + (lru_width,) * 2 + (q_rank, kv_rank) + (d_model,) * 3 + (MLA_ROPE_DIM, padded - total)
        cols = dict(zip(names, split_cols(z, widths)))
        q_a, k_a, v_a = cols["q_a"], cols["k_a"], cols["v_a"]
        out_a_lat = natten(q_a[:n_lat], k_a[:n_lat], v_a[:n_lat], k_a[n_lat:], v_a[n_lat:], na_bias_patterns(small["na_rpb"][l]))
        out_a_ctx = _heads_last(attention(*(_heads_first(t[n_lat:], NA_HEADS) for t in (q_a, k_a, v_a))))
        out_a = jnp.concatenate([out_a_lat, out_a_ctx], axis=0)
        u = _dwconv_seg(cols["lru_x"], small["lru_conv_w"][l], small["lru_conv_b"][l], n_lat)
        hs = [linrec(*_lru_coeffs(u, small["lru_w_a"][l, d], small["lru_b_a"][l, d], small["lru_w_x"][l, d],
                                  small["lru_b_x"][l, d], small["lru_lam"][l, d]), n_lat, d) for d in range(2)]
        y_b = hs[0] + hs[1]
        out_b = jax.nn.gelu(cols["lru_g"]) * y_b
        q_m = pdot(rmsnorm_p(cols["cq"], small["mla_q_norm"][l]), big["mla_w_q_up"][l], prox["mla_w_q_up"][l])
        q_m = q_m.reshape(-1, MLA_HEADS, MLA_NOPE_DIM + MLA_ROPE_DIM)
        q_m = jnp.concatenate([q_m[..., :MLA_NOPE_DIM], _rope(q_m[..., MLA_NOPE_DIM:], cos[:, None, :], sin[:, None, :])], axis=-1)
        kv = pdot(rmsnorm_p(cols["ckv"], small["mla_kv_norm"][l]), big["mla_w_kv_up"][l], prox["mla_w_kv_up"][l])
        kv = kv.reshape(-1, MLA_HEADS, MLA_NOPE_DIM + MLA_V_DIM)
        k_rope = jnp.broadcast_to(_rope(cols["kr"], cos, sin)[:, None, :], (n_lat + n_ctx, MLA_HEADS, MLA_ROPE_DIM))
        k_m = jnp.concatenate([kv[..., :MLA_NOPE_DIM], k_rope], axis=-1).transpose(1, 0, 2)
        v_m = kv[..., MLA_NOPE_DIM:].transpose(1, 0, 2)
        q_m = q_m.transpose(1, 0, 2)
        out_c = jnp.concatenate([_heads_last(attention(q_m[:, :n_lat], k_m, v_m)),
                                 _heads_last(attention(q_m[:, n_lat:], k_m[:, n_lat:], v_m[:, n_lat:]))], axis=0)
        y = sum(jax.nn.sigmoid(cols[gate]) * pdot(br, big["w_branch"][l][i], prox["w_branch"][l][i])
                for i, (gate, br) in enumerate(zip(("gate_a", "gate_b", "gate_c"), (out_a, out_b, out_c))))
        xs = xs + by_row(g1) * pdot(y, big["w_out"][l], prox["w_out"][l])
        h2 = norm_mod(xs, small["norm_ffn"][l], sh2, sc2, n_lat)
        d_ff = big["ffn_w_down"][l].shape[0]
        halves = (d_ff, d_ff)
        (w_val, w_gate), (p_val, p_gate) = big["ffn_w_up"][l], prox["ffn_w_up"][l]
        cw_val, cw_gate = split_cols(small["ffn_conv_w"][l], halves)
        cb_val, cb_gate = split_cols(small["ffn_conv_b"][l][None, :], halves)
        act = ffn_act(pdot(h2, w_val, p_val), pdot(h2, w_gate, p_gate), cw_val, cw_gate, cb_val[0], cb_gate[0], n_lat)
        xs = xs + by_row(g2) * pdot(act, big["ffn_w_down"][l], prox["ffn_w_down"][l])
    y_out = rmsnorm_p(xs[:n_lat], small["norm_final"])
    return 0.5 * jnp.sum(jnp.mean(jnp.square(y_out - target), axis=-1))


WEIGHTS = ['c_ctx', 'w_mod', 'b_mod', 'norm_mix', 'norm_ffn', 'w_in', 'na_rpb', 'lru_conv_w', 'lru_conv_b', 'lru_w_a',
           'lru_b_a', 'lru_w_x', 'lru_b_x', 'lru_lam', 'mla_q_norm', 'mla_kv_norm', 'mla_w_q_up', 'mla_w_kv_up', 'w_branch',
           'w_out', 'ffn_w_up', 'ffn_conv_w', 'ffn_conv_b', 'ffn_w_down', 'norm_final']
BIG = {'w_mod': 2, 'w_in': 2, 'mla_w_q_up': 2, 'mla_w_kv_up': 2, 'w_branch': 3, 'w_out': 1, 'ffn_w_up': 2, 'ffn_w_down': 1}
SMALL_SHARDED = {'lru_conv_w': 2, 'lru_b_a': 2, 'lru_b_x': 2, 'lru_lam': 2, 'ffn_conv_w': 2}
PACK_LANES = 128
PACK_ROWS = 16


def _gathered_to_full(g, ax):
    t = jnp.moveaxis(g, 0, ax)
    return t.reshape(t.shape[:ax] + (t.shape[ax] * t.shape[ax + 1],) + t.shape[ax + 2:])


def _canon(shape):
    return (2, int(np.prod(shape[:-1])) // 2, shape[-1])


def _pack(parts, multiple):
    flat = jnp.concatenate([p.reshape(-1) for p in parts])
    n = flat.shape[0]
    padded = -(-n // multiple) * multiple
    return jnp.pad(flat, (0, padded - n))


def _unpack(flat, shapes):
    out, o = [], 0
    for s in shapes:
        n = int(np.prod(s))
        out.append(flat[o:o + n].reshape(s))
        o += n
    return out


def kernel(x, c, ctx, c_ctx, w_mod, b_mod, norm_mix, norm_ffn, w_in, na_rpb, lru_conv_w, lru_conv_b, lru_w_a, lru_b_a, lru_w_x, lru_b_x, lru_lam, mla_q_norm, mla_kv_norm, mla_w_q_up, mla_w_kv_up, w_branch, w_out, ffn_w_up, ffn_conv_w, ffn_conv_b, ffn_w_down, norm_final, loss_target, m_c_ctx, m_w_mod, m_b_mod, m_norm_mix, m_norm_ffn, m_w_in, m_na_rpb, m_lru_conv_w, m_lru_conv_b, m_lru_w_a, m_lru_b_a, m_lru_w_x, m_lru_b_x, m_lru_lam, m_mla_q_norm, m_mla_kv_norm, m_mla_w_q_up, m_mla_w_kv_up, m_w_branch, m_w_out, m_ffn_w_up, m_ffn_conv_w, m_ffn_conv_b, m_ffn_w_down, m_norm_final, v_c_ctx, v_w_mod, v_b_mod, v_norm_mix, v_norm_ffn, v_w_in, v_na_rpb, v_lru_conv_w, v_lru_conv_b, v_lru_w_a, v_lru_b_a, v_lru_w_x, v_lru_b_x, v_lru_lam, v_mla_q_norm, v_mla_kv_norm, v_mla_w_q_up, v_mla_w_kv_up, v_w_branch, v_w_out, v_ffn_w_up, v_ffn_conv_w, v_ffn_conv_b, v_ffn_w_down, v_norm_final):
    args = dict(locals())
    w = {n: args[n] for n in WEIGHTS}
    m = {n: args["m_" + n] for n in WEIGHTS}
    v = {n: args["v_" + n] for n in WEIGHTS}
    me = 2 * lax.axis_index("x") + lax.axis_index("y")

    d_model = x.shape[-1]
    _, head, total, padded = _w_in_layout(d_model, NA_HEADS * NA_HEAD_DIM, lru_conv_b.shape[1], mla_q_norm.shape[1], mla_kv_norm.shape[1])
    full, big = {}, {}
    for name, ax in BIG.items():
        shard = w[name]
        g = all_gather_chips(shard.astype(BF16).reshape(_canon(shard.shape)), "ag_" + name).reshape((N_CHIPS,) + shard.shape)
        per_layer = [[g[s, l] for s in range(N_CHIPS)] for l in range(DEPTH)]
        if name == "w_in":
            big[name] = [_assemble_w_in(t, head, total, padded) for t in per_layer]
        elif name == "ffn_w_up":
            big[name] = [tuple(jnp.concatenate(t[i:i + N_CHIPS // 2], axis=1) for i in (0, N_CHIPS // 2)) for t in per_layer]
        elif name == "w_branch":
            big[name] = [[jnp.concatenate([sh[i] for sh in t], axis=1) for i in range(N_BRANCH)] for t in per_layer]
        else:
            big[name] = [jnp.concatenate(t, axis=ax - 1) for t in per_layer]
    packed = _pack([w[n] for n in SMALL_SHARDED], 2 * PACK_ROWS * PACK_LANES).reshape(2, -1, PACK_LANES)
    g = all_gather_chips(packed, "ag_small").reshape(N_CHIPS, -1)
    for name, t in zip(SMALL_SHARDED, zip(*[_unpack(g[s], [w[n].shape for n in SMALL_SHARDED]) for s in range(N_CHIPS)])):
        full[name] = _gathered_to_full(jnp.stack(t), SMALL_SHARDED[name])
    small = {n: full.get(n, w[n]) for n in WEIGHTS if n not in BIG}

    prox = jax.tree.map(lambda t: jnp.zeros(t.shape, F32), {n: t for n, t in big.items() if n != "w_mod"})
    silu_c = jax.nn.silu(c)
    probe = jnp.zeros((DEPTH, 2, N_MOD * d_model), F32)
    loss, (g_prox, g_small, g_mod, g_x) = jax.value_and_grad(_forward_loss, argnums=(1, 2, 3, 4))(
        big, prox, small, probe, x[0], ctx[0], silu_c, loss_target[0])
    loss = lax.psum(loss, ("x", "y", "c"))

    def shard_grad(name, l, s):
        g, n4 = g_prox[name][l], w[name].shape[BIG[name]]
        if name == "w_in":
            return _w_in_shard_grad(g, s, n4, head, total)
        if name == "ffn_w_up":
            half = N_CHIPS // 2
            return lax.slice_in_dim(g[s // half], (s % half) * n4, (s % half + 1) * n4, axis=1)
        if name == "w_branch":
            return jnp.stack([t[:, s * n4:(s + 1) * n4] for t in g])
        return lax.slice_in_dim(g, s * n4, (s + 1) * n4, axis=BIG[name] - 1)

    grads = {}
    for name in BIG:
        if name == "w_mod":
            continue
        stacked = jnp.stack([jnp.stack([shard_grad(name, l, s) for l in range(DEPTH)]) for s in range(N_CHIPS)])
        red = reduce_scatter_devices(stacked.reshape((N_CHIPS,) + _canon(w[name].shape)), name)
        grads[name] = red.reshape(w[name].shape)

    mine = _pack([silu_c, jax.nn.silu(c_ctx), g_mod], PACK_ROWS * PACK_LANES).reshape(-1, PACK_LANES)
    rows = all_gather_chips(_rs_join_halves(mine, "ag_mod_pair"), "ag_mod").reshape(2 * N_CHIPS, -1)
    u_all = jnp.pad(rows[:, :2 * d_model].reshape(-1, d_model), ((0, MOD_ROWS - 4 * N_CHIPS), (0, 0)))
    dm_all = rows[:, 2 * d_model:2 * d_model + g_mod.size].reshape(2 * N_CHIPS, DEPTH, 2, N_MOD * d_model)
    n4 = w_mod.shape[2]
    grads["w_mod"] = jnp.stack([
        _mm_acc(u_all, jnp.pad(lax.dynamic_slice_in_dim(dm_all[:, l].reshape(4 * N_CHIPS, -1), me * n4, n4, axis=1),
                               ((0, MOD_ROWS - 4 * N_CHIPS), (0, 0))), ta=True, name="mm_wgrad_mod") for l in range(DEPTH)])
    small_names = [n for n in WEIGHTS if n not in BIG]
    packed = _pack([g_small[n] for n in small_names], N_CHIPS * 2 * PACK_ROWS * PACK_LANES).reshape(N_CHIPS, 2, -1, PACK_LANES)
    red = all_gather_chips(reduce_scatter_devices(packed, "small"), "ag_small_grads").reshape(-1)
    for name, t in zip(small_names, _unpack(red, [g_small[n].shape for n in small_names])):
        if name in SMALL_SHARDED:
            ax = SMALL_SHARDED[name]
            t = lax.dynamic_slice_in_dim(t, me * w[name].shape[ax], w[name].shape[ax], axis=ax)
        grads[name] = t

    upd = {n: adamw(w[n], grads[n], m[n], v[n]) for n in WEIGHTS}
    return (loss, g_x[None], *[grads[n] for n in WEIGHTS], *[upd[n][0] for n in WEIGHTS],
            *[upd[n][1] for n in WEIGHTS], *[upd[n][2] for n in WEIGHTS])
```

```python
import functools
import math

import numpy as np
import jax
import jax.numpy as jnp
from jax import lax
from jax.experimental import pallas as pl
from jax.experimental.pallas import tpu as pltpu

F32 = jnp.float32
BF16 = jnp.bfloat16
MESH = pl.DeviceIdType.MESH

DEPTH = 4
GRID_W = 64
NORM_EPS = 1e-6
NEG_INF = -1e30
N_MOD = 6
NA_HEADS = 8
NA_HEAD_DIM = 128
NA_WIN_H = 8
NA_WIN_W = 16
LRU_BLOCKS = 8
LRU_CONV_W = 4
LRU_C = 8.0
MLA_HEADS = 8
MLA_NOPE_DIM = 128
MLA_ROPE_DIM = 64
MLA_V_DIM = 128
ROPE_THETA = 10000.0
N_BRANCH = 3
FFN_CONV_W = 3
ADAM_LR = 0.001
ADAM_B1 = 0.9
ADAM_B2 = 0.999
ADAM_EPS = 1e-08
ADAM_WD = 0.01
ADAM_STEP = 10

N_CHIPS = 4
LANE = 128
SUBLANE = 8
VMEM_LIMIT = 48 * 1024 * 1024
MOD_ROWS = 128


def _cparams(sem=None, **kw):
    if sem is not None:
        kw["dimension_semantics"] = sem
    return pltpu.CompilerParams(vmem_limit_bytes=VMEM_LIMIT, **kw)


def _tile(d, cap, align):
    best = None
    t = align
    while t <= min(d, cap):
        if d % t == 0:
            best = t
        t += align
    return d if best is None else best


FULL_K_MAX = 2304


def _mm_rows(a, b, *, tb, emit, name):
    m, k = a.shape
    n = b.shape[0] if tb else b.shape[1]
    tm = _tile(m, 1088, 16)
    tn = _tile(n, 640, LANE)
    dims = (((1,), (1 if tb else 0,)), ((), ()))

    def body(a_ref, b_ref, o_ref, *rest):
        a_sc = rest[-1]

        @pl.when(pl.program_id(1) == 0)
        def _():
            a_sc[...] = a_ref[...].astype(BF16)
            if emit:
                rest[0][...] = a_sc[...]

        o_ref[...] = lax.dot_general(a_sc[...], b_ref[...].astype(BF16), dims, preferred_element_type=F32)

    b_spec = pl.BlockSpec((tn, k), lambda i, j: (j, 0)) if tb else pl.BlockSpec((k, tn), lambda i, j: (0, j))
    out_specs = [pl.BlockSpec((tm, tn), lambda i, j: (i, j))]
    out_shape = [jax.ShapeDtypeStruct((m, n), F32)]
    if emit:
        out_specs.append(pl.BlockSpec((tm, k), lambda i, j: (i, 0)))
        out_shape.append(jax.ShapeDtypeStruct((m, k), BF16))
    out = pl.pallas_call(
        body, name=name, grid=(m // tm, n // tn),
        in_specs=[pl.BlockSpec((tm, k), lambda i, j: (i, 0)), b_spec], out_specs=out_specs, out_shape=out_shape,
        scratch_shapes=[pltpu.VMEM((tm, k), BF16)], compiler_params=_cparams(("parallel", "arbitrary")),
    )(a, b)
    return out if emit else out[0]


def _mm_acc(a, b, *, ta=False, tb=False, name):
    m, k = (a.shape[1], a.shape[0]) if ta else a.shape
    n = b.shape[0] if tb else b.shape[1]
    tm = _tile(m, 2048 if ta else 1088, 16)
    tn = _tile(n, 640 if ta else 2048, LANE)
    tk = _tile(k, 1088, 16) if ta else _tile(k, 640, LANE)
    dims = (((0 if ta else 1,), (1 if tb else 0,)), ((), ()))

    def body(a_ref, b_ref, o_ref):
        @pl.when(pl.program_id(2) == 0)
        def _():
            o_ref[...] = jnp.zeros_like(o_ref)

        o_ref[...] += lax.dot_general(a_ref[...].astype(BF16), b_ref[...].astype(BF16), dims, preferred_element_type=F32)

    a_spec = pl.BlockSpec((tk, tm), lambda i, j, kk: (kk, i)) if ta else pl.BlockSpec((tm, tk), lambda i, j, kk: (i, kk))
    b_spec = pl.BlockSpec((tn, tk), lambda i, j, kk: (j, kk)) if tb else pl.BlockSpec((tk, tn), lambda i, j, kk: (kk, j))
    return pl.pallas_call(
        body, name=name, grid=(m // tm, n // tn, k // tk),
        in_specs=[a_spec, b_spec], out_specs=pl.BlockSpec((tm, tn), lambda i, j, kk: (i, j)),
        out_shape=jax.ShapeDtypeStruct((m, n), F32),
        compiler_params=_cparams(("parallel", "parallel", "arbitrary")),
    )(a, b)


def _pdot_fwd(a, w, wp):
    if a.shape[1] <= FULL_K_MAX:
        out, a_bf = _mm_rows(a, w, tb=False, emit=True, name="mm_fwd")
        return out, (a_bf, w)
    return _mm_acc(a, w, name="mm_fwd_acc"), (a, w)


@jax.custom_vjp
def pdot(a, w, wp):
    return _pdot_fwd(a, w, wp)[0]


def _pdot_dgrad(w, g):
    if w.shape[1] <= FULL_K_MAX:
        return _mm_rows(g, w, tb=True, emit=False, name="mm_dgrad")
    return _mm_acc(g, w, tb=True, name="mm_dgrad_acc")


def _pdot_bwd(res, g):
    a, w = res
    return _pdot_dgrad(w, g), jnp.zeros_like(w), _mm_acc(a, g, ta=True, name="mm_wgrad")


pdot.defvjp(_pdot_fwd, _pdot_bwd)


@jax.custom_vjp
def pdot_act(a, w):
    return _pdot_fwd(a, w, None)[0]


def _pdot_act_fwd(a, w):
    return _pdot_fwd(a, w, None)[0], w


def _pdot_act_bwd(w, g):
    return _pdot_dgrad(w, g), jnp.zeros_like(w)


pdot_act.defvjp(_pdot_act_fwd, _pdot_act_bwd)


def _bd_mm(x, w, name):
    t, c = x.shape
    nb, bd, _ = w.shape
    tm = _tile(t, 1088, 16)

    def body(x_ref, w_ref, o_ref):
        o_ref[...] = jnp.dot(x_ref[...].astype(BF16), w_ref[0].astype(BF16), preferred_element_type=F32)

    return pl.pallas_call(
        body, name=name, grid=(t // tm, nb),
        in_specs=[pl.BlockSpec((tm, bd), lambda i, k: (i, k)), pl.BlockSpec((1, bd, bd), lambda i, k: (k, 0, 0))],
        out_specs=pl.BlockSpec((tm, bd), lambda i, k: (i, k)),
        out_shape=jax.ShapeDtypeStruct((t, c), F32),
        compiler_params=_cparams(("parallel", "parallel")),
    )(x, w)


def _bd_wgrad(x, g, nb, name):
    t, c = x.shape
    bd = c // nb
    tk = _tile(t, 1088, 16)
    nt = t // tk

    def body(x_ref, g_ref, o_ref):
        @pl.when(pl.program_id(1) == 0)
        def _():
            o_ref[...] = jnp.zeros_like(o_ref)

        xt = x_ref[...].T.astype(BF16)
        o_ref[0] += jnp.dot(xt, g_ref[...].astype(BF16), preferred_element_type=F32)

    return pl.pallas_call(
        body, name=name, grid=(nb, nt),
        in_specs=[pl.BlockSpec((tk, bd), lambda k, i: (i, k)), pl.BlockSpec((tk, bd), lambda k, i: (i, k))],
        out_specs=pl.BlockSpec((1, bd, bd), lambda k, i: (k, 0, 0)),
        out_shape=jax.ShapeDtypeStruct((nb, bd, bd), F32),
        compiler_params=_cparams(("parallel", "arbitrary")),
    )(x, g)


@jax.custom_vjp
def bd_dot(x, w):
    return _bd_mm(x, w, "bd_fwd")


def _bd_fwd(x, w):
    return _bd_mm(x, w, "bd_fwd"), (x, w)


def _bd_bwd(res, g):
    x, w = res
    return _bd_mm(g, jnp.swapaxes(w, 1, 2), "bd_dgrad"), _bd_wgrad(x, g, w.shape[0], "bd_wgrad")


bd_dot.defvjp(_bd_fwd, _bd_bwd)


def _nm_tiles(t, seg_rows):
    tm = _tile(math.gcd(t, seg_rows), 256, SUBLANE)
    return tm, seg_rows // tm


def _nm_fwd_call(x, g, shift, scale, seg_rows):
    t, d = x.shape
    tm, seg_blocks = _nm_tiles(t, seg_rows)
    nseg = shift.shape[0]

    def seg(i):
        return jnp.minimum(i // seg_blocks, nseg - 1)

    def body(x_ref, g_ref, sh_ref, sc_ref, o_ref):
        xv = x_ref[...]
        y = xv * lax.rsqrt(jnp.mean(xv * xv, axis=-1, keepdims=True) + NORM_EPS)
        o_ref[...] = (y * g_ref[...]) * (1.0 + sc_ref[0]) + sh_ref[0]

    return pl.pallas_call(
        body, name="norm_mod_fwd", grid=(t // tm,),
        in_specs=[pl.BlockSpec((tm, d), lambda i: (i, 0)), pl.BlockSpec((1, d), lambda i: (0, 0)),
                  pl.BlockSpec((1, 1, d), lambda i: (seg(i), 0, 0)), pl.BlockSpec((1, 1, d), lambda i: (seg(i), 0, 0))],
        out_specs=pl.BlockSpec((tm, d), lambda i: (i, 0)),
        out_shape=jax.ShapeDtypeStruct((t, d), F32),
        compiler_params=_cparams(("parallel",)),
    )(x, g.reshape(1, d), shift.reshape(nseg, 1, d), scale.reshape(nseg, 1, d))


def _nm_bwd_call(x, g, scale, dy, seg_rows):
    t, d = x.shape
    tm, seg_blocks = _nm_tiles(t, seg_rows)
    nseg = scale.shape[0]

    def seg(i):
        return jnp.minimum(i // seg_blocks, nseg - 1)

    def body(x_ref, g_ref, sc_ref, dy_ref, dx_ref, dg_ref, dsh_ref, dsc_ref):
        i = pl.program_id(0)

        @pl.when(i == 0)
        def _():
            dg_ref[...] = jnp.zeros_like(dg_ref)

        @pl.when((i == 0) | (i == seg_blocks))
        def _():
            dsh_ref[...] = jnp.zeros_like(dsh_ref)
            dsc_ref[...] = jnp.zeros_like(dsc_ref)

        xv = x_ref[...]
        dyv = dy_ref[...]
        gv = g_ref[...]
        rstd = lax.rsqrt(jnp.mean(xv * xv, axis=-1, keepdims=True) + NORM_EPS)
        xhat = xv * rstd
        dsh_ref[0] += jnp.sum(dyv, axis=0, keepdims=True)
        dsc_ref[0] += jnp.sum(dyv * (xhat * gv), axis=0, keepdims=True)
        dn = dyv * (1.0 + sc_ref[0])
        dg_ref[...] += jnp.sum(dn * xhat, axis=0, keepdims=True)
        dxh = dn * gv
        dx_ref[...] = rstd * (dxh - xhat * jnp.mean(dxh * xhat, axis=-1, keepdims=True))

    return pl.pallas_call(
        body, name="norm_mod_bwd", grid=(t // tm,),
        in_specs=[pl.BlockSpec((tm, d), lambda i: (i, 0)), pl.BlockSpec((1, d), lambda i: (0, 0)),
                  pl.BlockSpec((1, 1, d), lambda i: (seg(i), 0, 0)), pl.BlockSpec((tm, d), lambda i: (i, 0))],
        out_specs=[pl.BlockSpec((tm, d), lambda i: (i, 0)), pl.BlockSpec((1, d), lambda i: (0, 0)),
                   pl.BlockSpec((1, 1, d), lambda i: (seg(i), 0, 0)), pl.BlockSpec((1, 1, d), lambda i: (seg(i), 0, 0))],
        out_shape=[jax.ShapeDtypeStruct((t, d), F32), jax.ShapeDtypeStruct((1, d), F32),
                   jax.ShapeDtypeStruct((nseg, 1, d), F32), jax.ShapeDtypeStruct((nseg, 1, d), F32)],
        compiler_params=_cparams(("arbitrary",)),
    )(x, g.reshape(1, d), scale.reshape(nseg, 1, d), dy)


@functools.partial(jax.custom_vjp, nondiff_argnums=(4,))
def norm_mod(x, g, shift, scale, seg_rows):
    return _nm_fwd_call(x, g, shift, scale, seg_rows)


def _norm_mod_fwd(x, g, shift, scale, seg_rows):
    return _nm_fwd_call(x, g, shift, scale, seg_rows), (x, g, scale)


def _norm_mod_bwd(seg_rows, res, dy):
    x, g, scale = res
    dx, dg, dsh, dsc = _nm_bwd_call(x, g, scale, dy, seg_rows)
    return dx, dg.reshape(g.shape), dsh.reshape(scale.shape), dsc.reshape(scale.shape)


norm_mod.defvjp(_norm_mod_fwd, _norm_mod_bwd)


def rmsnorm_p(x, g):
    z = jnp.zeros((1, x.shape[1]), F32)
    return norm_mod(x, g, z, z, x.shape[0])


def _linrec_call(a, b, n_lat, mode, name):
    t, c = a.shape
    tb = _tile(math.gcd(t, n_lat), 512, SUBLANE)
    nb, nl = t // tb, n_lat // tb
    reverse = mode in ("B", "C")
    tiles = tb // SUBLANE

    def block(i):
        if mode == "A":
            return (i + nl) % nb
        if mode == "B":
            return nb - 1 - i
        if mode == "C":
            return jnp.where(i < nl, nl - 1 - i, nb - 1 - (i - nl))
        return i

    def body(a_ref, b_ref, h_ref, carry_ref):
        @pl.when(pl.program_id(0) == 0)
        def _():
            carry_ref[...] = jnp.zeros_like(carry_ref)

        def tile(i, carry):
            r = pl.multiple_of((tiles - 1 - i if reverse else i) * SUBLANE, SUBLANE)
            av = a_ref[pl.ds(r, SUBLANE), :]
            bv = b_ref[pl.ds(r, SUBLANE), :]
            row = lax.broadcasted_iota(jnp.int32, av.shape, 0)
            d = 1
            while d < SUBLANE:
                shift = SUBLANE - d if reverse else d
                a_sh = pltpu.roll(av, shift, 0)
                b_sh = pltpu.roll(bv, shift, 0)
                m = (row < SUBLANE - d) if reverse else (row >= d)
                bv = jnp.where(m, av * b_sh + bv, bv)
                av = jnp.where(m, av * a_sh, av)
                d *= 2
            hv = av * carry + bv
            h_ref[pl.ds(r, SUBLANE), :] = hv
            return jnp.sum(jnp.where(row == (0 if reverse else SUBLANE - 1), hv, 0.0), axis=0, keepdims=True)

        carry_ref[...] = lax.fori_loop(0, tiles, tile, carry_ref[...])

    spec = pl.BlockSpec((tb, c), lambda i: (block(i), 0))
    return pl.pallas_call(
        body, name=name, grid=(nb,), in_specs=[spec, spec], out_specs=spec,
        out_shape=jax.ShapeDtypeStruct((t, c), F32), scratch_shapes=[pltpu.VMEM((1, c), F32)],
        compiler_params=_cparams(("arbitrary",)),
    )(a, b)


@functools.partial(jax.custom_vjp, nondiff_argnums=(2, 3))
def linrec(a, b, n_lat, direction):
    return _linrec_call(a, b, n_lat, "AB"[direction], "linrec_fwd")


def _linrec_fwd(a, b, n_lat, direction):
    h = _linrec_call(a, b, n_lat, "AB"[direction], "linrec_fwd")
    return h, (a, h)


def _linrec_bwd(n_lat, direction, res, dh):
    a, h = res
    zero = jnp.zeros_like(a[:1])
    if direction == 0:
        a_next = jnp.concatenate([a[1:n_lat], zero, a[n_lat + 1:], a[:1]], axis=0)
        h_prev = jnp.concatenate([h[-1:], h[:n_lat - 1], zero, h[n_lat:-1]], axis=0)
    else:
        a_next = jnp.concatenate([zero, a[:-1]], axis=0)
        h_prev = jnp.concatenate([h[1:], zero], axis=0)
    g = _linrec_call(a_next, dh, n_lat, "CD"[direction], "linrec_bwd")
    return g * h_prev, g


linrec.defvjp(_linrec_fwd, _linrec_bwd)


def _attn_fwd_call(q, k, v, scale):
    h, lq, dq = q.shape
    lk, dv = v.shape[1], v.shape[2]
    tq = _tile(lq, 256, SUBLANE)

    def body(q_ref, k_ref, v_ref, o_ref, lse_ref, k_sc, v_sc):
        @pl.when(pl.program_id(1) == 0)
        def _():
            k_sc[...] = k_ref[0].astype(BF16)
            v_sc[...] = v_ref[0].astype(BF16)

        s = lax.dot_general(q_ref[0].astype(BF16), k_sc[...], (((1,), (1,)), ((), ())), preferred_element_type=F32) * scale
        m = jnp.max(s, axis=-1, keepdims=True)
        p = jnp.exp(s - m)
        l = jnp.sum(p, axis=-1, keepdims=True)
        o_ref[0] = jnp.dot(p.astype(BF16), v_sc[...], preferred_element_type=F32) / l
        lse_ref[0] = m + jnp.log(l)

    return pl.pallas_call(
        body, name="attn_fwd", grid=(h, lq // tq),
        in_specs=[pl.BlockSpec((1, tq, dq), lambda a, i: (a, i, 0)), pl.BlockSpec((1, lk, dq), lambda a, i: (a, 0, 0)),
                  pl.BlockSpec((1, lk, dv), lambda a, i: (a, 0, 0))],
        out_specs=[pl.BlockSpec((1, tq, dv), lambda a, i: (a, i, 0)), pl.BlockSpec((1, tq, 1), lambda a, i: (a, i, 0))],
        out_shape=[jax.ShapeDtypeStruct((h, lq, dv), F32), jax.ShapeDtypeStruct((h, lq, 1), F32)],
        scratch_shapes=[pltpu.VMEM((lk, dq), BF16), pltpu.VMEM((lk, dv), BF16)],
        compiler_params=_cparams(("parallel", "arbitrary")),
    )(q, k, v)


def _attn_dq_call(q, k, v, do, lse, delta, scale):
    h, lq, dq = q.shape
    lk, dv = v.shape[1], v.shape[2]
    tq = _tile(lq, 256, SUBLANE)

    def body(q_ref, k_ref, v_ref, do_ref, lse_ref, dl_ref, dq_ref, k_sc, v_sc):
        @pl.when(pl.program_id(1) == 0)
        def _():
            k_sc[...] = k_ref[0].astype(BF16)
            v_sc[...] = v_ref[0].astype(BF16)

        kb = k_sc[...]
        s = lax.dot_general(q_ref[0].astype(BF16), kb, (((1,), (1,)), ((), ())), preferred_element_type=F32) * scale
        p = jnp.exp(s - lse_ref[0])
        dp = lax.dot_general(do_ref[0].astype(BF16), v_sc[...], (((1,), (1,)), ((), ())), preferred_element_type=F32)
        ds = p * (dp - dl_ref[0])
        dq_ref[0] = jnp.dot(ds.astype(BF16), kb, preferred_element_type=F32) * scale

    return pl.pallas_call(
        body, name="attn_dq", grid=(h, lq // tq),
        in_specs=[pl.BlockSpec((1, tq, dq), lambda a, i: (a, i, 0)), pl.BlockSpec((1, lk, dq), lambda a, i: (a, 0, 0)),
                  pl.BlockSpec((1, lk, dv), lambda a, i: (a, 0, 0)), pl.BlockSpec((1, tq, dv), lambda a, i: (a, i, 0)),
                  pl.BlockSpec((1, tq, 1), lambda a, i: (a, i, 0)), pl.BlockSpec((1, tq, 1), lambda a, i: (a, i, 0))],
        out_specs=pl.BlockSpec((1, tq, dq), lambda a, i: (a, i, 0)),
        out_shape=jax.ShapeDtypeStruct((h, lq, dq), F32),
        scratch_shapes=[pltpu.VMEM((lk, dq), BF16), pltpu.VMEM((lk, dv), BF16)],
        compiler_params=_cparams(("parallel", "arbitrary")),
    )(q, k, v, do, lse, delta)


def _attn_dkv_call(q, k, v, do, lse_row, delta_row, scale):
    h, lq, dq = q.shape
    lk, dv = v.shape[1], v.shape[2]
    tk = _tile(lk, 256, SUBLANE)

    def body(q_ref, k_ref, v_ref, do_ref, lse_ref, dl_ref, dk_ref, dv_ref, q_sc, do_sc):
        @pl.when(pl.program_id(1) == 0)
        def _():
            q_sc[...] = q_ref[0].astype(BF16)
            do_sc[...] = do_ref[0].astype(BF16)

        qb = q_sc[...]
        dob = do_sc[...]
        st = lax.dot_general(k_ref[0].astype(BF16), qb, (((1,), (1,)), ((), ())), preferred_element_type=F32) * scale
        pt = jnp.exp(st - lse_ref[0])
        dpt = lax.dot_general(v_ref[0].astype(BF16), dob, (((1,), (1,)), ((), ())), preferred_element_type=F32)
        dst = pt * (dpt - dl_ref[0])
        dk_ref[0] = jnp.dot(dst.astype(BF16), qb, preferred_element_type=F32) * scale
        dv_ref[0] = jnp.dot(pt.astype(BF16), dob, preferred_element_type=F32)

    return pl.pallas_call(
        body, name="attn_dkv", grid=(h, lk // tk),
        in_specs=[pl.BlockSpec((1, lq, dq), lambda a, i: (a, 0, 0)), pl.BlockSpec((1, tk, dq), lambda a, i: (a, i, 0)),
                  pl.BlockSpec((1, tk, dv), lambda a, i: (a, i, 0)), pl.BlockSpec((1, lq, dv), lambda a, i: (a, 0, 0)),
                  pl.BlockSpec((1, 1, lq), lambda a, i: (a, 0, 0)), pl.BlockSpec((1, 1, lq), lambda a, i: (a, 0, 0))],
        out_specs=[pl.BlockSpec((1, tk, dq), lambda a, i: (a, i, 0)), pl.BlockSpec((1, tk, dv), lambda a, i: (a, i, 0))],
        out_shape=[jax.ShapeDtypeStruct((h, lk, dq), F32), jax.ShapeDtypeStruct((h, lk, dv), F32)],
        scratch_shapes=[pltpu.VMEM((lq, dq), BF16), pltpu.VMEM((lq, dv), BF16)],
        compiler_params=_cparams(("parallel", "arbitrary")),
    )(q, k, v, do, lse_row, delta_row)


@jax.custom_vjp
def attention(q, k, v):
    return _attn_fwd_call(q, k, v, q.shape[-1] ** -0.5)[0]


def _attention_fwd(q, k, v):
    o, lse = _attn_fwd_call(q, k, v, q.shape[-1] ** -0.5)
    return o, (q, k, v, o, lse)


def _attention_bwd(res, do):
    q, k, v, o, lse = res
    h, lq, _ = q.shape
    scale = q.shape[-1] ** -0.5
    delta = jnp.sum(do * o, axis=-1, keepdims=True)
    dq = _attn_dq_call(q, k, v, do, lse, delta, scale)
    dk, dv = _attn_dkv_call(q, k, v, do, lse.reshape(h, 1, lq), delta.reshape(h, 1, lq), scale)
    return dq, dk, dv


attention.defvjp(_attention_fwd, _attention_bwd)


def _na_start(r, rows):
    return jnp.clip(r - NA_WIN_H // 2, 0, rows - NA_WIN_H)


NA_HEAD_BLOCK = 2


def _na_head_views(refs, hh, hd):
    return [t.at[:, pl.ds(hh * hd, hd)] if len(t.shape) == 2 else t.at[pl.ds(hh, 1)] for t in refs]


def _na_fwd_call(q, k, v, kc, vc, bias):
    n, width = q.shape
    heads, hd = width // NA_HEAD_DIM, NA_HEAD_DIM
    hb = math.gcd(heads, NA_HEAD_BLOCK)
    rows, nwin, nctx = n // GRID_W, NA_WIN_H * GRID_W, kc.shape[0]
    scale = hd ** -0.5
    nt = (((1,), (1,)), ((), ()))

    def one_head(q_ref, k_ref, v_ref, kc_ref, vc_ref, b_ref, o_ref, lse_ref):
        r = pl.program_id(1)
        start = pl.multiple_of(_na_start(r, rows) * GRID_W, GRID_W)
        qb = q_ref[...].astype(BF16)
        kw = k_ref[pl.ds(start, nwin), :].astype(BF16)
        vw = v_ref[pl.ds(start, nwin), :].astype(BF16)
        sw = lax.dot_general(qb, kw, nt, preferred_element_type=F32) * scale + b_ref[0, 0]
        sc = lax.dot_general(qb, kc_ref[...].astype(BF16), nt, preferred_element_type=F32) * scale
        m = jnp.maximum(jnp.max(sw, axis=-1, keepdims=True), jnp.max(sc, axis=-1, keepdims=True))
        pw = jnp.exp(sw - m)
        pc = jnp.exp(sc - m)
        l = jnp.sum(pw, axis=-1, keepdims=True) + jnp.sum(pc, axis=-1, keepdims=True)
        pw = pw / l
        pc = pc / l
        o_ref[...] = (jnp.dot(pw.astype(BF16), vw, preferred_element_type=F32)
                      + jnp.dot(pc.astype(BF16), vc_ref[...].astype(BF16), preferred_element_type=F32))
        lse_ref[0] = m + jnp.log(l)

    def body(*refs):
        for hh in range(hb):
            one_head(*_na_head_views(refs, hh, hd))

    full = lambda rws: pl.BlockSpec((rws, hb * hd), lambda a, r: (0, a))
    return pl.pallas_call(
        body, name="natten_fwd", grid=(heads // hb, rows),
        in_specs=[pl.BlockSpec((GRID_W, hb * hd), lambda a, r: (r, a)), full(n), full(n), full(nctx), full(nctx),
                  pl.BlockSpec((hb, 1, GRID_W, nwin), lambda a, r: (a, r - _na_start(r, rows), 0, 0))],
        out_specs=[pl.BlockSpec((GRID_W, hb * hd), lambda a, r: (r, a)), pl.BlockSpec((hb, GRID_W, 1), lambda a, r: (a, r, 0))],
        out_shape=[jax.ShapeDtypeStruct((n, width), F32), jax.ShapeDtypeStruct((heads, n, 1), F32)],
        compiler_params=_cparams(("parallel", "arbitrary")),
    )(q, k, v, kc, vc, bias)


def _na_bwd_call(q, k, v, kc, vc, bias, bias_t, do, lse, delta, lse_row, delta_row):
    n, width = q.shape
    heads, hd = width // NA_HEAD_DIM, NA_HEAD_DIM
    hb = math.gcd(heads, NA_HEAD_BLOCK)
    rows, nwin, nctx = n // GRID_W, NA_WIN_H * GRID_W, kc.shape[0]
    scale = hd ** -0.5
    nt = (((1,), (1,)), ((), ()))

    def one_head(q_ref, k_ref, v_ref, kc_ref, vc_ref, b_ref, bt_ref, do_ref, lse_ref, dl_ref, lser_ref, dlr_ref,
                 dq_ref, dk_ref, dv_ref, dkc_ref, dvc_ref, db_ref):
        r = pl.program_id(1)
        st = _na_start(r, rows)
        start = pl.multiple_of(st * GRID_W, GRID_W)

        @pl.when(r == 0)
        def _():
            dk_ref[...] = jnp.zeros_like(dk_ref)
            dv_ref[...] = jnp.zeros_like(dv_ref)
            dkc_ref[...] = jnp.zeros_like(dkc_ref)
            dvc_ref[...] = jnp.zeros_like(dvc_ref)

        @pl.when((r <= NA_WIN_H // 2) | (r > rows - NA_WIN_H // 2))
        def _():
            db_ref[...] = jnp.zeros_like(db_ref)

        qb = q_ref[...].astype(BF16)
        dob = do_ref[...].astype(BF16)
        kw = k_ref[pl.ds(start, nwin), :].astype(BF16)
        vw = v_ref[pl.ds(start, nwin), :].astype(BF16)
        kcb = kc_ref[...].astype(BF16)
        vcb = vc_ref[...].astype(BF16)
        lse_c, dl_c = lse_ref[0], dl_ref[0]
        pw = jnp.exp(lax.dot_general(qb, kw, nt, preferred_element_type=F32) * scale + b_ref[0, 0] - lse_c)
        pc = jnp.exp(lax.dot_general(qb, kcb, nt, preferred_element_type=F32) * scale - lse_c)
        dsw = pw * (lax.dot_general(dob, vw, nt, preferred_element_type=F32) - dl_c)
        dsc = pc * (lax.dot_general(dob, vcb, nt, preferred_element_type=F32) - dl_c)
        db_ref[0, 0] += dsw
        dq_ref[...] = (jnp.dot(dsw.astype(BF16), kw, preferred_element_type=F32)
                       + jnp.dot(dsc.astype(BF16), kcb, preferred_element_type=F32)) * scale
        lse_r, dl_r = lser_ref[0, 0], dlr_ref[0, 0]
        pwt = jnp.exp(lax.dot_general(kw, qb, nt, preferred_element_type=F32) * scale + bt_ref[0, 0] - lse_r)
        pct = jnp.exp(lax.dot_general(kcb, qb, nt, preferred_element_type=F32) * scale - lse_r)
        dswt = pwt * (lax.dot_general(vw, dob, nt, preferred_element_type=F32) - dl_r)
        dsct = pct * (lax.dot_general(vcb, dob, nt, preferred_element_type=F32) - dl_r)
        dk_ref[pl.ds(start, nwin), :] += jnp.dot(dswt.astype(BF16), qb, preferred_element_type=F32) * scale
        dv_ref[pl.ds(start, nwin), :] += jnp.dot(pwt.astype(BF16), dob, preferred_element_type=F32)
        dkc_ref[...] += jnp.dot(dsct.astype(BF16), qb, preferred_element_type=F32) * scale
        dvc_ref[...] += jnp.dot(pct.astype(BF16), dob, preferred_element_type=F32)

    def body(*refs):
        for hh in range(hb):
            one_head(*_na_head_views(refs, hh, hd))

    full = lambda rws: pl.BlockSpec((rws, hb * hd), lambda a, r: (0, a))
    tile = pl.BlockSpec((GRID_W, hb * hd), lambda a, r: (r, a))
    pat = lambda r: r - _na_start(r, rows)
    col = pl.BlockSpec((hb, GRID_W, 1), lambda a, r: (a, r, 0))
    rowv = pl.BlockSpec((hb, 1, 1, GRID_W), lambda a, r: (a, r, 0, 0))
    return pl.pallas_call(
        body, name="natten_bwd", grid=(heads // hb, rows),
        in_specs=[tile, full(n), full(n), full(nctx), full(nctx),
                  pl.BlockSpec((hb, 1, GRID_W, nwin), lambda a, r: (a, pat(r), 0, 0)),
                  pl.BlockSpec((hb, 1, nwin, GRID_W), lambda a, r: (a, pat(r), 0, 0)),
                  tile, col, col, rowv, rowv],
        out_specs=[tile, full(n), full(n), full(nctx), full(nctx),
                   pl.BlockSpec((hb, 1, GRID_W, nwin), lambda a, r: (a, pat(r), 0, 0))],
        out_shape=[jax.ShapeDtypeStruct((n, width), F32), jax.ShapeDtypeStruct((n, width), F32),
                   jax.ShapeDtypeStruct((n, width), F32), jax.ShapeDtypeStruct((nctx, width), F32),
                   jax.ShapeDtypeStruct((nctx, width), F32), jax.ShapeDtypeStruct(bias.shape, F32)],
        compiler_params=_cparams(("parallel", "arbitrary")),
    )(q, k, v, kc, vc, bias, bias_t, do, lse, delta, lse_row, delta_row)


@jax.custom_vjp
def natten(q, k, v, kc, vc, bias):
    return _na_fwd_call(q, k, v, kc, vc, bias)[0]


def _natten_fwd(q, k, v, kc, vc, bias):
    o, lse = _na_fwd_call(q, k, v, kc, vc, bias)
    return o, (q, k, v, kc, vc, bias, o, lse)


def _natten_bwd(res, do):
    q, k, v, kc, vc, bias, o, lse = res
    n, width = q.shape
    heads, rows = width // NA_HEAD_DIM, n // GRID_W
    delta = jnp.sum((do * o).reshape(n, heads, NA_HEAD_DIM), axis=-1).T.reshape(heads, n, 1)
    return tuple(_na_bwd_call(q, k, v, kc, vc, bias, jnp.swapaxes(bias, 2, 3), do, lse, delta,
                              lse.reshape(heads, rows, 1, GRID_W), delta.reshape(heads, rows, 1, GRID_W)))


natten.defvjp(_natten_fwd, _natten_bwd)


def na_bias_patterns(rpb):
    heads = rpb.shape[0]
    pid = np.arange(NA_WIN_H)[:, None]
    j = np.arange(NA_WIN_H)[None, :]
    row_idx = j - pid + (NA_WIN_H - 1)
    row_hot = (row_idx[..., None] == np.arange(2 * NA_WIN_H - 1)).astype(np.float32)
    cidx = np.arange(GRID_W)
    c_start = np.clip(cidx - NA_WIN_W // 2, 0, GRID_W - NA_WIN_W)
    in_win = (cidx[None, :] >= c_start[:, None]) & (cidx[None, :] < c_start[:, None] + NA_WIN_W)
    col_idx = np.clip(cidx[None, :] - cidx[:, None], -(NA_WIN_W - 1), NA_WIN_W - 1) + (NA_WIN_W - 1)
    col_hot = (col_idx[..., None] == np.arange(2 * NA_WIN_W - 1)).astype(np.float32)
    hi = lax.Precision.HIGHEST
    tmp = jnp.einsum("hab,pja->hpjb", rpb, jnp.asarray(row_hot), precision=hi)
    bias = jnp.einsum("hpjb,qkb->hpqjk", tmp, jnp.asarray(col_hot), precision=hi)
    bias = jnp.where(jnp.asarray(in_win)[None, None, :, None, :], bias, NEG_INF)
    return bias.reshape(heads, NA_WIN_H, GRID_W, NA_WIN_H * GRID_W)


FFN_COLS = 128
FFN_HALO = SUBLANE


def _ffn_chunks(t):
    ch = _tile(t, 544, SUBLANE)
    ext = min(t, ch + 2 * FFN_HALO)
    return [(r0, ch, min(max(r0 - FFN_HALO, 0), t - ext), ext) for r0 in range(0, t, ch)]


def _ffn_conv_parts(x, start, n_lat, t):
    ext = x.shape[0]
    row = start + lax.broadcasted_iota(jnp.int32, x.shape, 0)
    no_prev = (row == 0) | (row == n_lat)
    no_next = (row == n_lat - 1) | (row == t - 1)
    return jnp.where(no_prev, 0.0, pltpu.roll(x, 1, 0)), jnp.where(no_next, 0.0, pltpu.roll(x, ext - 1, 0)), no_prev, no_next


def _sigmoid(x):
    return 1.0 / (1.0 + jnp.exp(-x))


def _ffn_act_fwd_call(uv, ug, cwv, cwg, cbv, cbg, n_lat):
    t, f = uv.shape
    tc = _tile(f, FFN_COLS, LANE)
    chunks = _ffn_chunks(t)

    def body(xv_ref, xg_ref, wv_ref, wg_ref, bv_ref, bg_ref, o_ref):
        def conv(x, start, w_ref, b_ref):
            prev, nxt, _, _ = _ffn_conv_parts(x, start, n_lat, t)
            return prev * w_ref[0:1, :] + x * w_ref[1:2, :] + nxt * w_ref[2:3, :] + b_ref[...]

        for r0, ch, start, ext in chunks:
            cv = conv(xv_ref[start:start + ext, :], start, wv_ref, bv_ref)
            cg = conv(xg_ref[start:start + ext, :], start, wg_ref, bg_ref)
            act = (cg * _sigmoid(cg)) * cv
            o_ref[r0:r0 + ch, :] = act[r0 - start:r0 - start + ch, :]

    col = pl.BlockSpec((t, tc), lambda j: (0, j))
    wsp = pl.BlockSpec((FFN_CONV_W, tc), lambda j: (0, j))
    bsp = pl.BlockSpec((1, tc), lambda j: (0, j))
    return pl.pallas_call(
        body, name="ffn_act_fwd", grid=(f // tc,), in_specs=[col, col, wsp, wsp, bsp, bsp], out_specs=col,
        out_shape=jax.ShapeDtypeStruct((t, f), F32), compiler_params=_cparams(("parallel",)),
    )(uv, ug, cwv, cwg, cbv.reshape(1, f), cbg.reshape(1, f))


def _ffn_act_bwd_call(uv, ug, cwv, cwg, cbv, cbg, dact, n_lat):
    t, f = uv.shape
    tc = _tile(f, FFN_COLS, LANE)
    chunks = _ffn_chunks(t)

    def body(xv_ref, xg_ref, wv_ref, wg_ref, bv_ref, bg_ref, da_ref, dxv_ref, dxg_ref, dw_ref):
        dw_ref[...] = jnp.zeros_like(dw_ref)
        for r0, ch, start, ext in chunks:
            lo = r0 - start
            xv = xv_ref[start:start + ext, :]
            xg = xg_ref[start:start + ext, :]
            da = da_ref[start:start + ext, :]
            pv, nv, no_prev, no_next = _ffn_conv_parts(xv, start, n_lat, t)
            pg, ng, _, _ = _ffn_conv_parts(xg, start, n_lat, t)
            cv = pv * wv_ref[0:1, :] + xv * wv_ref[1:2, :] + nv * wv_ref[2:3, :] + bv_ref[...]
            cg = pg * wg_ref[0:1, :] + xg * wg_ref[1:2, :] + ng * wg_ref[2:3, :] + bg_ref[...]
            sig = _sigmoid(cg)
            dcv = da * (cg * sig)
            dcg = da * cv * (sig * (1.0 + cg * (1.0 - sig)))
            for base, w_ref, dc, parts, dx_ref in ((0, wv_ref, dcv, (pv, xv, nv), dxv_ref), (4, wg_ref, dcg, (pg, xg, ng), dxg_ref)):
                dc_next = jnp.where(no_next, 0.0, pltpu.roll(dc, ext - 1, 0))
                dc_prev = jnp.where(no_prev, 0.0, pltpu.roll(dc, 1, 0))
                dx = dc_next * w_ref[0:1, :] + dc * w_ref[1:2, :] + dc_prev * w_ref[2:3, :]
                dx_ref[r0:r0 + ch, :] = dx[lo:lo + ch, :]
                dci = dc[lo:lo + ch, :]
                for k, part in enumerate(parts):
                    dw_ref[base + k:base + k + 1, :] += jnp.sum(dci * part[lo:lo + ch, :], axis=0, keepdims=True)
                dw_ref[base + 3:base + 4, :] += jnp.sum(dci, axis=0, keepdims=True)

    col = pl.BlockSpec((t, tc), lambda j: (0, j))
    wsp = pl.BlockSpec((FFN_CONV_W, tc), lambda j: (0, j))
    bsp = pl.BlockSpec((1, tc), lambda j: (0, j))
    return pl.pallas_call(
        body, name="ffn_act_bwd", grid=(f // tc,), in_specs=[col, col, wsp, wsp, bsp, bsp, col],
        out_specs=[col, col, pl.BlockSpec((2 * (FFN_CONV_W + 1), tc), lambda j: (0, j))],
        out_shape=[jax.ShapeDtypeStruct((t, f), F32), jax.ShapeDtypeStruct((t, f), F32),
                   jax.ShapeDtypeStruct((2 * (FFN_CONV_W + 1), f), F32)],
        compiler_params=_cparams(("parallel",)),
    )(uv, ug, cwv, cwg, cbv.reshape(1, f), cbg.reshape(1, f), dact)


@functools.partial(jax.custom_vjp, nondiff_argnums=(6,))
def ffn_act(uv, ug, cwv, cwg, cbv, cbg, n_lat):
    return _ffn_act_fwd_call(uv, ug, cwv, cwg, cbv, cbg, n_lat)


def _ffn_act_fwd(uv, ug, cwv, cwg, cbv, cbg, n_lat):
    return _ffn_act_fwd_call(uv, ug, cwv, cwg, cbv, cbg, n_lat), (uv, ug, cwv, cwg, cbv, cbg)


def _ffn_act_bwd(n_lat, res, dact):
    duv, dug, dw = _ffn_act_bwd_call(*res, dact, n_lat)
    nw = FFN_CONV_W
    return duv, dug, dw[:nw], dw[nw + 1:2 * nw + 1], dw[nw], dw[2 * nw + 1]


ffn_act.defvjp(_ffn_act_fwd, _ffn_act_bwd)


def adamw(w, g, m, v):
    shape = w.shape
    last = shape[-1]
    r = int(np.prod(shape[:-1]))
    w2, g2, m2, v2 = (t.reshape(1, r, last) for t in (w, g, m, v))
    tr = _tile(r, max(SUBLANE, (1 << 18) // last), SUBLANE)

    def body(w_ref, g_ref, m_ref, v_ref, d_ref, mo_ref, vo_ref):
        gv = g_ref[...]
        mn = ADAM_B1 * m_ref[...] + (1.0 - ADAM_B1) * gv
        vn = ADAM_B2 * v_ref[...] + (1.0 - ADAM_B2) * (gv * gv)
        m_hat = mn / (1.0 - ADAM_B1 ** ADAM_STEP)
        v_hat = vn / (1.0 - ADAM_B2 ** ADAM_STEP)
        d_ref[...] = -ADAM_LR * (m_hat / (jnp.sqrt(v_hat) + ADAM_EPS) + ADAM_WD * w_ref[...])
        mo_ref[...] = mn
        vo_ref[...] = vn

    spec = pl.BlockSpec((1, tr, last), lambda i: (0, i, 0))
    out = pl.pallas_call(
        body, name="adamw", grid=(r // tr,), in_specs=[spec] * 4, out_specs=[spec] * 3,
        out_shape=[jax.ShapeDtypeStruct((1, r, last), F32)] * 3, compiler_params=_cparams(("parallel",)),
    )(w2, g2, m2, v2)
    return tuple(t.reshape(shape) for t in out)


def _place():
    x, y, c = lax.axis_index("x"), lax.axis_index("y"), lax.axis_index("c")
    chips = [(1 - x, y), (x, 1 - y), (1 - x, 1 - y)]
    return x, y, c, chips


HBM_SPEC = pl.BlockSpec(memory_space=pltpu.HBM)


def all_gather_chips(xs, name):
    def body(x_ref, out_ref, send_sems, recv_sems):
        x, y, c, chips = _place()
        sibling = (x, y, 1 - c)

        def piece(chip, half):
            return out_ref.at[2 * chip[0] + chip[1], half]

        def copy(k, chip, half, to, src=None):
            dst = piece(chip, half)
            return pltpu.make_async_remote_copy(src_ref=dst if src is None else src, dst_ref=dst, send_sem=send_sems.at[k],
                                                recv_sem=recv_sems.at[k], device_id=to, device_id_type=MESH)

        first = [copy(k, (x, y), c, (*chip, c), src=x_ref.at[c]) for k, chip in enumerate(chips)]
        for cp in first:
            cp.start()
        passed = [copy(3 + k, chip, c, sibling) for k, chip in enumerate(chips)]
        for k, chip in enumerate(chips):
            copy(k, chip, c, sibling).wait_recv()
            passed[k].start()
        for k, chip in enumerate(chips):
            copy(3 + k, chip, 1 - c, sibling).wait_recv()
        for cp in first + passed:
            cp.wait_send()

    out = pl.pallas_call(
        body, name=name, in_specs=[HBM_SPEC], out_specs=HBM_SPEC,
        out_shape=jax.ShapeDtypeStruct((N_CHIPS,) + xs.shape, xs.dtype),
        scratch_shapes=[pltpu.SemaphoreType.DMA((6,)), pltpu.SemaphoreType.DMA((6,))],
        compiler_params=pltpu.CompilerParams(has_side_effects=True),
    )(xs)
    me = 2 * lax.axis_index("x") + lax.axis_index("y")
    return lax.dynamic_update_slice(out, xs[None], (me,) + (0,) * xs.ndim)


def _rs_to_sibling(g, name):
    _, _, r, b = g.shape

    def body(g_ref, out_ref, send_sems, recv_sems):
        x, y, c, _ = _place()
        cps = [pltpu.make_async_remote_copy(src_ref=g_ref.at[s, 1 - c], dst_ref=out_ref.at[s], send_sem=send_sems.at[s],
                                            recv_sem=recv_sems.at[s], device_id=(x, y, 1 - c), device_id_type=MESH)
               for s in range(N_CHIPS)]
        for cp in cps:
            cp.start()
        for cp in cps:
            cp.wait()

    return pl.pallas_call(
        body, name=name, in_specs=[HBM_SPEC], out_specs=HBM_SPEC, out_shape=jax.ShapeDtypeStruct((N_CHIPS, r, b), g.dtype),
        scratch_shapes=[pltpu.SemaphoreType.DMA((N_CHIPS,)), pltpu.SemaphoreType.DMA((N_CHIPS,))],
        compiler_params=pltpu.CompilerParams(has_side_effects=True),
    )(g)


def _rs_chip_sum(g, la, c_idx, name):
    _, _, r, b = g.shape
    tr = _tile(r, max(16, (1 << 18) // b), 16)

    def body(c_ref, g_ref, la_ref, o_ref):
        o_ref[...] = (g_ref[:, 0] + la_ref[...]).astype(BF16)

    return pl.pallas_call(
        body, name=name,
        grid_spec=pltpu.PrefetchScalarGridSpec(
            num_scalar_prefetch=1, grid=(N_CHIPS, r // tr),
            in_specs=[pl.BlockSpec((1, 1, tr, b), lambda s, i, c_ref: (s, c_ref[0], i, 0)),
                      pl.BlockSpec((1, tr, b), lambda s, i, c_ref: (s, i, 0))],
            out_specs=pl.BlockSpec((1, tr, b), lambda s, i, c_ref: (s, i, 0))),
        out_shape=jax.ShapeDtypeStruct((N_CHIPS, r, b), BF16), compiler_params=_cparams(("parallel", "parallel")),
    )(c_idx, g, la)


def _rs_to_owners(hb, name):
    _, r, b = hb.shape

    def body(h_ref, out_ref, send_sems, recv_sems):
        x, y, c, chips = _place()
        cps = [pltpu.make_async_remote_copy(src_ref=h_ref.at[2 * chip[0] + chip[1]], dst_ref=out_ref.at[k],
                                            send_sem=send_sems.at[k], recv_sem=recv_sems.at[k], device_id=(*chip, c),
                                            device_id_type=MESH) for k, chip in enumerate(chips)]
        for cp in cps:
            cp.start()
        for cp in cps:
            cp.wait()

    return pl.pallas_call(
        body, name=name, in_specs=[HBM_SPEC], out_specs=HBM_SPEC, out_shape=jax.ShapeDtypeStruct((3, r, b), hb.dtype),
        scratch_shapes=[pltpu.SemaphoreType.DMA((3,)), pltpu.SemaphoreType.DMA((3,))],
        compiler_params=pltpu.CompilerParams(has_side_effects=True),
    )(hb)


def _rs_final_sum(g, la, lb, sc_idx, name):
    _, _, r, b = g.shape
    tr = _tile(r, max(16, (1 << 18) // b), 16)

    def body(i_ref, g_ref, la_ref, lb_ref, o_ref):
        o_ref[...] = (g_ref[0, 0] + la_ref[0]) + lb_ref[0].astype(F32) + lb_ref[1].astype(F32) + lb_ref[2].astype(F32)

    return pl.pallas_call(
        body, name=name,
        grid_spec=pltpu.PrefetchScalarGridSpec(
            num_scalar_prefetch=1, grid=(r // tr,),
            in_specs=[pl.BlockSpec((1, 1, tr, b), lambda i, i_ref: (i_ref[0], i_ref[1], i, 0)),
                      pl.BlockSpec((1, tr, b), lambda i, i_ref: (i_ref[0], i, 0)),
                      pl.BlockSpec((3, tr, b), lambda i, i_ref: (0, i, 0))],
            out_specs=pl.BlockSpec((tr, b), lambda i, i_ref: (i, 0))),
        out_shape=jax.ShapeDtypeStruct((r, b), F32), compiler_params=_cparams(("parallel",)),
    )(sc_idx, g, la, lb)


def _rs_join_halves(f, name):
    r, b = f.shape

    def body(f_ref, out_ref, send_sem, recv_sem):
        x, y, c, _ = _place()
        cp = pltpu.make_async_remote_copy(src_ref=f_ref, dst_ref=out_ref.at[c], send_sem=send_sem, recv_sem=recv_sem,
                                          device_id=(x, y, 1 - c), device_id_type=MESH)
        cp.start()
        pltpu.make_async_remote_copy(src_ref=f_ref, dst_ref=out_ref.at[1 - c], send_sem=send_sem, recv_sem=recv_sem,
                                     device_id=(x, y, 1 - c), device_id_type=MESH).wait_recv()
        cp.wait_send()

    out = pl.pallas_call(
        body, name=name, in_specs=[HBM_SPEC], out_specs=HBM_SPEC, out_shape=jax.ShapeDtypeStruct((2, r, b), f.dtype),
        scratch_shapes=[pltpu.SemaphoreType.DMA, pltpu.SemaphoreType.DMA],
        compiler_params=pltpu.CompilerParams(has_side_effects=True),
    )(f)
    return lax.dynamic_update_slice(out, f[None], (lax.axis_index("c"), 0, 0))


def reduce_scatter_devices(g, tag):
    c = lax.axis_index("c")
    me = 2 * lax.axis_index("x") + lax.axis_index("y")
    la = _rs_to_sibling(g, "rs_sibling_" + tag)
    hb = _rs_chip_sum(g, la, jnp.stack([c]).astype(jnp.int32), "rs_chipsum_" + tag)
    lb = _rs_to_owners(hb, "rs_owners_" + tag)
    f = _rs_final_sum(g, la, lb, jnp.stack([me, c]).astype(jnp.int32), "rs_final_" + tag)
    return _rs_join_halves(f, "rs_join_" + tag)


def _w_in_layout(d_model, na_width, lru_width, q_rank, kv_rank):
    head = 3 * na_width + 2 * lru_width + q_rank + kv_rank
    total = head + MLA_ROPE_DIM + N_BRANCH * d_model
    padded = -(-total // LANE) * LANE
    return None, head, total, padded


def _col_pieces(shards, a, b):
    n4 = shards[0].shape[1]
    out = []
    for s, t in enumerate(shards):
        lo, hi = max(a, s * n4), min(b, (s + 1) * n4)
        if lo < hi:
            out.append(t[:, lo - s * n4:hi - s * n4])
    return out


def _assemble_w_in(shards, head, total, padded):
    parts = (_col_pieces(shards, 0, head) + _col_pieces(shards, head + MLA_ROPE_DIM, total)
             + _col_pieces(shards, head, head + MLA_ROPE_DIM))
    if padded > total:
        parts.append(jnp.zeros((shards[0].shape[0], padded - total), shards[0].dtype))
    return jnp.concatenate(parts, axis=1)


def _w_in_shard_grad(g, s, n4, head, total):
    n_gates = total - head - MLA_ROPE_DIM
    a, b = s * n4, (s + 1) * n4
    parts = []
    for lo, hi, shift in ((0, head, 0), (head, head + MLA_ROPE_DIM, n_gates), (head + MLA_ROPE_DIM, total, -MLA_ROPE_DIM)):
        lo, hi = max(a, lo), min(b, hi)
        if lo < hi:
            parts.append(g[:, lo + shift:hi + shift])
    return jnp.concatenate(parts, axis=1)


@functools.partial(jax.custom_vjp, nondiff_argnums=(1,))
def split_cols(z, sizes):
    out, o = [], 0
    for sz in sizes:
        out.append(z[:, o:o + sz])
        o += sz
    return tuple(out)


def _split_cols_fwd(z, sizes):
    return split_cols(z, sizes), None


def _split_cols_bwd(sizes, res, cts):
    return (jnp.concatenate(cts, axis=1),)


split_cols.defvjp(_split_cols_fwd, _split_cols_bwd)


def _dwconv(x, w, b):
    width = w.shape[0]
    n = x.shape[0]
    left = width // 2
    xp = jnp.pad(x, ((left, width - 1 - left), (0, 0)))
    return sum(xp[i:i + n] * w[i] for i in range(width)) + b


def _dwconv_seg(x, w, b, n_lat):
    return jnp.concatenate([_dwconv(x[:n_lat], w, b), _dwconv(x[n_lat:], w, b)], axis=0)


def _rope(x, cos, sin):
    half = x.shape[-1] // 2
    x1, x2 = x[..., :half], x[..., half:]
    return jnp.concatenate([x1 * cos - x2 * sin, x1 * sin + x2 * cos], axis=-1)


def _rope_tables(n_lat, n_ctx):
    t = jnp.arange(n_lat, dtype=jnp.int32)
    row = (t // GRID_W).astype(F32)
    col = (t % GRID_W).astype(F32)
    n_freq = MLA_ROPE_DIM // 4
    inv_freq = ROPE_THETA ** (-jnp.arange(n_freq, dtype=F32) / n_freq)
    ang = jnp.concatenate([row[:, None] * inv_freq, col[:, None] * inv_freq], axis=-1)
    ones = jnp.ones((n_ctx, MLA_ROPE_DIM // 2), F32)
    return jnp.concatenate([jnp.cos(ang), ones], axis=0), jnp.concatenate([jnp.sin(ang), 0.0 * ones], axis=0)


def _heads_first(t, heads):
    n = t.shape[0]
    return t.reshape(n, heads, -1).transpose(1, 0, 2)


def _heads_last(t):
    return t.transpose(1, 0, 2).reshape(t.shape[1], -1)


def _lru_coeffs(u, w_a, b_a, w_x, b_x, lam):
    r = jax.nn.sigmoid(bd_dot(u, w_a) + b_a)
    i = jax.nn.sigmoid(bd_dot(u, w_x) + b_x)
    log_a = -LRU_C * r * jax.nn.softplus(-lam)
    return jnp.exp(log_a), jnp.sqrt(-jnp.expm1(2.0 * log_a)) * (i * u)


def _forward_loss(big, prox, small, mod_probe, x, ctx, silu_c, target):
    n_lat, d_model = x.shape
    n_ctx = ctx.shape[0]
    na_width = NA_HEADS * NA_HEAD_DIM
    lru_width = small["lru_conv_b"].shape[1]
    q_rank, kv_rank = small["mla_q_norm"].shape[1], small["mla_kv_norm"].shape[1]
    _, head, total, padded = _w_in_layout(d_model, na_width, lru_width, q_rank, kv_rank)
    cos, sin = _rope_tables(n_lat, n_ctx)
    is_ctx = (jnp.arange(n_lat + n_ctx) >= n_lat)[:, None]

    def by_row(v2):
        return jnp.where(is_ctx, v2[1][None, :], v2[0][None, :])

    s_rows = jnp.concatenate([silu_c, jax.nn.silu(small["c_ctx"])[None, :],
                              jnp.zeros((MOD_ROWS - 2, d_model), F32)], axis=0)
    xs = jnp.concatenate([x, ctx], axis=0)
    for l in range(DEPTH):
        mod = (pdot_act(s_rows, big["w_mod"][l]) + small["b_mod"][l])[:2] + mod_probe[l]
        sh1, sc1, g1, sh2, sc2, g2 = split_cols(mod, (d_model,) * N_MOD)
        h = norm_mod(xs, small["norm_mix"][l], sh1, sc1, n_lat)
        z = pdot(h, big["w_in"][l], prox["w_in"][l])
        names = ("q_a", "k_a", "v_a", "lru_x", "lru_g", "cq", "ckv", "gate_a", "gate_b", "gate_c", "kr", "pad")
        widths = (na_width,) * 3 + (lru_width,) * 2 + (q_rank, kv_rank) + (d_model,) * 3 + (MLA_ROPE_DIM, padded - total)
        cols = dict(zip(names, split_cols(z, widths)))
        q_a, k_a, v_a = cols["q_a"], cols["k_a"], cols["v_a"]
        out_a_lat = natten(q_a[:n_lat], k_a[:n_lat], v_a[:n_lat], k_a[n_lat:], v_a[n_lat:], na_bias_patterns(small["na_rpb"][l]))
        out_a_ctx = _heads_last(attention(*(_heads_first(t[n_lat:], NA_HEADS) for t in (q_a, k_a, v_a))))
        out_a = jnp.concatenate([out_a_lat, out_a_ctx], axis=0)
        u = _dwconv_seg(cols["lru_x"], small["lru_conv_w"][l], small["lru_conv_b"][l], n_lat)
        hs = [linrec(*_lru_coeffs(u, small["lru_w_a"][l, d], small["lru_b_a"][l, d], small["lru_w_x"][l, d],
                                  small["lru_b_x"][l, d], small["lru_lam"][l, d]), n_lat, d) for d in range(2)]
        y_b = hs[0] + hs[1]
        out_b = jax.nn.gelu(cols["lru_g"]) * y_b
        q_m = pdot(rmsnorm_p(cols["cq"], small["mla_q_norm"][l]), big["mla_w_q_up"][l], prox["mla_w_q_up"][l])
        q_m = q_m.reshape(-1, MLA_HEADS, MLA_NOPE_DIM + MLA_ROPE_DIM)
        q_m = jnp.concatenate([q_m[..., :MLA_NOPE_DIM], _rope(q_m[..., MLA_NOPE_DIM:], cos[:, None, :], sin[:, None, :])], axis=-1)
        kv = pdot(rmsnorm_p(cols["ckv"], small["mla_kv_norm"][l]), big["mla_w_kv_up"][l], prox["mla_w_kv_up"][l])
        kv = kv.reshape(-1, MLA_HEADS, MLA_NOPE_DIM + MLA_V_DIM)
        k_rope = jnp.broadcast_to(_rope(cols["kr"], cos, sin)[:, None, :], (n_lat + n_ctx, MLA_HEADS, MLA_ROPE_DIM))
        k_m = jnp.concatenate([kv[..., :MLA_NOPE_DIM], k_rope], axis=-1).transpose(1, 0, 2)
        v_m = kv[..., MLA_NOPE_DIM:].transpose(1, 0, 2)
        q_m = q_m.transpose(1, 0, 2)
        out_c = jnp.concatenate([_heads_last(attention(q_m[:, :n_lat], k_m, v_m)),
                                 _heads_last(attention(q_m[:, n_lat:], k_m[:, n_lat:], v_m[:, n_lat:]))], axis=0)
        y = sum(jax.nn.sigmoid(cols[gate]) * pdot(br, big["w_branch"][l][i], prox["w_branch"][l][i])
                for i, (gate, br) in enumerate(zip(("gate_a", "gate_b", "gate_c"), (out_a, out_b, out_c))))
        xs = xs + by_row(g1) * pdot(y, big["w_out"][l], prox["w_out"][l])
        h2 = norm_mod(xs, small["norm_ffn"][l], sh2, sc2, n_lat)
        d_ff = big["ffn_w_down"][l].shape[0]
        halves = (d_ff, d_ff)
        (w_val, w_gate), (p_val, p_gate) = big["ffn_w_up"][l], prox["ffn_w_up"][l]
        cw_val, cw_gate = split_cols(small["ffn_conv_w"][l], halves)
        cb_val, cb_gate = split_cols(small["ffn_conv_b"][l][None, :], halves)
        act = ffn_act(pdot(h2, w_val, p_val), pdot(h2, w_gate, p_gate), cw_val, cw_gate, cb_val[0], cb_gate[0], n_lat)
        xs = xs + by_row(g2) * pdot(act, big["ffn_w_down"][l], prox["ffn_w_down"][l])
    y_out = rmsnorm_p(xs[:n_lat], small["norm_final"])
    return 0.5 * jnp.sum(jnp.mean(jnp.square(y_out - target), axis=-1))


WEIGHTS = ['c_ctx', 'w_mod', 'b_mod', 'norm_mix', 'norm_ffn', 'w_in', 'na_rpb', 'lru_conv_w', 'lru_conv_b', 'lru_w_a',
           'lru_b_a', 'lru_w_x', 'lru_b_x', 'lru_lam', 'mla_q_norm', 'mla_kv_norm', 'mla_w_q_up', 'mla_w_kv_up', 'w_branch',
           'w_out', 'ffn_w_up', 'ffn_conv_w', 'ffn_conv_b', 'ffn_w_down', 'norm_final']
BIG = {'w_mod': 2, 'w_in': 2, 'mla_w_q_up': 2, 'mla_w_kv_up': 2, 'w_branch': 3, 'w_out': 1, 'ffn_w_up': 2, 'ffn_w_down': 1}
SMALL_SHARDED = {'lru_conv_w': 2, 'lru_b_a': 2, 'lru_b_x': 2, 'lru_lam': 2, 'ffn_conv_w': 2}
PACK_LANES = 128
PACK_ROWS = 16


def _gathered_to_full(g, ax):
    t = jnp.moveaxis(g, 0, ax)
    return t.reshape(t.shape[:ax] + (t.shape[ax] * t.shape[ax + 1],) + t.shape[ax + 2:])


def _canon(shape):
    return (2, int(np.prod(shape[:-1])) // 2, shape[-1])


def _pack(parts, multiple):
    flat = jnp.concatenate([p.reshape(-1) for p in parts])
    n = flat.shape[0]
    padded = -(-n // multiple) * multiple
    return jnp.pad(flat, (0, padded - n))


def _unpack(flat, shapes):
    out, o = [], 0
    for s in shapes:
        n = int(np.prod(s))
        out.append(flat[o:o + n].reshape(s))
        o += n
    return out


def kernel(x, c, ctx, c_ctx, w_mod, b_mod, norm_mix, norm_ffn, w_in, na_rpb, lru_conv_w, lru_conv_b, lru_w_a, lru_b_a, lru_w_x, lru_b_x, lru_lam, mla_q_norm, mla_kv_norm, mla_w_q_up, mla_w_kv_up, w_branch, w_out, ffn_w_up, ffn_conv_w, ffn_conv_b, ffn_w_down, norm_final, loss_target, m_c_ctx, m_w_mod, m_b_mod, m_norm_mix, m_norm_ffn, m_w_in, m_na_rpb, m_lru_conv_w, m_lru_conv_b, m_lru_w_a, m_lru_b_a, m_lru_w_x, m_lru_b_x, m_lru_lam, m_mla_q_norm, m_mla_kv_norm, m_mla_w_q_up, m_mla_w_kv_up, m_w_branch, m_w_out, m_ffn_w_up, m_ffn_conv_w, m_ffn_conv_b, m_ffn_w_down, m_norm_final, v_c_ctx, v_w_mod, v_b_mod, v_norm_mix, v_norm_ffn, v_w_in, v_na_rpb, v_lru_conv_w, v_lru_conv_b, v_lru_w_a, v_lru_b_a, v_lru_w_x, v_lru_b_x, v_lru_lam, v_mla_q_norm, v_mla_kv_norm, v_mla_w_q_up, v_mla_w_kv_up, v_w_branch, v_w_out, v_ffn_w_up, v_ffn_conv_w, v_ffn_conv_b, v_ffn_w_down, v_norm_final):
    args = dict(locals())
    w = {n: args[n] for n in WEIGHTS}
    m = {n: args["m_" + n] for n in WEIGHTS}
    v = {n: args["v_" + n] for n in WEIGHTS}
    me = 2 * lax.axis_index("x") + lax.axis_index("y")

    d_model = x.shape[-1]
    _, head, total, padded = _w_in_layout(d_model, NA_HEADS * NA_HEAD_DIM, lru_conv_b.shape[1], mla_q_norm.shape[1], mla_kv_norm.shape[1])
    full, big = {}, {}
    for name, ax in BIG.items():
        shard = w[name]
        g = all_gather_chips(shard.astype(BF16).reshape(_canon(shard.shape)), "ag_" + name).reshape((N_CHIPS,) + shard.shape)
        per_layer = [[g[s, l] for s in range(N_CHIPS)] for l in range(DEPTH)]
        if name == "w_in":
            big[name] = [_assemble_w_in(t, head, total, padded) for t in per_layer]
        elif name == "ffn_w_up":
            big[name] = [tuple(jnp.concatenate(t[i:i + N_CHIPS // 2], axis=1) for i in (0, N_CHIPS // 2)) for t in per_layer]
        elif name == "w_branch":
            big[name] = [[jnp.concatenate([sh[i] for sh in t], axis=1) for i in range(N_BRANCH)] for t in per_layer]
        else:
            big[name] = [jnp.concatenate(t, axis=ax - 1) for t in per_layer]
    packed = _pack([w[n] for n in SMALL_SHARDED], 2 * PACK_ROWS * PACK_LANES).reshape(2, -1, PACK_LANES)
    g = all_gather_chips(packed, "ag_small").reshape(N_CHIPS, -1)
    for name, t in zip(SMALL_SHARDED, zip(*[_unpack(g[s], [w[n].shape for n in SMALL_SHARDED]) for s in range(N_CHIPS)])):
        full[name] = _gathered_to_full(jnp.stack(t), SMALL_SHARDED[name])
    small = {n: full.get(n, w[n]) for n in WEIGHTS if n not in BIG}

    prox = jax.tree.map(lambda t: jnp.zeros(t.shape, F32), {n: t for n, t in big.items() if n != "w_mod"})
    silu_c = jax.nn.silu(c)
    probe = jnp.zeros((DEPTH, 2, N_MOD * d_model), F32)
    loss, (g_prox, g_small, g_mod, g_x) = jax.value_and_grad(_forward_loss, argnums=(1, 2, 3, 4))(
        big, prox, small, probe, x[0], ctx[0], silu_c, loss_target[0])
    loss = lax.psum(loss, ("x", "y", "c"))

    def shard_grad(name, l, s):
        g, n4 = g_prox[name][l], w[name].shape[BIG[name]]
        if name == "w_in":
            return _w_in_shard_grad(g, s, n4, head, total)
        if name == "ffn_w_up":
            half = N_CHIPS // 2
            return lax.slice_in_dim(g[s // half], (s % half) * n4, (s % half + 1) * n4, axis=1)
        if name == "w_branch":
            return jnp.stack([t[:, s * n4:(s + 1) * n4] for t in g])
        return lax.slice_in_dim(g, s * n4, (s + 1) * n4, axis=BIG[name] - 1)

    grads = {}
    for name in BIG:
        if name == "w_mod":
            continue
        stacked = jnp.stack([jnp.stack([shard_grad(name, l, s) for l in range(DEPTH)]) for s in range(N_CHIPS)])
        red = reduce_scatter_devices(stacked.reshape((N_CHIPS,) + _canon(w[name].shape)), name)
        grads[name] = red.reshape(w[name].shape)

    mine = _pack([silu_c, jax.nn.silu(c_ctx), g_mod], PACK_ROWS * PACK_LANES).reshape(-1, PACK_LANES)
    rows = all_gather_chips(_rs_join_halves(mine, "ag_mod_pair"), "ag_mod").reshape(2 * N_CHIPS, -1)
    u_all = jnp.pad(rows[:, :2 * d_model].reshape(-1, d_model), ((0, MOD_ROWS - 4 * N_CHIPS), (0, 0)))
    dm_all = rows[:, 2 * d_model:2 * d_model + g_mod.size].reshape(2 * N_CHIPS, DEPTH, 2, N_MOD * d_model)
    n4 = w_mod.shape[2]
    grads["w_mod"] = jnp.stack([
        _mm_acc(u_all, jnp.pad(lax.dynamic_slice_in_dim(dm_all[:, l].reshape(4 * N_CHIPS, -1), me * n4, n4, axis=1),
                               ((0, MOD_ROWS - 4 * N_CHIPS), (0, 0))), ta=True, name="mm_wgrad_mod") for l in range(DEPTH)])
    small_names = [n for n in WEIGHTS if n not in BIG]
    packed = _pack([g_small[n] for n in small_names], N_CHIPS * 2 * PACK_ROWS * PACK_LANES).reshape(N_CHIPS, 2, -1, PACK_LANES)
    red = all_gather_chips(reduce_scatter_devices(packed, "small"), "ag_small_grads").reshape(-1)
    for name, t in zip(small_names, _unpack(red, [g_small[n].shape for n in small_names])):
        if name in SMALL_SHARDED:
            ax = SMALL_SHARDED[name]
            t = lax.dynamic_slice_in_dim(t, me * w[name].shape[ax], w[name].shape[ax], axis=ax)
        grads[name] = t

    upd = {n: adamw(w[n], grads[n], m[n], v[n]) for n in WEIGHTS}
    return (loss, g_x[None], *[grads[n] for n in WEIGHTS], *[upd[n][0] for n in WEIGHTS],
            *[upd[n][1] for n in WEIGHTS], *[upd[n][2] for n in WEIGHTS])
```
